```python
import math
import jax, jax.numpy as jnp
from jax import lax
import numpy as np

D_MODEL = 2048
BATCH = 4
SEQ = 2048
DEPTH = 2
DEC_BATCH = 16
DEC_SEQ = 16
PAST_LEN = 4096

CHUNK = 64
Q_BLOCK = 128
HEAD_DIM = 64
N_BRANCH = 4
BRANCH_WIDTH = D_MODEL // N_BRANCH
H_A = BRANCH_WIDTH // (2 * HEAD_DIM)
H_B = BRANCH_WIDTH // HEAD_DIM
H_C = BRANCH_WIDTH // HEAD_DIM
H_D = BRANCH_WIDTH // HEAD_DIM
LEFT_CHUNKS = 8
BAND = (LEFT_CHUNKS + 1) * CHUNK
REL_CLIP = 128
ROPE_THETA = 10000.0
FORGET_BIAS = 3.0
N_GROUPS = 4
EXPERTS_PER_GROUP = 4
N_EXPERTS = N_GROUPS * EXPERTS_PER_GROUP
TOP_K_IN_GROUP = 2
D_EXPERT = D_MODEL // 4
RMS_EPS = 1e-6
NEG_INF = -1e30
N_QKV_COLS = 12 * BRANCH_WIDTH
N_IN = N_QKV_COLS + H_C + N_BRANCH * D_MODEL

kernel_name = 'hybrid_streaming_encoder_step'


def _rmsnorm(x, g):
    xf = x.astype(jnp.float32)
    y = xf * lax.rsqrt(jnp.mean(xf * xf, axis=-1, keepdims=True) + RMS_EPS)
    return (y * g.astype(jnp.float32)).astype(x.dtype)


def _rope(x, pos):
    d = x.shape[-1]
    half = d // 2
    inv_freq = jnp.exp(jnp.arange(half, dtype=jnp.float32) * (-2.0 * math.log(ROPE_THETA) / d))
    ang = pos.astype(jnp.float32)[:, None] * inv_freq[None, :]
    cos = jnp.cos(ang)[None, :, None, :]
    sin = jnp.sin(ang)[None, :, None, :]
    xf = x.astype(jnp.float32)
    x1, x2 = xf[..., :half], xf[..., half:]
    return jnp.concatenate([x1 * cos - x2 * sin, x1 * sin + x2 * cos], axis=-1).astype(x.dtype)


def _sweep_queries(fn, qs, qpos):
    sq = qpos.shape[0]
    if sq > Q_BLOCK and sq % Q_BLOCK == 0:
        nb = sq // Q_BLOCK
        split = lambda a: jnp.moveaxis(a.reshape(a.shape[0], nb, Q_BLOCK, *a.shape[2:]), 1, 0)
        blocks = (tuple(split(a) for a in qs), qpos.reshape(nb, Q_BLOCK))
        out = lax.map(lambda args: fn(args[0], args[1]), blocks)
        out = jnp.moveaxis(out, 0, 1)
        return out.reshape(out.shape[0], sq, *out.shape[3:])
    return fn(qs, qpos)


def _diff_attention(q, k, v, qpos, kpos, lam):
    scale = HEAD_DIM ** -0.5
    kchunk = kpos // CHUNK
    def blk(qs, pb):
        (qb,) = qs
        sc = jnp.einsum('bqhmd,bkhmd->bhmqk', qb, k).astype(jnp.float32) * scale
        mask = kchunk[None, :] <= (pb // CHUNK)[:, None]
        p = jax.nn.softmax(jnp.where(mask, sc, NEG_INF), axis=-1)
        w = p[:, :, 0] - lam * p[:, :, 1]
        return jnp.einsum('bhqk,bkhe->bqhe', w.astype(v.dtype), v)
    return _sweep_queries(blk, (q,), qpos)


def _band_attention_prompt(q, k, v, rel_bias):
    b, s, h, d = q.shape
    nc = s // CHUNK
    scale = d ** -0.5
    qc = q.reshape(b, nc, CHUNK, h, d)
    pad = ((0, 0), (LEFT_CHUNKS, 0), (0, 0), (0, 0), (0, 0))
    kp = jnp.pad(k.reshape(b, nc, CHUNK, h, d), pad)
    vp = jnp.pad(v.reshape(b, nc, CHUNK, h, d), pad)
    kband = jnp.concatenate([kp[:, j:j + nc] for j in range(LEFT_CHUNKS + 1)], axis=2)
    vband = jnp.concatenate([vp[:, j:j + nc] for j in range(LEFT_CHUNKS + 1)], axis=2)
    key_chunk = jnp.arange(nc)[:, None] - LEFT_CHUNKS + (jnp.arange(BAND) // CHUNK)[None, :]
    valid = key_chunk >= 0
    rel = (LEFT_CHUNKS * CHUNK + jnp.arange(CHUNK))[:, None] - jnp.arange(BAND)[None, :]
    bias = rel_bias[:, jnp.clip(rel, -REL_CLIP, REL_CLIP) + REL_CLIP].astype(jnp.float32)
    sc = jnp.einsum('bcqhd,bckhd->bhcqk', qc, kband).astype(jnp.float32) * scale + bias[None, :, None]
    p = jax.nn.softmax(jnp.where(valid[None, None, :, None, :], sc, NEG_INF), axis=-1)
    o = jnp.einsum('bhcqk,bckhd->bcqhd', p.astype(v.dtype), vband)
    return o.reshape(b, s, h, d)


def _band_attention_step(q, k, v, qpos, kpos, rel_bias):
    scale = q.shape[-1] ** -0.5
    qc, kc = qpos // CHUNK, kpos // CHUNK
    mask = (kc[None, :] <= qc[:, None]) & (kc[None, :] >= qc[:, None] - LEFT_CHUNKS)
    rel = jnp.clip(qpos[:, None] - kpos[None, :], -REL_CLIP, REL_CLIP) + REL_CLIP
    bias = rel_bias[:, rel].astype(jnp.float32)
    sc = jnp.einsum('bqhd,bkhd->bhqk', q, k).astype(jnp.float32) * scale + bias[None]
    p = jax.nn.softmax(jnp.where(mask, sc, NEG_INF), axis=-1)
    return jnp.einsum('bhqk,bkhd->bqhd', p.astype(v.dtype), v)


def _forgetting_attention(q, k, v, cum_q, cum_k, qpos, kpos):
    scale = HEAD_DIM ** -0.5
    cum_kt = jnp.swapaxes(cum_k, 1, 2)
    def blk(qs, pb):
        qb, cqb = qs
        sc = jnp.einsum('bqhd,bkhd->bhqk', qb, k).astype(jnp.float32) * scale
        sc = sc + jnp.swapaxes(cqb, 1, 2)[..., :, None] - cum_kt[:, :, None, :]
        mask = kpos[None, :] <= pb[:, None]
        p = jax.nn.softmax(jnp.where(mask, sc, NEG_INF), axis=-1)
        return jnp.einsum('bhqk,bkhd->bqhd', p.astype(v.dtype), v)
    return _sweep_queries(blk, (q, cum_q), qpos)


def _stick_breaking(q, k, v, qpos, kpos):
    scale = HEAD_DIM ** -0.5
    def blk(qs, pb):
        (qb,) = qs
        z = jnp.einsum('bqhd,bkhd->bhqk', qb, k).astype(jnp.float32) * scale
        mask = kpos[None, :] < pb[:, None]
        log_1m = jnp.where(mask, jax.nn.log_sigmoid(-z), 0.0)
        later = lax.cumsum(log_1m, axis=3, reverse=True) - log_1m
        a = jnp.where(mask, jnp.exp(jax.nn.log_sigmoid(z) + later), 0.0)
        return jnp.einsum('bhqk,bkhd->bqhd', a.astype(v.dtype), v)
    return _sweep_queries(blk, (q,), qpos)


def _hier_moe(x, wr_g, br_g, wr_e, br_e, w_gate, w_up, w_down):
    b, s, d = x.shape
    t = x.reshape(b * s, d)
    g_logits = jnp.einsum('td,dg->tg', t, wr_g).astype(jnp.float32) + br_g.astype(jnp.float32)
    g_prob = jax.nn.softmax(g_logits, axis=-1)
    _, g_idx = lax.top_k(g_logits, 1)
    g_w = jnp.take_along_axis(g_prob, g_idx, axis=-1)
    e_logits = (jnp.einsum('td,de->te', t, wr_e).astype(jnp.float32) + br_e.astype(jnp.float32))
    e_logits = e_logits.reshape(b * s, N_GROUPS, EXPERTS_PER_GROUP)
    e_sel = jnp.take_along_axis(e_logits, g_idx[:, :, None], axis=1)[:, 0]
    top_v, top_i = lax.top_k(e_sel, TOP_K_IN_GROUP)
    top_w = jax.nn.softmax(top_v, axis=-1)
    within = jnp.sum(jax.nn.one_hot(top_i, EXPERTS_PER_GROUP, dtype=jnp.float32) * top_w[..., None], axis=1)
    group_oh = jax.nn.one_hot(g_idx[:, 0], N_GROUPS, dtype=jnp.float32)
    combine = (group_oh[:, :, None] * within[:, None, :] * g_w[:, :, None]).reshape(b * s, N_EXPERTS)
    hid = jax.nn.silu(jnp.einsum('td,edf->tef', t, w_gate)) * jnp.einsum('td,edf->tef', t, w_up)
    out = jnp.einsum('tef,efd->td', hid * combine[:, :, None].astype(hid.dtype), w_down)
    return out.reshape(b, s, d)


def _layer(x, pos, past, lw, lam_init):
    (g_mix, w_in, b_f, lam_vec, subln_w, rel_bias, w_branch, w_out,
     g_ffn, wr_g, br_g, wr_e, br_e, w_gate, w_up, w_down) = lw
    b, s, _ = x.shape
    W = BRANCH_WIDTH
    xn = _rmsnorm(x, g_mix)
    proj = jnp.einsum('bsd,dn->bsn', xn, w_in)
    seg = lambda i: proj[..., i * W:(i + 1) * W]
    aq = _rope(seg(0).reshape(b, s, 2 * H_A, HEAD_DIM), pos).reshape(b, s, H_A, 2, HEAD_DIM)
    ak = _rope(seg(1).reshape(b, s, 2 * H_A, HEAD_DIM), pos).reshape(b, s, H_A, 2, HEAD_DIM)
    av = seg(2).reshape(b, s, H_A, 2 * HEAD_DIM)
    bq, bk, bv = [seg(i).reshape(b, s, H_B, HEAD_DIM) for i in (3, 4, 5)]
    cq, ck, cv = [seg(i).reshape(b, s, H_C, HEAD_DIM) for i in (6, 7, 8)]
    dq, dk, dv = [seg(i).reshape(b, s, H_D, HEAD_DIM) for i in (9, 10, 11)]
    logf = jax.nn.log_sigmoid(proj[..., N_QKV_COLS:N_QKV_COLS + H_C].astype(jnp.float32)
                              + b_f.astype(jnp.float32))
    gates = jax.nn.sigmoid(proj[..., N_QKV_COLS + H_C:].reshape(b, s, N_BRANCH, D_MODEL))
    lv = lam_vec.astype(jnp.float32)
    lam = jnp.exp(jnp.sum(lv[0] * lv[1])) - jnp.exp(jnp.sum(lv[2] * lv[3])) + lam_init

    if past is None:
        kpos = pos
        ka, va, kc, vc, kd, vd = ak, av, ck, cv, dk, dv
        cum_k = jnp.cumsum(logf, axis=1)
        cum_q = cum_k
        o_b = _band_attention_prompt(bq, bk, bv, rel_bias)
        wb = min(LEFT_CHUNKS * CHUNK, s)
        state = (ak, av, bk[:, s - wb:], bv[:, s - wb:], ck, cv, logf.astype(x.dtype), dk, dv)
    else:
        (pa_k, pa_v, pb_k, pb_v, pc_k, pc_v, pc_f, pd_k, pd_v) = past
        p_len = pa_k.shape[1]
        kpos = jnp.arange(p_len + s, dtype=jnp.int32)
        cat = lambda c, n: jnp.concatenate([c.astype(n.dtype), n], axis=1)
        ka, va = cat(pa_k, ak), cat(pa_v, av)
        kc, vc = cat(pc_k, ck), cat(pc_v, cv)
        kd, vd = cat(pd_k, dk), cat(pd_v, dv)
        cum_k = jnp.cumsum(jnp.concatenate([pc_f.astype(jnp.float32), logf], axis=1), axis=1)
        cum_q = cum_k[:, p_len:]
        wb = pb_k.shape[1]
        kpos_b = jnp.arange(p_len - wb, p_len + s, dtype=jnp.int32)
        o_b = _band_attention_step(bq, cat(pb_k, bk), cat(pb_v, bv), pos, kpos_b, rel_bias)
        state = (ak, av, bk, bv, ck, cv, logf.astype(x.dtype), dk, dv)

    o_a = _diff_attention(aq, ka, va, pos, kpos, lam)
    o_a = _rmsnorm(o_a, subln_w) * (1.0 - lam_init)
    o_c = _forgetting_attention(cq, kc, vc, cum_q, cum_k, pos, kpos)
    o_d = _stick_breaking(dq, kd, vd, pos, kpos)
    branches = jnp.stack([o_a.reshape(b, s, W), o_b.reshape(b, s, W),
                          o_c.reshape(b, s, W), o_d.reshape(b, s, W)], axis=2)
    proj_b = jnp.einsum('bsnw,nwd->bsnd', branches, w_branch)
    merged = jnp.sum(gates * proj_b, axis=2)
    h = x + jnp.einsum('bsd,de->bse', merged, w_out)
    h = h + _hier_moe(_rmsnorm(h, g_ffn), wr_g, br_g, wr_e, br_e, w_gate, w_up, w_down)
    return h, state


def setup_inputs(seed: int = 0) -> dict:
    key = jax.random.key(seed)
    ks = jax.random.split(key, 40)
    f32 = jnp.float32
    nrm = lambda k, shape, sc: jax.random.normal(k, shape, f32) * sc
    b_cache = min(LEFT_CHUNKS * CHUNK, PAST_LEN)
    dsc = D_MODEL ** -0.5
    w_in = jnp.concatenate([
        nrm(ks[16], (DEPTH, D_MODEL, N_QKV_COLS), dsc),
        nrm(ks[17], (DEPTH, D_MODEL, H_C), 0.5 * dsc),
        nrm(ks[18], (DEPTH, D_MODEL, N_BRANCH * D_MODEL), dsc)], axis=-1)
    return {
        'x_prompt': nrm(ks[0], (BATCH, SEQ, D_MODEL), 1.0),
        'x_sample': nrm(ks[1], (DEC_BATCH, DEC_SEQ, D_MODEL), 1.0),
        'cache_a_k': nrm(ks[2], (DEPTH, DEC_BATCH, PAST_LEN, H_A, 2, HEAD_DIM), 1.0),
        'cache_a_v': nrm(ks[3], (DEPTH, DEC_BATCH, PAST_LEN, H_A, 2 * HEAD_DIM), 1.0),
        'cache_b_k': nrm(ks[4], (DEPTH, DEC_BATCH, b_cache, H_B, HEAD_DIM), 1.0),
        'cache_b_v': nrm(ks[5], (DEPTH, DEC_BATCH, b_cache, H_B, HEAD_DIM), 1.0),
        'cache_c_k': nrm(ks[6], (DEPTH, DEC_BATCH, PAST_LEN, H_C, HEAD_DIM), 1.0),
        'cache_c_v': nrm(ks[7], (DEPTH, DEC_BATCH, PAST_LEN, H_C, HEAD_DIM), 1.0),
        'cache_c_logf': jax.nn.log_sigmoid(FORGET_BIAS + nrm(ks[8], (DEPTH, DEC_BATCH, PAST_LEN, H_C), 1.0)),
        'cache_d_k': nrm(ks[9], (DEPTH, DEC_BATCH, PAST_LEN, H_D, HEAD_DIM), 1.0),
        'cache_d_v': nrm(ks[10], (DEPTH, DEC_BATCH, PAST_LEN, H_D, HEAD_DIM), 1.0),
        'norm_mix': 1.0 + nrm(ks[11], (DEPTH, D_MODEL), 0.01),
        'w_in': w_in,
        'b_forget': FORGET_BIAS + nrm(ks[12], (DEPTH, H_C), 0.1),
        'diff_lambda': nrm(ks[13], (DEPTH, 4, HEAD_DIM), 0.1),
        'diff_subln': 1.0 + nrm(ks[14], (DEPTH, 2 * HEAD_DIM), 0.01),
        'rel_bias': nrm(ks[15], (DEPTH, H_B, 2 * REL_CLIP + 1), 0.2),
        'w_branch': nrm(ks[19], (DEPTH, N_BRANCH, BRANCH_WIDTH, D_MODEL), BRANCH_WIDTH ** -0.5),
        'w_out': nrm(ks[20], (DEPTH, D_MODEL, D_MODEL), dsc),
        'norm_ffn': 1.0 + nrm(ks[21], (DEPTH, D_MODEL), 0.01),
        'router_group_w': nrm(ks[22], (DEPTH, D_MODEL, N_GROUPS), dsc),
        'router_group_b': nrm(ks[23], (DEPTH, N_GROUPS), 0.01),
        'router_expert_w': nrm(ks[24], (DEPTH, D_MODEL, N_EXPERTS), dsc),
        'router_expert_b': nrm(ks[25], (DEPTH, N_EXPERTS), 0.01),
        'expert_w_gate': nrm(ks[26], (DEPTH, N_EXPERTS, D_MODEL, D_EXPERT), dsc),
        'expert_w_up': nrm(ks[27], (DEPTH, N_EXPERTS, D_MODEL, D_EXPERT), dsc),
        'expert_w_down': nrm(ks[28], (DEPTH, N_EXPERTS, D_EXPERT, D_MODEL), D_EXPERT ** -0.5),
        'norm_final': 1.0 + nrm(ks[29], (D_MODEL,), 0.01),
    }


def reference(x_prompt, x_sample, cache_a_k, cache_a_v, cache_b_k, cache_b_v, cache_c_k, cache_c_v,
              cache_c_logf, cache_d_k, cache_d_v, norm_mix, w_in, b_forget, diff_lambda, diff_subln,
              rel_bias, w_branch, w_out, norm_ffn, router_group_w, router_group_b, router_expert_w,
              router_expert_b, expert_w_gate, expert_w_up, expert_w_down, norm_final):
    s_p = x_prompt.shape[1]
    s_s = x_sample.shape[1]
    p_len = cache_a_k.shape[2]
    pos_p = jnp.arange(s_p, dtype=jnp.int32)
    pos_s = p_len + jnp.arange(s_s, dtype=jnp.int32)
    hp, hs = x_prompt, x_sample
    st_p, st_s = [], []
    for l in range(DEPTH):
        lam_init = 0.8 - 0.6 * math.exp(-0.3 * l)
        lw = (norm_mix[l], w_in[l], b_forget[l], diff_lambda[l], diff_subln[l], rel_bias[l],
              w_branch[l], w_out[l], norm_ffn[l], router_group_w[l], router_group_b[l],
              router_expert_w[l], router_expert_b[l], expert_w_gate[l], expert_w_up[l], expert_w_down[l])
        past = (cache_a_k[l], cache_a_v[l], cache_b_k[l], cache_b_v[l], cache_c_k[l], cache_c_v[l],
                cache_c_logf[l], cache_d_k[l], cache_d_v[l])
        hp, sp = _layer(hp, pos_p, None, lw, lam_init)
        hs, ss = _layer(hs, pos_s, past, lw, lam_init)
        st_p.append(sp)
        st_s.append(ss)
    y_prompt = _rmsnorm(hp, norm_final)
    y_sample = _rmsnorm(hs, norm_final)
    stk = lambda sts, i: jnp.stack([st[i] for st in sts], axis=0)
    return (y_prompt, y_sample,
            stk(st_p, 0), stk(st_p, 1), stk(st_p, 2), stk(st_p, 3), stk(st_p, 4),
            stk(st_p, 5), stk(st_p, 6), stk(st_p, 7), stk(st_p, 8),
            stk(st_s, 0), stk(st_s, 1), stk(st_s, 2), stk(st_s, 3), stk(st_s, 4),
            stk(st_s, 5), stk(st_s, 6), stk(st_s, 7), stk(st_s, 8))
```

```python
import functools
import math

import jax
import jax.numpy as jnp
from jax import lax
from jax.experimental import pallas as pl
from jax.experimental.pallas import tpu as pltpu

F32 = jnp.float32
BF16 = jnp.bfloat16

CHUNK = 64
HEAD_DIM = 64
LEFT_CHUNKS = 8
REL_CLIP = 128
ROPE_THETA = 10000.0
N_BRANCH = 4
N_GROUPS = 4
EXPERTS_PER_GROUP = 4
N_EXPERTS = N_GROUPS * EXPERTS_PER_GROUP
RMS_EPS = 1e-6
NEG_INF = -1e30
Q_SCALE = HEAD_DIM ** -0.5
N_SEG = 12

LANES = 128
VMEM_LIMIT = 56 * 1024 * 1024

NT_DIMS = (((1,), (1,)), ((), ()))


def _cparams(sem):
    return pltpu.CompilerParams(dimension_semantics=sem, vmem_limit_bytes=VMEM_LIMIT)


def _row_tile(t, pref):
    return pref if t % pref == 0 else t


def _softplus(z):
    return jnp.maximum(z, 0.0) + jnp.log1p(jnp.exp(-jnp.abs(z)))


def _sigmoid(z):
    return 1.0 / (1.0 + jnp.exp(-z))


def _dot(a, b):
    return jnp.dot(a, b, preferred_element_type=F32)


def _dot_nt(a, b):
    return lax.dot_general(a, b, NT_DIMS, preferred_element_type=F32)


def _rms_kernel(x_ref, g_ref, o_ref):
    x = x_ref[...]
    ms = jnp.mean(x * x, axis=-1, keepdims=True)
    o_ref[...] = (x * lax.rsqrt(ms + RMS_EPS) * g_ref[...]).astype(o_ref.dtype)


def rmsnorm(x, g, out_dtype):
    t, d = x.shape
    tm = _row_tile(t, 512)
    return pl.pallas_call(
        _rms_kernel,
        out_shape=jax.ShapeDtypeStruct((t, d), out_dtype),
        grid=(t // tm,),
        in_specs=[pl.BlockSpec((tm, d), lambda m: (m, 0)),
                  pl.BlockSpec((1, d), lambda m: (0, 0))],
        out_specs=pl.BlockSpec((tm, d), lambda m: (m, 0)),
        compiler_params=_cparams(("parallel",)),
        name="rmsnorm",
    )(x, g.reshape(1, d))


def _inproj_kernel(x_ref, w_ref, cos_ref, sin_ref, o_ref):
    n = pl.program_id(0)
    acc = _dot(x_ref[...], w_ref[...])

    @pl.when(n < 2)
    def _():
        width = acc.shape[1]
        half = HEAD_DIM // 2
        lane = lax.broadcasted_iota(jnp.int32, acc.shape, 1)
        lower = (lane % HEAD_DIM) < half
        partner = jnp.where(lower, pltpu.roll(acc, width - half, 1), pltpu.roll(acc, half, 1))
        o_ref[0] = acc * cos_ref[...] + partner * sin_ref[...]

    @pl.when(n >= 2)
    def _():
        o_ref[0] = acc


def in_projection(xn, w_qkv, cos_tab, sin_tab, seq):
    t, d = xn.shape
    w = w_qkv.shape[1] // N_SEG
    tm = _row_tile(t, 512)
    if seq % tm == 0:
        n_tab = seq // tm
    else:
        assert tm % seq == 0
        cos_tab = jnp.tile(cos_tab, (tm // seq, 1))
        sin_tab = jnp.tile(sin_tab, (tm // seq, 1))
        n_tab = 1
    return pl.pallas_call(
        _inproj_kernel,
        out_shape=jax.ShapeDtypeStruct((N_SEG, t, w), F32),
        grid=(N_SEG, t // tm),
        in_specs=[pl.BlockSpec((tm, d), lambda n, m: (m, 0)),
                  pl.BlockSpec((d, w), lambda n, m: (0, n)),
                  pl.BlockSpec((tm, w), lambda n, m: (m % n_tab, 0)),
                  pl.BlockSpec((tm, w), lambda n, m: (m % n_tab, 0))],
        out_specs=pl.BlockSpec((1, tm, w), lambda n, m: (n, m, 0)),
        compiler_params=_cparams(("parallel", "parallel")),
        name="in_projection",
    )(xn, w_qkv, cos_tab, sin_tab)


def _logf_kernel(x_ref, w_ref, b_ref, o_ref):
    z = _dot(x_ref[...], w_ref[...]) + b_ref[...]
    o_ref[...] = jnp.minimum(z, 0.0) - jnp.log1p(jnp.exp(-jnp.abs(z)))


def log_forget(xn, w_f, b_f):
    t, d = xn.shape
    tm = _row_tile(t, 512)
    return pl.pallas_call(
        _logf_kernel,
        out_shape=jax.ShapeDtypeStruct((t, LANES), F32),
        grid=(t // tm,),
        in_specs=[pl.BlockSpec((tm, d), lambda m: (m, 0)),
                  pl.BlockSpec((d, LANES), lambda m: (0, 0)),
                  pl.BlockSpec((1, LANES), lambda m: (0, 0))],
        out_specs=pl.BlockSpec((tm, LANES), lambda m: (m, 0)),
        compiler_params=_cparams(("parallel",)),
        name="log_forget",
    )(xn, w_f, b_f)


def _cumsum_kernel(x_ref, o_ref):
    x = x_ref[0]
    length = x.shape[1]
    lane = lax.broadcasted_iota(jnp.int32, x.shape, 1)
    shift = 1
    while shift < length:
        x = x + jnp.where(lane >= shift, pltpu.roll(x, shift, 1), 0.0)
        shift *= 2
    o_ref[0] = x


def cumsum_lanes(x):
    b, h, length = x.shape
    return pl.pallas_call(
        _cumsum_kernel,
        out_shape=jax.ShapeDtypeStruct(x.shape, F32),
        grid=(b,),
        in_specs=[pl.BlockSpec((1, h, length), lambda i: (i, 0, 0))],
        out_specs=pl.BlockSpec((1, h, length), lambda i: (i, 0, 0)),
        compiler_params=_cparams(("parallel",)),
        name="cumsum",
    )(x)


def _diff_lambda(lam_ref, lam_init):
    lv = lam_ref[...]
    a = jnp.sum(lv[0:1] * lv[1:2], axis=1, keepdims=True)
    b = jnp.sum(lv[2:3] * lv[3:4], axis=1, keepdims=True)
    return jnp.exp(a) - jnp.exp(b) + lam_init


def _subln(o, w, lam_init):
    ms = jnp.mean(o * o, axis=-1, keepdims=True)
    return o * lax.rsqrt(ms + RMS_EPS) * w * (1.0 - lam_init)


def _online_softmax_step(s, v, m_ref, l_ref, acc_ref, j):
    m_prev = m_ref[j]
    m_new = jnp.maximum(m_prev, jnp.max(s, axis=-1, keepdims=True))
    alpha = jnp.exp(m_prev - m_new)
    p = jnp.exp(s - m_new)
    l_ref[j] = alpha * l_ref[j] + jnp.sum(p, axis=-1, keepdims=True)
    acc_ref[j] = alpha * acc_ref[j] + _dot(p.astype(BF16), v)
    m_ref[j] = m_new


def _a_prompt_kernel(q_ref, k_ref, v_ref, lam_ref, w_ref, o_ref, m_ref, l_ref, acc_ref, *, tile, lam_init):
    qi, ki = pl.program_id(2), pl.program_id(3)

    @pl.when(ki == 0)
    def _():
        m_ref[...] = jnp.full(m_ref.shape, NEG_INF, F32)
        l_ref[...] = jnp.zeros(l_ref.shape, F32)
        acc_ref[...] = jnp.zeros(acc_ref.shape, F32)

    @pl.when(ki <= qi)
    def _():
        q = (q_ref[0, 0] * Q_SCALE).astype(BF16)
        k = k_ref[0, 0].astype(BF16)
        v = v_ref[0, 0].astype(BF16)
        row = lax.broadcasted_iota(jnp.int32, (tile, tile), 0) + qi * tile
        col = lax.broadcasted_iota(jnp.int32, (tile, tile), 1) + ki * tile
        mask = (col // CHUNK) <= (row // CHUNK)
        for j in range(2):
            sl = slice(j * HEAD_DIM, (j + 1) * HEAD_DIM)
            s = jnp.where(mask, _dot_nt(q[:, sl], k[:, sl]), NEG_INF)
            _online_softmax_step(s, v, m_ref, l_ref, acc_ref, j)

    @pl.when(ki == pl.num_programs(3) - 1)
    def _():
        lam = _diff_lambda(lam_ref, lam_init)
        o = acc_ref[0] / l_ref[0] - lam * (acc_ref[1] / l_ref[1])
        o_ref[0] = _subln(o, w_ref[...], lam_init).astype(o_ref.dtype)


def _c_prompt_kernel(q_ref, k_ref, v_ref, cq_ref, ck_ref, o_ref, m_ref, l_ref, acc_ref, *, tile):
    qi, ki = pl.program_id(2), pl.program_id(3)

    @pl.when(ki == 0)
    def _():
        m_ref[...] = jnp.full(m_ref.shape, NEG_INF, F32)
        l_ref[...] = jnp.zeros(l_ref.shape, F32)
        acc_ref[...] = jnp.zeros(acc_ref.shape, F32)

    @pl.when(ki <= qi)
    def _():
        q = (q_ref[0, 0] * Q_SCALE).astype(BF16)
        k = k_ref[0, 0].astype(BF16)
        v = v_ref[0, 0].astype(BF16)
        cq = cq_ref[0, 0]
        ck = ck_ref[0, 0]
        row = lax.broadcasted_iota(jnp.int32, (tile, tile), 0) + qi * tile
        col = lax.broadcasted_iota(jnp.int32, (tile, tile), 1) + ki * tile
        mask = col <= row
        for j in range(2):
            sl = slice(j * HEAD_DIM, (j + 1) * HEAD_DIM)
            s = _dot_nt(q[:, sl], k[:, sl]) + cq[:, j:j + 1] - ck[j:j + 1, :]
            s = jnp.where(mask, s, NEG_INF)
            _online_softmax_step(s, v[:, sl], m_ref, l_ref, acc_ref, j)

    @pl.when(ki == pl.num_programs(3) - 1)
    def _():
        o = jnp.concatenate([acc_ref[0] / l_ref[0], acc_ref[1] / l_ref[1]], axis=1)
        o_ref[0] = o.astype(o_ref.dtype)


def _strict_upper_ones(n):
    r = lax.broadcasted_iota(jnp.int32, (n, n), 0)
    c = lax.broadcasted_iota(jnp.int32, (n, n), 1)
    return jnp.where(r > c, 1.0, 0.0).astype(BF16)


def _suffix_sums(lm, ones_after):
    hi = lm.astype(BF16)
    lo = (lm - hi.astype(F32)).astype(BF16)
    return _dot(hi, ones_after) + _dot(lo, ones_after)


def _stick_tile(z, mask, v, run, ones_after):
    sp = _softplus(z)
    lm = -sp if mask is None else jnp.where(mask, -sp, 0.0)
    log_a = (z - sp) + _suffix_sums(lm, ones_after) + run
    a = jnp.exp(log_a)
    if mask is not None:
        a = jnp.where(mask, a, 0.0)
    return _dot(a.astype(BF16), v), jnp.sum(lm, axis=-1, keepdims=True)


def _d_prompt_kernel(q_ref, k_ref, v_ref, o_ref, run_ref, acc_ref, *, tile):
    qi, step = pl.program_id(2), pl.program_id(3)

    @pl.when(step == 0)
    def _():
        run_ref[...] = jnp.zeros(run_ref.shape, F32)
        acc_ref[...] = jnp.zeros(acc_ref.shape, F32)

    def tile_update(diagonal):
        q = (q_ref[0, 0] * Q_SCALE).astype(BF16)
        k = k_ref[0, 0].astype(BF16)
        v = v_ref[0, 0].astype(BF16)
        ones_after = _strict_upper_ones(tile)
        mask = None
        if diagonal:
            row = lax.broadcasted_iota(jnp.int32, (tile, tile), 0)
            col = lax.broadcasted_iota(jnp.int32, (tile, tile), 1)
            mask = col < row
        for j in range(2):
            sl = slice(j * HEAD_DIM, (j + 1) * HEAD_DIM)
            z = _dot_nt(q[:, sl], k[:, sl])
            pv, lsum = _stick_tile(z, mask, v[:, sl], run_ref[j], ones_after)
            acc_ref[j] += pv
            run_ref[j] += lsum

    @pl.when(step == 0)
    def _():
        tile_update(True)

    @pl.when(jnp.logical_and(step > 0, step <= qi))
    def _():
        tile_update(False)

    @pl.when(step == pl.num_programs(3) - 1)
    def _():
        o_ref[0] = jnp.concatenate([acc_ref[0], acc_ref[1]], axis=1).astype(o_ref.dtype)


def _b_prompt_kernel(q_ref, k0_ref, k1_ref, k2_ref, v0_ref, v1_ref, v2_ref, bias_ref, o_ref, *, tile):
    qi = pl.program_id(2)
    q = (q_ref[0, 0] * Q_SCALE).astype(BF16)
    row_chunk = lax.broadcasted_iota(jnp.int32, (tile, tile), 0) // CHUNK
    col_chunk = lax.broadcasted_iota(jnp.int32, (tile, tile), 1) // CHUNK
    back = 2
    ks = [r[0, 0].astype(BF16) for r in (k0_ref, k1_ref, k2_ref)]
    vs = [r[0, 0].astype(BF16) for r in (v0_ref, v1_ref, v2_ref)]
    outs = []
    for j in range(2):
        sl = slice(j * HEAD_DIM, (j + 1) * HEAD_DIM)
        scores = []
        for c in range(3):
            dist = row_chunk - col_chunk + (back - c) * (tile // CHUNK)
            valid = (dist >= 0) & (dist <= LEFT_CHUNKS) & (qi - back + c >= 0)
            s = _dot_nt(q[:, sl], ks[c][:, sl]) + bias_ref[0, j, :, c * tile:(c + 1) * tile]
            scores.append(jnp.where(valid, s, NEG_INF))
        m = jnp.maximum(jnp.maximum(jnp.max(scores[0], axis=-1, keepdims=True),
                                    jnp.max(scores[1], axis=-1, keepdims=True)),
                        jnp.max(scores[2], axis=-1, keepdims=True))
        ps = [jnp.exp(s - m) for s in scores]
        denom = sum(jnp.sum(p, axis=-1, keepdims=True) for p in ps)
        o = sum(_dot(p.astype(BF16), v[:, sl]) for p, v in zip(ps, vs))
        outs.append(o / denom)
    o_ref[0] = jnp.concatenate(outs, axis=1).astype(o_ref.dtype)


def prompt_attention(qkv, cum_col, cum_row, band_bias, lam_vec, subln_w, lam_init, batch, seq):
    tile = 256
    assert seq % tile == 0 and tile % CHUNK == 0 and tile * 2 == LEFT_CHUNKS * CHUNK
    nq = seq // tile
    n_pair = qkv.shape[-1] // LANES
    out_shape = jax.ShapeDtypeStruct((batch, seq, qkv.shape[-1]), BF16)
    sem4 = ("parallel", "parallel", "parallel", "arbitrary")

    def seg_spec(seg, tile_of):
        return pl.BlockSpec((1, 1, tile, LANES), lambda b, h, qi, ki: (seg, b, tile_of(qi, ki), h))

    q_tile = lambda qi, ki: qi
    causal_tile = lambda qi, ki: jnp.minimum(ki, qi)
    out_spec = pl.BlockSpec((1, tile, LANES), lambda b, h, qi, ki: (b, qi, h))
    softmax_scratch = lambda width: [pltpu.VMEM((2, tile, 1), F32), pltpu.VMEM((2, tile, 1), F32),
                                     pltpu.VMEM((2, tile, width), F32)]

    o_a = pl.pallas_call(
        functools.partial(_a_prompt_kernel, tile=tile, lam_init=lam_init),
        out_shape=out_shape,
        grid=(batch, n_pair, nq, nq),
        in_specs=[seg_spec(0, q_tile), seg_spec(1, causal_tile), seg_spec(2, causal_tile),
                  pl.BlockSpec((4, HEAD_DIM), lambda b, h, qi, ki: (0, 0)),
                  pl.BlockSpec((1, LANES), lambda b, h, qi, ki: (0, 0))],
        out_specs=out_spec,
        scratch_shapes=softmax_scratch(LANES),
        compiler_params=_cparams(sem4),
        name="diff_attention",
    )(qkv, qkv, qkv, lam_vec, subln_w.reshape(1, LANES))

    o_c = pl.pallas_call(
        functools.partial(_c_prompt_kernel, tile=tile),
        out_shape=out_shape,
        grid=(batch, n_pair, nq, nq),
        in_specs=[seg_spec(6, q_tile), seg_spec(7, causal_tile), seg_spec(8, causal_tile),
                  pl.BlockSpec((1, 1, tile, 2), lambda b, h, qi, ki: (b, h, qi, 0)),
                  pl.BlockSpec((1, 1, 2, tile), lambda b, h, qi, ki: (b, h, 0, jnp.minimum(ki, qi)))],
        out_specs=out_spec,
        scratch_shapes=softmax_scratch(HEAD_DIM),
        compiler_params=_cparams(sem4),
        name="forgetting_attention",
    )(qkv, qkv, qkv, cum_col, cum_row)

    reverse_tile = lambda qi, step: jnp.maximum(qi - step, 0)
    o_d = pl.pallas_call(
        functools.partial(_d_prompt_kernel, tile=tile),
        out_shape=out_shape,
        grid=(batch, n_pair, nq, nq),
        in_specs=[seg_spec(9, q_tile), seg_spec(10, reverse_tile), seg_spec(11, reverse_tile)],
        out_specs=out_spec,
        scratch_shapes=[pltpu.VMEM((2, tile, 1), F32), pltpu.VMEM((2, tile, HEAD_DIM), F32)],
        compiler_params=_cparams(sem4),
        name="stick_breaking_attention",
    )(qkv, qkv, qkv)

    def band_spec(seg, back):
        return pl.BlockSpec((1, 1, tile, LANES), lambda b, h, qi: (seg, b, jnp.maximum(qi - back, 0), h))

    o_b = pl.pallas_call(
        functools.partial(_b_prompt_kernel, tile=tile),
        out_shape=out_shape,
        grid=(batch, n_pair, nq),
        in_specs=[pl.BlockSpec((1, 1, tile, LANES), lambda b, h, qi: (3, b, qi, h)),
                  band_spec(4, 2), band_spec(4, 1), band_spec(4, 0),
                  band_spec(5, 2), band_spec(5, 1), band_spec(5, 0),
                  pl.BlockSpec((1, 2, tile, 3 * tile), lambda b, h, qi: (h, 0, 0, 0))],
        out_specs=pl.BlockSpec((1, tile, LANES), lambda b, h, qi: (b, qi, h)),
        compiler_params=_cparams(("parallel", "parallel", "parallel")),
        name="band_attention",
    )(qkv, qkv, qkv, qkv, qkv, qkv, qkv, band_bias)
    return o_a, o_b, o_c, o_d


def _block_diag_queries(q, n_heads):
    frames, width = q.shape
    head_w = width // n_heads
    tiled = jnp.concatenate([q] * n_heads, axis=0)
    row_head = lax.broadcasted_iota(jnp.int32, tiled.shape, 0) // frames
    col_head = lax.broadcasted_iota(jnp.int32, tiled.shape, 1) // head_w
    return jnp.where(row_head == col_head, tiled * Q_SCALE, 0.0).astype(BF16)


def _gather_head_blocks(acc, frames, n_heads):
    width = acc.shape[1]
    head_w = width // n_heads
    col_head = lax.broadcasted_iota(jnp.int32, (frames, width), 1) // head_w
    out = jnp.zeros((frames, width), F32)
    for h in range(n_heads):
        out = out + jnp.where(col_head == h, acc[h * frames:(h + 1) * frames, :], 0.0)
    return out


def _rows_per_head(x, frames):
    return jnp.concatenate([jnp.broadcast_to(x[h:h + 1, :], (frames, x.shape[1]))
                            for h in range(x.shape[0])], axis=0)


def _new_rows_mask(frames, n_rows, strict):
    frame = lax.broadcasted_iota(jnp.int32, (n_rows, frames), 0) % frames
    key = lax.broadcasted_iota(jnp.int32, (n_rows, frames), 1)
    return key < frame if strict else key <= frame


def _softmax_tile(s, v, m_ref, l_ref, acc_ref):
    m_prev = m_ref[...]
    m_new = jnp.maximum(m_prev, jnp.max(s, axis=-1, keepdims=True))
    alpha = jnp.exp(m_prev - m_new)
    p = jnp.exp(s - m_new)
    l_ref[...] = alpha * l_ref[...] + jnp.sum(p, axis=-1, keepdims=True)
    acc_ref[...] = alpha * acc_ref[...] + _dot(p.astype(BF16), v)
    m_ref[...] = m_new


def _softmax_step_kernel(*refs, mode, frames, lam_init):
    if mode == "A":
        (q_ref, kc_ref, vc_ref, kn_ref, vn_ref, lam_ref, w_ref,
         o_ref, qbd_ref, m_ref, l_ref, acc_ref) = refs
    elif mode == "B":
        (q_ref, kc_ref, vc_ref, kn_ref, vn_ref, bias_c_ref, bias_n_ref,
         o_ref, qbd_ref, m_ref, l_ref, acc_ref) = refs
    else:
        (q_ref, kc_ref, vc_ref, kn_ref, vn_ref, cq_ref, ckc_ref, ckn_ref,
         o_ref, qbd_ref, m_ref, l_ref, acc_ref) = refs
    n_heads = 8
    n_rows = n_heads * frames
    kt = pl.program_id(1)

    @pl.when(kt == 0)
    def _():
        qbd_ref[...] = _block_diag_queries(q_ref[0, 0], n_heads)
        m_ref[...] = jnp.full(m_ref.shape, NEG_INF, F32)
        l_ref[...] = jnp.zeros(l_ref.shape, F32)
        acc_ref[...] = jnp.zeros(acc_ref.shape, F32)

    s = _dot_nt(qbd_ref[...], kc_ref[0, 0].astype(BF16))
    if mode == "B":
        s = s + bias_c_ref[...]
    elif mode == "C":
        s = s + cq_ref[0] - _rows_per_head(ckc_ref[0], frames)
    _softmax_tile(s, vc_ref[0, 0].astype(BF16), m_ref, l_ref, acc_ref)

    @pl.when(kt == pl.num_programs(1) - 1)
    def _():
        s = _dot_nt(qbd_ref[...], kn_ref[0, 0].astype(BF16))
        if mode == "B":
            s = s + bias_n_ref[...]
        elif mode == "C":
            s = s + cq_ref[0] - _rows_per_head(ckn_ref[0], frames)
            s = jnp.where(_new_rows_mask(frames, n_rows, strict=False), s, NEG_INF)
        _softmax_tile(s, vn_ref[0, 0].astype(BF16), m_ref, l_ref, acc_ref)
        o = acc_ref[...] / l_ref[...]
        if mode == "A":
            lam = _diff_lambda(lam_ref, lam_init)
            pieces = []
            for h in range(n_heads // 2):
                cols = slice(h * LANES, (h + 1) * LANES)
                o1 = o[(2 * h) * frames:(2 * h + 1) * frames, cols]
                o2 = o[(2 * h + 1) * frames:(2 * h + 2) * frames, cols]
                pieces.append(_subln(o1 - lam * o2, w_ref[...], lam_init))
            o_ref[0] = jnp.concatenate(pieces, axis=1).astype(o_ref.dtype)
        else:
            o_ref[0] = _gather_head_blocks(o, frames, n_heads).astype(o_ref.dtype)


def _stick_step_kernel(q_ref, kc_ref, vc_ref, kn_ref, vn_ref, ones_ref, o_ref,
                       qbd_ref, run_ref, acc_ref, *, frames):
    n_heads = 8
    n_rows = n_heads * frames
    kt = pl.program_id(1)

    @pl.when(kt == 0)
    def _():
        qbd = _block_diag_queries(q_ref[0, 0], n_heads)
        qbd_ref[...] = qbd
        z = _dot_nt(qbd, kn_ref[0, 0].astype(BF16))
        mask = _new_rows_mask(frames, n_rows, strict=True)
        pv, lsum = _stick_tile(z, mask, vn_ref[0, 0].astype(BF16), jnp.zeros((n_rows, 1), F32),
                               _strict_upper_ones(frames))
        acc_ref[...] = pv
        run_ref[...] = lsum

    z = _dot_nt(qbd_ref[...], kc_ref[0, 0].astype(BF16))
    pv, lsum = _stick_tile(z, None, vc_ref[0, 0].astype(BF16), run_ref[...], ones_ref[...])
    acc_ref[...] += pv
    run_ref[...] += lsum

    @pl.when(kt == pl.num_programs(1) - 1)
    def _():
        o_ref[0] = _gather_head_blocks(acc_ref[...], frames, n_heads).astype(o_ref.dtype)


def step_attention(layer, qkv, caches, cum_q_col, cum_cache_row, cum_new_row, bias_cache, bias_new,
                   lam_vec, subln_w, lam_init):
    _, batch, frames, width = qkv.shape
    n_rows = 8 * frames
    out_shape = jax.ShapeDtypeStruct((batch, frames, width), BF16)
    sem = ("parallel", "arbitrary")

    def new_spec(seg):
        return pl.BlockSpec((1, 1, frames, width), lambda b, kt: (seg, b, 0, 0))

    def cache_spec(tk, tile_of):
        return pl.BlockSpec((1, 1, tk, width), lambda b, kt: (layer, b, tile_of(kt), 0))

    out_spec = pl.BlockSpec((1, frames, width), lambda b, kt: (b, 0, 0))
    soft_scratch = [pltpu.VMEM((n_rows, width), BF16), pltpu.VMEM((n_rows, 1), F32),
                    pltpu.VMEM((n_rows, 1), F32), pltpu.VMEM((n_rows, width), F32)]
    forward = lambda kt: kt

    def softmax_call(mode, q_seg, kc, vc, extra_args, extra_specs, name):
        past = kc.shape[2]
        tk = min(past, 1024)
        assert past % tk == 0
        return pl.pallas_call(
            functools.partial(_softmax_step_kernel, mode=mode, frames=frames, lam_init=lam_init),
            out_shape=out_shape,
            grid=(batch, past // tk),
            in_specs=[new_spec(q_seg), cache_spec(tk, forward), cache_spec(tk, forward),
                      new_spec(q_seg + 1), new_spec(q_seg + 2)] + extra_specs(tk),
            out_specs=out_spec,
            scratch_shapes=soft_scratch,
            compiler_params=_cparams(sem),
            name=name,
        )(qkv, kc, vc, qkv, qkv, *extra_args)

    o_a = softmax_call(
        "A", 0, caches["a_k"], caches["a_v"], (lam_vec, subln_w.reshape(1, LANES)),
        lambda tk: [pl.BlockSpec((4, HEAD_DIM), lambda b, kt: (0, 0)),
                    pl.BlockSpec((1, LANES), lambda b, kt: (0, 0))],
        "diff_attention_step")
    o_b = softmax_call(
        "B", 3, caches["b_k"], caches["b_v"], (bias_cache, bias_new),
        lambda tk: [pl.BlockSpec((n_rows, tk), lambda b, kt: (0, kt)),
                    pl.BlockSpec((n_rows, frames), lambda b, kt: (0, 0))],
        "band_attention_step")
    o_c = softmax_call(
        "C", 6, caches["c_k"], caches["c_v"], (cum_q_col, cum_cache_row, cum_new_row),
        lambda tk: [pl.BlockSpec((1, n_rows, 1), lambda b, kt: (b, 0, 0)),
                    pl.BlockSpec((1, 8, tk), lambda b, kt: (b, 0, kt)),
                    pl.BlockSpec((1, 8, frames), lambda b, kt: (b, 0, 0))],
        "forgetting_attention_step")

    past = caches["d_k"].shape[2]
    tk = min(past, 512)
    assert past % tk == 0
    n_kt = past // tk
    backward = lambda kt: n_kt - 1 - kt
    ones_after = jnp.tril(jnp.ones((tk, tk), F32), -1).astype(BF16)
    o_d = pl.pallas_call(
        functools.partial(_stick_step_kernel, frames=frames),
        out_shape=out_shape,
        grid=(batch, n_kt),
        in_specs=[new_spec(9), cache_spec(tk, backward), cache_spec(tk, backward),
                  new_spec(10), new_spec(11),
                  pl.BlockSpec((tk, tk), lambda b, kt: (0, 0))],
        out_specs=out_spec,
        scratch_shapes=[pltpu.VMEM((n_rows, width), BF16), pltpu.VMEM((n_rows, 1), F32),
                        pltpu.VMEM((n_rows, width), F32)],
        compiler_params=_cparams(sem),
        name="stick_breaking_attention_step",
    )(qkv, caches["d_k"], caches["d_v"], qkv, qkv, ones_after)
    return o_a, o_b, o_c, o_d


def _merge_kernel(x_ref, g0_ref, g1_ref, g2_ref, g3_ref, b0_ref, b1_ref, b2_ref, b3_ref, wb_ref, o_ref):
    x = x_ref[...]
    acc = None
    for g_ref, b_ref, n in zip((g0_ref, g1_ref, g2_ref, g3_ref), (b0_ref, b1_ref, b2_ref, b3_ref), range(4)):
        term = _sigmoid(_dot(x, g_ref[...])) * _dot(b_ref[...], wb_ref[n])
        acc = term if acc is None else acc + term
    o_ref[...] = acc.astype(o_ref.dtype)


def gated_merge(xn, w_gate, branches, w_branch):
    t, d = xn.shape
    wcol = 512
    n_col = d // wcol
    tm = _row_tile(t, 512)
    bw = branches[0].shape[1]
    gate_specs = [pl.BlockSpec((d, wcol), functools.partial(lambda c, m, n: (0, n * n_col + c), n=n))
                  for n in range(N_BRANCH)]
    branch_specs = [pl.BlockSpec((tm, bw), lambda c, m: (m, 0)) for _ in range(N_BRANCH)]
    return pl.pallas_call(
        _merge_kernel,
        out_shape=jax.ShapeDtypeStruct((t, d), BF16),
        grid=(n_col, t // tm),
        in_specs=[pl.BlockSpec((tm, d), lambda c, m: (m, 0))] + gate_specs + branch_specs
                 + [pl.BlockSpec((N_BRANCH, bw, wcol), lambda c, m: (0, 0, c))],
        out_specs=pl.BlockSpec((tm, wcol), lambda c, m: (m, c)),
        compiler_params=_cparams(("parallel", "parallel")),
        name="gated_merge",
    )(xn, w_gate, w_gate, w_gate, w_gate, *branches, w_branch)


def _outproj_kernel(a_ref, w_ref, x_ref, o_ref):
    o_ref[...] = x_ref[...] + _dot(a_ref[...], w_ref[...])


def out_projection(merged, w_out, x):
    t, d = x.shape
    tm = _row_tile(t, 512)
    return pl.pallas_call(
        _outproj_kernel,
        out_shape=jax.ShapeDtypeStruct((t, d), F32),
        grid=(t // tm,),
        in_specs=[pl.BlockSpec((tm, d), lambda m: (m, 0)),
                  pl.BlockSpec((d, d), lambda m: (0, 0)),
                  pl.BlockSpec((tm, d), lambda m: (m, 0))],
        out_specs=pl.BlockSpec((tm, d), lambda m: (m, 0)),
        compiler_params=_cparams(("parallel",)),
        name="out_projection",
    )(merged, w_out, x)


def _router_kernel(h_ref, g_ref, w_ref, b_ref, xn_ref, comb_ref):
    h = h_ref[...]
    ms = jnp.mean(h * h, axis=-1, keepdims=True)
    xn = h * lax.rsqrt(ms + RMS_EPS) * g_ref[...]
    xn_ref[...] = xn.astype(xn_ref.dtype)
    logits = jnp.dot(xn, w_ref[...], preferred_element_type=F32, precision=lax.Precision.HIGHEST) + b_ref[...]
    lane = lax.broadcasted_iota(jnp.int32, logits.shape, 1).astype(F32)
    far = float(LANES)

    def first_argmax(vals):
        top = jnp.max(vals, axis=-1, keepdims=True)
        idx = jnp.min(jnp.where(vals == top, lane, far), axis=-1, keepdims=True)
        return top, idx

    group_logits = jnp.where(lane < N_GROUPS, logits, -jnp.inf)
    g_top, g_idx = first_argmax(group_logits)
    g_weight = 1.0 / jnp.sum(jnp.exp(group_logits - g_top), axis=-1, keepdims=True)
    first = N_GROUPS + EXPERTS_PER_GROUP * g_idx
    in_group = (lane >= first) & (lane < first + EXPERTS_PER_GROUP)
    expert_logits = jnp.where(in_group, logits, -jnp.inf)
    top1, idx1 = first_argmax(expert_logits)
    top2, idx2 = first_argmax(jnp.where(lane == idx1, -jnp.inf, expert_logits))
    w1 = 1.0 / (1.0 + jnp.exp(top2 - top1))
    w2 = jnp.exp(top2 - top1) * w1
    comb_ref[...] = g_weight * (jnp.where(lane == idx1, w1, 0.0) + jnp.where(lane == idx2, w2, 0.0))


def router(h, g_ffn, w_router, b_router):
    t, d = h.shape
    tm = _row_tile(t, 256)
    return pl.pallas_call(
        _router_kernel,
        out_shape=(jax.ShapeDtypeStruct((t, d), BF16), jax.ShapeDtypeStruct((t, LANES), F32)),
        grid=(t // tm,),
        in_specs=[pl.BlockSpec((tm, d), lambda m: (m, 0)),
                  pl.BlockSpec((1, d), lambda m: (0, 0)),
                  pl.BlockSpec((d, LANES), lambda m: (0, 0)),
                  pl.BlockSpec((1, LANES), lambda m: (0, 0))],
        out_specs=(pl.BlockSpec((tm, d), lambda m: (m, 0)),
                   pl.BlockSpec((tm, LANES), lambda m: (m, 0))),
        compiler_params=_cparams(("parallel",)),
        name="router",
    )(h, g_ffn.reshape(1, d), w_router, b_router)


def _moe_kernel(x_ref, comb_ref, wg_ref, wu_ref, wd_ref, h_ref, o_ref):
    e = pl.program_id(1)

    @pl.when(e == 0)
    def _():
        o_ref[...] = h_ref[...]

    x = x_ref[...]
    comb = comb_ref[...]
    lane = lax.broadcasted_iota(jnp.int32, comb.shape, 1)
    weight = jnp.sum(jnp.where(lane == N_GROUPS + e, comb, 0.0), axis=-1, keepdims=True)
    gate = _dot(x, wg_ref[0])
    hid = gate * _sigmoid(gate) * _dot(x, wu_ref[0]) * weight
    o_ref[...] += _dot(hid.astype(BF16), wd_ref[0])


def moe(xn, comb, w_gate, w_up, w_down, h):
    t, d = xn.shape
    n_exp, _, f = w_gate.shape
    tm = _row_tile(t, 512)
    return pl.pallas_call(
        _moe_kernel,
        out_shape=jax.ShapeDtypeStruct((t, d), F32),
        grid=(t // tm, n_exp),
        in_specs=[pl.BlockSpec((tm, d), lambda m, e: (m, 0)),
                  pl.BlockSpec((tm, LANES), lambda m, e: (m, 0)),
                  pl.BlockSpec((1, d, f), lambda m, e: (e, 0, 0)),
                  pl.BlockSpec((1, d, f), lambda m, e: (e, 0, 0)),
                  pl.BlockSpec((1, f, d), lambda m, e: (e, 0, 0)),
                  pl.BlockSpec((tm, d), lambda m, e: (m, 0))],
        out_specs=pl.BlockSpec((tm, d), lambda m, e: (m, 0)),
        compiler_params=_cparams(("parallel", "arbitrary")),
        name="moe",
    )(xn, comb, w_gate, w_up, w_down, h)


def _rope_tables(pos, width):
    half = HEAD_DIM // 2
    inv_freq = jnp.exp(jnp.arange(half, dtype=F32) * (-2.0 * math.log(ROPE_THETA) / HEAD_DIM))
    ang = pos.astype(F32)[:, None] * inv_freq[None, :]
    reps = width // half
    cos = jnp.tile(jnp.cos(ang), (1, reps))
    sin = jnp.tile(jnp.sin(ang), (1, reps))
    lower = (jnp.arange(width) % HEAD_DIM) < half
    return cos, jnp.where(lower[None, :], -sin, sin)


def _rel_bias_table(rel_bias, qpos, kpos):
    rel = jnp.clip(qpos[:, None] - kpos[None, :], -REL_CLIP, REL_CLIP) + REL_CLIP
    return rel_bias[:, rel].astype(F32)


def _layer_weights(l, p):
    n_qkv = N_SEG * 512
    w_in = p["w_in"][l]
    n_fg = w_in.shape[1] - n_qkv - N_BRANCH * w_in.shape[0]
    w_f = jnp.pad(w_in[:, n_qkv:n_qkv + n_fg], ((0, 0), (0, LANES - n_fg))).astype(BF16)
    b_f = jnp.pad(p["b_forget"][l], (0, LANES - n_fg)).reshape(1, LANES)
    w_router = jnp.pad(jnp.concatenate([p["router_group_w"][l], p["router_expert_w"][l]], axis=1),
                       ((0, 0), (0, LANES - N_GROUPS - N_EXPERTS)))
    b_router = jnp.pad(jnp.concatenate([p["router_group_b"][l], p["router_expert_b"][l]]),
                       (0, LANES - N_GROUPS - N_EXPERTS)).reshape(1, LANES)
    return dict(
        w_qkv=w_in[:, :n_qkv].astype(BF16), w_f=w_f, b_f=b_f, n_fg=n_fg,
        w_gate=w_in[:, n_qkv + n_fg:].astype(BF16),
        w_branch=p["w_branch"][l].astype(BF16), w_out=p["w_out"][l].astype(BF16),
        w_router=w_router, b_router=b_router,
        e_gate=p["expert_w_gate"][l].astype(BF16), e_up=p["expert_w_up"][l].astype(BF16),
        e_down=p["expert_w_down"][l].astype(BF16),
    )


def _ffn(h, l, p, lw):
    xn2, comb = router(h, p["norm_ffn"][l], lw["w_router"], lw["b_router"])
    return moe(xn2, comb, lw["e_gate"], lw["e_up"], lw["e_down"], h)


def _prompt_layer(x, l, p, lw, lam_init, batch, seq, tables):
    t, d = x.shape
    xn = rmsnorm(x, p["norm_mix"][l], BF16)
    qkv = in_projection(xn, lw["w_qkv"], tables["cos"], tables["sin"], seq)
    width = qkv.shape[-1]
    n_fg = lw["n_fg"]
    logf = log_forget(xn, lw["w_f"], lw["b_f"])[:, :n_fg].reshape(batch, seq, n_fg)
    cum_row = cumsum_lanes(jnp.swapaxes(logf, 1, 2))
    cum_col = jnp.swapaxes(cum_row.reshape(batch, n_fg // 2, 2, seq), 2, 3)
    band_bias = tables["band_bias"][l]
    branches = prompt_attention(
        qkv.reshape(N_SEG, batch, seq, width), cum_col, cum_row.reshape(batch, n_fg // 2, 2, seq),
        band_bias, p["diff_lambda"][l], p["diff_subln"][l], lam_init, batch, seq)
    merged = gated_merge(xn, lw["w_gate"], [b.reshape(t, width) for b in branches], lw["w_branch"])
    h = out_projection(merged, lw["w_out"], x)
    h = _ffn(h, l, p, lw)
    seg = lambda i: qkv[i].reshape(batch, seq, width)
    wb = min(LEFT_CHUNKS * CHUNK, seq)
    state = (seg(1), seg(2), seg(4)[:, seq - wb:], seg(5)[:, seq - wb:], seg(7), seg(8), logf, seg(10), seg(11))
    return h, state


def _step_layer(x, l, p, lw, lam_init, batch, frames, caches, tables):
    t, d = x.shape
    xn = rmsnorm(x, p["norm_mix"][l], BF16)
    qkv = in_projection(xn, lw["w_qkv"], tables["cos"], tables["sin"], frames)
    width = qkv.shape[-1]
    n_fg = lw["n_fg"]
    logf = log_forget(xn, lw["w_f"], lw["b_f"])[:, :n_fg].reshape(batch, frames, n_fg)
    past_f = caches["c_logf"][l]
    past = past_f.shape[1]
    total = past + frames
    padded = -(-total // LANES) * LANES
    series = jnp.concatenate([past_f, logf], axis=1)
    cum = cumsum_lanes(jnp.pad(jnp.swapaxes(series, 1, 2), ((0, 0), (0, 0), (0, padded - total))))
    cum_cache_row = cum[:, :, :past]
    cum_new_row = cum[:, :, past:total]
    cum_q_col = cum_new_row.reshape(batch, n_fg * frames, 1)
    branches = step_attention(
        l, qkv.reshape(N_SEG, batch, frames, width), caches, cum_q_col, cum_cache_row, cum_new_row,
        tables["bias_cache"][l], tables["bias_new"][l], p["diff_lambda"][l], p["diff_subln"][l], lam_init)
    merged = gated_merge(xn, lw["w_gate"], [b.reshape(t, width) for b in branches], lw["w_branch"])
    h = out_projection(merged, lw["w_out"], x)
    h = _ffn(h, l, p, lw)
    seg = lambda i: qkv[i].reshape(batch, frames, width)
    state = (seg(1), seg(2), seg(4), seg(5), seg(7), seg(8), logf, seg(10), seg(11))
    return h, state


def kernel(x_prompt, x_sample, cache_a_k, cache_a_v, cache_b_k, cache_b_v, cache_c_k, cache_c_v, cache_c_logf, cache_d_k, cache_d_v, norm_mix, w_in, b_forget, diff_lambda, diff_subln, rel_bias, w_branch, w_out, norm_ffn, router_group_w, router_group_b, router_expert_w, router_expert_b, expert_w_gate, expert_w_up, expert_w_down, norm_final):
    p = dict(norm_mix=norm_mix, w_in=w_in, b_forget=b_forget, diff_lambda=diff_lambda, diff_subln=diff_subln,
             w_branch=w_branch, w_out=w_out, norm_ffn=norm_ffn, router_group_w=router_group_w,
             router_group_b=router_group_b, router_expert_w=router_expert_w, router_expert_b=router_expert_b,
             expert_w_gate=expert_w_gate, expert_w_up=expert_w_up, expert_w_down=expert_w_down)
    depth = w_in.shape[0]
    bp, sp, d = x_prompt.shape
    bs, fs, _ = x_sample.shape
    past = cache_a_k.shape[2]
    width = 512
    n_heads = width // HEAD_DIM

    flat = lambda c: c.reshape(c.shape[0], c.shape[1], c.shape[2], -1)
    caches = dict(a_k=flat(cache_a_k), a_v=flat(cache_a_v), b_k=flat(cache_b_k), b_v=flat(cache_b_v),
                  c_k=flat(cache_c_k), c_v=flat(cache_c_v), c_logf=cache_c_logf,
                  d_k=flat(cache_d_k), d_v=flat(cache_d_v))

    pos_p = jnp.arange(sp, dtype=jnp.int32)
    pos_s = past + jnp.arange(fs, dtype=jnp.int32)
    cos_p, sin_p = _rope_tables(pos_p, width)
    cos_s, sin_s = _rope_tables(pos_s, width)
    tile = 256
    band_q = LEFT_CHUNKS * CHUNK + jnp.arange(tile, dtype=jnp.int32)
    band_k = jnp.arange(3 * tile, dtype=jnp.int32)
    band_bias = jnp.stack([_rel_bias_table(rel_bias[l], band_q, band_k).reshape(n_heads // 2, 2, tile, 3 * tile)
                           for l in range(depth)])
    wb = cache_b_k.shape[2]
    kpos_b = jnp.arange(past - wb, past + fs, dtype=jnp.int32)
    step_bias = jnp.stack([_rel_bias_table(rel_bias[l], pos_s, kpos_b).reshape(n_heads * fs, wb + fs)
                           for l in range(depth)])
    tables_p = dict(cos=cos_p, sin=sin_p, band_bias=band_bias)
    tables_s = dict(cos=cos_s, sin=sin_s, bias_cache=step_bias[:, :, :wb], bias_new=step_bias[:, :, wb:])

    hp = x_prompt.reshape(bp * sp, d)
    hs = x_sample.reshape(bs * fs, d)
    st_p, st_s = [], []
    for l in range(depth):
        lam_init = 0.8 - 0.6 * math.exp(-0.3 * l)
        lw = _layer_weights(l, p)
        hp, state_p = _prompt_layer(hp, l, p, lw, lam_init, bp, sp, tables_p)
        hs, state_s = _step_layer(hs, l, p, lw, lam_init, bs, fs, caches, tables_s)
        st_p.append(state_p)
        st_s.append(state_s)
    y_prompt = rmsnorm(hp, norm_final, F32).reshape(bp, sp, d)
    y_sample = rmsnorm(hs, norm_final, F32).reshape(bs, fs, d)

    h_a = width // (2 * HEAD_DIM)
    state_shapes = lambda b, s, sb: (
        (b, s, h_a, 2, HEAD_DIM), (b, s, h_a, 2 * HEAD_DIM), (b, sb, n_heads, HEAD_DIM), (b, sb, n_heads, HEAD_DIM),
        (b, s, n_heads, HEAD_DIM), (b, s, n_heads, HEAD_DIM), (b, s, n_heads), (b, s, n_heads, HEAD_DIM),
        (b, s, n_heads, HEAD_DIM))

    def stack(states, shapes):
        return tuple(jnp.stack([st[i].reshape(shapes[i]) for st in states], axis=0) for i in range(len(shapes)))

    out_p = stack(st_p, state_shapes(bp, sp, min(LEFT_CHUNKS * CHUNK, sp)))
    out_s = stack(st_s, state_shapes(bs, fs, fs))
    return (y_prompt, y_sample) + out_p + out_s
```

```python
import functools
import math

import jax
import jax.numpy as jnp
from jax import lax
from jax.experimental import pallas as pl
from jax.experimental.pallas import tpu as pltpu

F32 = jnp.float32
BF16 = jnp.bfloat16

CHUNK = 64
HEAD_DIM = 64
LEFT_CHUNKS = 8
REL_CLIP = 128
ROPE_THETA = 10000.0
N_BRANCH = 4
N_GROUPS = 4
EXPERTS_PER_GROUP = 4
N_EXPERTS = N_GROUPS * EXPERTS_PER_GROUP
RMS_EPS = 1e-6
NEG_INF = -1e30
Q_SCALE = HEAD_DIM ** -0.5
N_SEG = 12

LANES = 128
VMEM_LIMIT = 56 * 1024 * 1024

NT_DIMS = (((1,), (1,)), ((), ()))


def _cparams(sem):
    return pltpu.CompilerParams(dimension_semantics=sem, vmem_limit_bytes=VMEM_LIMIT)


def _row_tile(t, pref):
    return pref if t % pref == 0 else t


def _sigmoid(z):
    return 1.0 / (1.0 + jnp.exp(-z))


def _dot(a, b):
    return jnp.dot(a, b, preferred_element_type=F32)


def _dot_nt(a, b):
    return lax.dot_general(a, b, NT_DIMS, preferred_element_type=F32)


def _rms_kernel(x_ref, g_ref, o_ref):
    x = x_ref[...]
    ms = jnp.mean(x * x, axis=-1, keepdims=True)
    o_ref[...] = (x * lax.rsqrt(ms + RMS_EPS) * g_ref[...]).astype(o_ref.dtype)


def rmsnorm(x, g, out_dtype):
    t, d = x.shape
    tm = _row_tile(t, 512)
    return pl.pallas_call(
        _rms_kernel,
        out_shape=jax.ShapeDtypeStruct((t, d), out_dtype),
        grid=(t // tm,),
        in_specs=[pl.BlockSpec((tm, d), lambda m: (m, 0)),
                  pl.BlockSpec((1, d), lambda m: (0, 0))],
        out_specs=pl.BlockSpec((tm, d), lambda m: (m, 0)),
        compiler_params=_cparams(("parallel",)),
        name="rmsnorm",
    )(x, g.reshape(1, d))


def _inproj_kernel(x_ref, w_ref, cos_ref, sin_ref, o_ref, ob_ref):
    n = pl.program_id(0)
    acc = _dot(x_ref[...], w_ref[...])
    q_scale = jnp.where(n % 3 == 0, Q_SCALE, 1.0)

    @pl.when(n < 2)
    def _():
        width = acc.shape[1]
        half = HEAD_DIM // 2
        lane = lax.broadcasted_iota(jnp.int32, acc.shape, 1)
        lower = (lane % HEAD_DIM) < half
        partner = jnp.where(lower, pltpu.roll(acc, width - half, 1), pltpu.roll(acc, half, 1))
        roped = acc * cos_ref[...] + partner * sin_ref[...]
        o_ref[0] = roped
        ob_ref[0] = (roped * q_scale).astype(ob_ref.dtype)

    @pl.when(n >= 2)
    def _():
        o_ref[0] = acc
        ob_ref[0] = (acc * q_scale).astype(ob_ref.dtype)


def in_projection(xn, w_qkv, cos_tab, sin_tab, seq):
    t, d = xn.shape
    w = w_qkv.shape[1] // N_SEG
    tm = _row_tile(t, 512)
    if seq % tm == 0:
        n_tab = seq // tm
    else:
        assert tm % seq == 0
        cos_tab = jnp.tile(cos_tab, (tm // seq, 1))
        sin_tab = jnp.tile(sin_tab, (tm // seq, 1))
        n_tab = 1
    return pl.pallas_call(
        _inproj_kernel,
        out_shape=(jax.ShapeDtypeStruct((N_SEG, t, w), F32), jax.ShapeDtypeStruct((N_SEG, t, w), BF16)),
        grid=(N_SEG, t // tm),
        in_specs=[pl.BlockSpec((tm, d), lambda n, m: (m, 0)),
                  pl.BlockSpec((d, w), lambda n, m: (0, n)),
                  pl.BlockSpec((tm, w), lambda n, m: (m % n_tab, 0)),
                  pl.BlockSpec((tm, w), lambda n, m: (m % n_tab, 0))],
        out_specs=(pl.BlockSpec((1, tm, w), lambda n, m: (n, m, 0)),
                   pl.BlockSpec((1, tm, w), lambda n, m: (n, m, 0))),
        compiler_params=_cparams(("parallel", "parallel")),
        name="in_projection",
    )(xn, w_qkv, cos_tab, sin_tab)


def _logf_kernel(x_ref, w_ref, b_ref, o_ref):
    z = _dot(x_ref[...], w_ref[...]) + b_ref[...]
    o_ref[...] = jnp.minimum(z, 0.0) - jnp.log1p(jnp.exp(-jnp.abs(z)))


def log_forget(xn, w_f, b_f):
    t, d = xn.shape
    tm = _row_tile(t, 512)
    return pl.pallas_call(
        _logf_kernel,
        out_shape=jax.ShapeDtypeStruct((t, LANES), F32),
        grid=(t // tm,),
        in_specs=[pl.BlockSpec((tm, d), lambda m: (m, 0)),
                  pl.BlockSpec((d, LANES), lambda m: (0, 0)),
                  pl.BlockSpec((1, LANES), lambda m: (0, 0))],
        out_specs=pl.BlockSpec((tm, LANES), lambda m: (m, 0)),
        compiler_params=_cparams(("parallel",)),
        name="log_forget",
    )(xn, w_f, b_f)


def _cumsum_kernel(x_ref, o_ref):
    x = x_ref[0]
    length = x.shape[1]
    lane = lax.broadcasted_iota(jnp.int32, x.shape, 1)
    shift = 1
    while shift < length:
        x = x + jnp.where(lane >= shift, pltpu.roll(x, shift, 1), 0.0)
        shift *= 2
    o_ref[0] = x


def cumsum_lanes(x):
    b, h, length = x.shape
    return pl.pallas_call(
        _cumsum_kernel,
        out_shape=jax.ShapeDtypeStruct(x.shape, F32),
        grid=(b,),
        in_specs=[pl.BlockSpec((1, h, length), lambda i: (i, 0, 0))],
        out_specs=pl.BlockSpec((1, h, length), lambda i: (i, 0, 0)),
        compiler_params=_cparams(("parallel",)),
        name="cumsum",
    )(x)


def _diff_lambda(lam_ref, lam_init):
    lv = lam_ref[...]
    a = jnp.sum(lv[0:1] * lv[1:2], axis=1, keepdims=True)
    b = jnp.sum(lv[2:3] * lv[3:4], axis=1, keepdims=True)
    return jnp.exp(a) - jnp.exp(b) + lam_init


def _subln(o, w, lam_init):
    ms = jnp.mean(o * o, axis=-1, keepdims=True)
    return o * lax.rsqrt(ms + RMS_EPS) * w * (1.0 - lam_init)


def _head_lanes(shape, j):
    lane = lax.broadcasted_iota(jnp.int32, shape, 1)
    return (lane >= j * HEAD_DIM) & (lane < (j + 1) * HEAD_DIM)


def _split3(x):
    hi = x.astype(BF16).astype(F32)
    mid = (x - hi).astype(BF16).astype(F32)
    lo = (x - hi - mid).astype(BF16).astype(F32)
    return hi, mid, lo


def _with_bias_columns(x, j, cols):
    lane = lax.broadcasted_iota(jnp.int32, x.shape, 1)
    out = jnp.where(_head_lanes(x.shape, j), x.astype(F32), 0.0)
    base = (1 - j) * HEAD_DIM
    for i, c in enumerate(cols):
        out = jnp.where(lane == base + i, c, out)
    return out.astype(BF16)


def _softmax_tile_update(state, s, v):
    m, l, acc = state
    m_new = jnp.maximum(m, jnp.max(s, axis=-1, keepdims=True))
    alpha = jnp.exp(m - m_new)
    p = jnp.exp(s - m_new)
    return (m_new, alpha * l + jnp.sum(p, axis=-1, keepdims=True), alpha * acc + _dot(p.astype(BF16), v))


def _causal_softmax_sweep(qs, k_tile, v_tile, diag_mask, qi, tile):
    init = tuple((jnp.full((tile, 1), NEG_INF, F32), jnp.zeros((tile, 1), F32), jnp.zeros((tile, LANES), F32))
                 for _ in qs)

    def update(states, start, mask):
        v = v_tile(start)
        out = []
        for j, q in enumerate(qs):
            s = _dot_nt(q, k_tile(j, start))
            if mask is not None:
                s = jnp.where(mask, s, NEG_INF)
            out.append(_softmax_tile_update(states[j], s, v))
        return tuple(out)

    states = lax.fori_loop(0, qi, lambda i, st: update(st, pl.multiple_of(i * tile, tile), None), init)
    return update(states, pl.multiple_of(qi * tile, tile), diag_mask)


def _a_prompt_kernel(q_ref, k_ref, v_ref, lam_ref, w_ref, o_ref, *, tile, lam_init):
    qi = pl.program_id(2)
    q = q_ref[0, 0]
    qs = [jnp.where(_head_lanes(q.shape, j), q, jnp.zeros_like(q)) for j in range(2)]
    row = lax.broadcasted_iota(jnp.int32, (tile, tile), 0)
    col = lax.broadcasted_iota(jnp.int32, (tile, tile), 1)
    diag_mask = (col // CHUNK) <= (row // CHUNK)
    k_tile = lambda j, start: k_ref[0, 0, pl.ds(start, tile), :]
    v_tile = lambda start: v_ref[0, 0, pl.ds(start, tile), :]
    (_, l1, acc1), (_, l2, acc2) = _causal_softmax_sweep(qs, k_tile, v_tile, diag_mask, qi, tile)
    lam = _diff_lambda(lam_ref, lam_init)
    o = acc1 / l1 - lam * (acc2 / l2)
    o_ref[0] = _subln(o, w_ref[...], lam_init).astype(o_ref.dtype)


def _c_prompt_kernel(q_ref, k_ref, v_ref, cq_ref, ck_ref, o_ref, kb_ref, *, tile):
    qi = pl.program_id(2)
    one = jnp.float32(1.0)

    @pl.when(qi == 0)
    def _():
        k = k_ref[0, 0]
        ck = ck_ref[0, 0]
        for j in range(2):
            hi, mid, lo = _split3(-ck[:, j:j + 1])
            kb_ref[j] = _with_bias_columns(k, j, [one, one, one, hi, mid, lo])

    q = q_ref[0, 0]
    cq = cq_ref[0, 0]
    qs = []
    for j in range(2):
        hi, mid, lo = _split3(cq[:, j:j + 1])
        qs.append(_with_bias_columns(q, j, [hi, mid, lo, one, one, one]))
    row = lax.broadcasted_iota(jnp.int32, (tile, tile), 0)
    col = lax.broadcasted_iota(jnp.int32, (tile, tile), 1)
    k_tile = lambda j, start: kb_ref[j, pl.ds(start, tile), :]
    v_tile = lambda start: v_ref[0, 0, pl.ds(start, tile), :]
    (_, l1, acc1), (_, l2, acc2) = _causal_softmax_sweep(qs, k_tile, v_tile, col <= row, qi, tile)
    o_ref[0] = jnp.where(_head_lanes(acc1.shape, 0), acc1 / l1, acc2 / l2).astype(o_ref.dtype)


SUFFIX_BLOCK = 256


def _strict_upper_ones(n):
    r = lax.broadcasted_iota(jnp.int32, (n, n), 0)
    c = lax.broadcasted_iota(jnp.int32, (n, n), 1)
    return jnp.where(r > c, 1.0, 0.0).astype(BF16)


def _suffix_sums(lm, ones_after):
    hi = lm.astype(BF16)
    lo = (lm - hi.astype(F32)).astype(BF16)
    return _dot(hi, ones_after) + _dot(lo, ones_after)


def _stick_tile(z, mask, v, run, ones_after):
    sp = jnp.maximum(z, 0.0) + jnp.log(1.0 + jnp.exp(-jnp.abs(z)))
    lm = -sp if mask is None else jnp.where(mask, -sp, 0.0)
    block = ones_after.shape[0]
    n_block = z.shape[1] // block
    later = [None] * n_block
    for c in reversed(range(n_block)):
        part = lm[:, c * block:(c + 1) * block]
        later[c] = _suffix_sums(part, ones_after) + run
        run = run + jnp.sum(part, axis=-1, keepdims=True)
    later = later[0] if n_block == 1 else jnp.concatenate(later, axis=1)
    a = jnp.exp((z - sp) + later)
    if mask is not None:
        a = jnp.where(mask, a, 0.0)
    return _dot(a.astype(BF16), v), run


def _d_prompt_kernel(q_ref, k_ref, v_ref, o_ref, *, tile):
    qi = pl.program_id(2)
    q = q_ref[0, 0]
    qs = [jnp.where(_head_lanes(q.shape, j), q, jnp.zeros_like(q)) for j in range(2)]
    ones_after = _strict_upper_ones(SUFFIX_BLOCK)
    row = lax.broadcasted_iota(jnp.int32, (tile, tile), 0)
    col = lax.broadcasted_iota(jnp.int32, (tile, tile), 1)

    def update(states, start, mask):
        k = k_ref[0, 0, pl.ds(start, tile), :]
        v = v_ref[0, 0, pl.ds(start, tile), :]
        out = []
        for j, q_j in enumerate(qs):
            run, acc = states[j]
            pv, run = _stick_tile(_dot_nt(q_j, k), mask, v, run, ones_after)
            out.append((run, acc + pv))
        return tuple(out)

    init = tuple((jnp.zeros((tile, 1), F32), jnp.zeros((tile, LANES), F32)) for _ in qs)
    states = update(init, pl.multiple_of(qi * tile, tile), col < row)
    states = lax.fori_loop(
        0, qi, lambda i, st: update(st, pl.multiple_of((qi - 1 - i) * tile, tile), None), states)
    o_ref[0] = jnp.where(_head_lanes((tile, LANES), 0), states[0][1], states[1][1]).astype(o_ref.dtype)


def _b_prompt_kernel(q_ref, k0_ref, k1_ref, k2_ref, v0_ref, v1_ref, v2_ref, bias_ref, o_ref, *, tile):
    qi = pl.program_id(2)
    q = q_ref[0, 0]
    row_chunk = lax.broadcasted_iota(jnp.int32, (tile, tile), 0) // CHUNK
    col_chunk = lax.broadcasted_iota(jnp.int32, (tile, tile), 1) // CHUNK
    back = 2
    ks = [r[0, 0] for r in (k0_ref, k1_ref, k2_ref)]
    vs = [r[0, 0] for r in (v0_ref, v1_ref, v2_ref)]
    outs = []
    for j in range(2):
        q_j = jnp.where(_head_lanes(q.shape, j), q, jnp.zeros_like(q))
        scores = []
        for c in range(3):
            dist = row_chunk - col_chunk + (back - c) * (tile // CHUNK)
            valid = (dist >= 0) & (dist <= LEFT_CHUNKS) & (qi - back + c >= 0)
            s = _dot_nt(q_j, ks[c]) + bias_ref[0, j, :, c * tile:(c + 1) * tile]
            scores.append(jnp.where(valid, s, NEG_INF))
        m = jnp.maximum(jnp.maximum(jnp.max(scores[0], axis=-1, keepdims=True),
                                    jnp.max(scores[1], axis=-1, keepdims=True)),
                        jnp.max(scores[2], axis=-1, keepdims=True))
        ps = [jnp.exp(s - m) for s in scores]
        denom = sum(jnp.sum(p, axis=-1, keepdims=True) for p in ps)
        o = sum(_dot(p.astype(BF16), v) for p, v in zip(ps, vs))
        outs.append(o / denom)
    o_ref[0] = jnp.where(_head_lanes(outs[0].shape, 0), outs[0], outs[1]).astype(o_ref.dtype)


def prompt_attention(qkv, cum_col, band_bias, lam_vec, subln_w, lam_init, batch, seq):
    width = qkv.shape[-1]
    n_pair = width // LANES
    out_shape = jax.ShapeDtypeStruct((batch, seq, width), BF16)
    tile = 512
    assert seq % tile == 0 and tile % CHUNK == 0
    sem = ("parallel", "parallel", "arbitrary")

    q_spec = lambda seg: pl.BlockSpec((1, 1, tile, LANES), lambda b, h, qi: (seg, b, qi, h))
    full_spec = lambda seg: pl.BlockSpec((1, 1, seq, LANES), lambda b, h, qi: (seg, b, 0, h))
    out_spec = pl.BlockSpec((1, tile, LANES), lambda b, h, qi: (b, qi, h))
    grid = (batch, n_pair, seq // tile)

    o_a = pl.pallas_call(
        functools.partial(_a_prompt_kernel, tile=tile, lam_init=lam_init),
        out_shape=out_shape, grid=grid,
        in_specs=[q_spec(0), full_spec(1), full_spec(2),
                  pl.BlockSpec((4, HEAD_DIM), lambda b, h, qi: (0, 0)),
                  pl.BlockSpec((1, LANES), lambda b, h, qi: (0, 0))],
        out_specs=out_spec,
        compiler_params=_cparams(sem),
        name="diff_attention",
    )(qkv, qkv, qkv, lam_vec, subln_w.reshape(1, LANES))

    o_c = pl.pallas_call(
        functools.partial(_c_prompt_kernel, tile=tile),
        out_shape=out_shape, grid=grid,
        in_specs=[q_spec(6), full_spec(7), full_spec(8),
                  pl.BlockSpec((1, 1, tile, 2), lambda b, h, qi: (b, h, qi, 0)),
                  pl.BlockSpec((1, 1, seq, 2), lambda b, h, qi: (b, h, 0, 0))],
        out_specs=out_spec,
        scratch_shapes=[pltpu.VMEM((2, seq, LANES), BF16)],
        compiler_params=_cparams(sem),
        name="forgetting_attention",
    )(qkv, qkv, qkv, cum_col, cum_col)

    o_d = pl.pallas_call(
        functools.partial(_d_prompt_kernel, tile=tile),
        out_shape=out_shape, grid=grid,
        in_specs=[q_spec(9), full_spec(10), full_spec(11)],
        out_specs=out_spec,
        compiler_params=_cparams(sem),
        name="stick_breaking_attention",
    )(qkv, qkv, qkv)

    band_tile = 256
    assert seq % band_tile == 0 and band_tile * 2 == LEFT_CHUNKS * CHUNK

    def band_spec(seg, back):
        return pl.BlockSpec((1, 1, band_tile, LANES), lambda b, h, qi: (seg, b, jnp.maximum(qi - back, 0), h))

    o_b = pl.pallas_call(
        functools.partial(_b_prompt_kernel, tile=band_tile),
        out_shape=out_shape,
        grid=(batch, n_pair, seq // band_tile),
        in_specs=[band_spec(3, 0),
                  band_spec(4, 2), band_spec(4, 1), band_spec(4, 0),
                  band_spec(5, 2), band_spec(5, 1), band_spec(5, 0),
                  pl.BlockSpec((1, 2, band_tile, 3 * band_tile), lambda b, h, qi: (h, 0, 0, 0))],
        out_specs=pl.BlockSpec((1, band_tile, LANES), lambda b, h, qi: (b, qi, h)),
        compiler_params=_cparams(("parallel", "parallel", "parallel")),
        name="band_attention",
    )(qkv, qkv, qkv, qkv, qkv, qkv, qkv, band_bias)
    return o_a, o_b, o_c, o_d


def _block_diag_queries(q, n_heads):
    frames, width = q.shape
    head_w = width // n_heads
    tiled = jnp.concatenate([q] * n_heads, axis=0)
    row_head = lax.broadcasted_iota(jnp.int32, tiled.shape, 0) // frames
    col_head = lax.broadcasted_iota(jnp.int32, tiled.shape, 1) // head_w
    return jnp.where(row_head == col_head, tiled * Q_SCALE, 0.0).astype(BF16)


def _gather_head_blocks(acc, frames, n_heads):
    width = acc.shape[1]
    head_w = width // n_heads
    col_head = lax.broadcasted_iota(jnp.int32, (frames, width), 1) // head_w
    out = jnp.zeros((frames, width), F32)
    for h in range(n_heads):
        out = out + jnp.where(col_head == h, acc[h * frames:(h + 1) * frames, :], 0.0)
    return out


def _rows_per_head(x, frames):
    return jnp.concatenate([jnp.broadcast_to(x[h:h + 1, :], (frames, x.shape[1]))
                            for h in range(x.shape[0])], axis=0)


def _new_rows_mask(frames, n_rows, strict):
    frame = lax.broadcasted_iota(jnp.int32, (n_rows, frames), 0) % frames
    key = lax.broadcasted_iota(jnp.int32, (n_rows, frames), 1)
    return key < frame if strict else key <= frame


def _softmax_tile(s, v, m_ref, l_ref, acc_ref):
    m_prev = m_ref[...]
    m_new = jnp.maximum(m_prev, jnp.max(s, axis=-1, keepdims=True))
    alpha = jnp.exp(m_prev - m_new)
    p = jnp.exp(s - m_new)
    l_ref[...] = alpha * l_ref[...] + jnp.sum(p, axis=-1, keepdims=True)
    acc_ref[...] = alpha * acc_ref[...] + _dot(p.astype(BF16), v)
    m_ref[...] = m_new


def _softmax_step_kernel(*refs, mode, frames, lam_init):
    if mode == "A":
        (q_ref, kc_ref, vc_ref, kn_ref, vn_ref, lam_ref, w_ref,
         o_ref, qbd_ref, m_ref, l_ref, acc_ref) = refs
    elif mode == "B":
        (q_ref, kc_ref, vc_ref, kn_ref, vn_ref, bias_c_ref, bias_n_ref,
         o_ref, qbd_ref, m_ref, l_ref, acc_ref) = refs
    else:
        (q_ref, kc_ref, vc_ref, kn_ref, vn_ref, cq_ref, ckc_ref, ckn_ref,
         o_ref, qbd_ref, m_ref, l_ref, acc_ref) = refs
    n_heads = 8
    n_rows = n_heads * frames
    kt = pl.program_id(1)

    @pl.when(kt == 0)
    def _():
        qbd_ref[...] = _block_diag_queries(q_ref[0, 0], n_heads)
        m_ref[...] = jnp.full(m_ref.shape, NEG_INF, F32)
        l_ref[...] = jnp.zeros(l_ref.shape, F32)
        acc_ref[...] = jnp.zeros(acc_ref.shape, F32)

    s = _dot_nt(qbd_ref[...], kc_ref[0, 0].astype(BF16))
    if mode == "B":
        s = s + bias_c_ref[...]
    elif mode == "C":
        s = s + cq_ref[0] - _rows_per_head(ckc_ref[0], frames)
    _softmax_tile(s, vc_ref[0, 0].astype(BF16), m_ref, l_ref, acc_ref)

    @pl.when(kt == pl.num_programs(1) - 1)
    def _():
        s = _dot_nt(qbd_ref[...], kn_ref[0, 0].astype(BF16))
        if mode == "B":
            s = s + bias_n_ref[...]
        elif mode == "C":
            s = s + cq_ref[0] - _rows_per_head(ckn_ref[0], frames)
            s = jnp.where(_new_rows_mask(frames, n_rows, strict=False), s, NEG_INF)
        _softmax_tile(s, vn_ref[0, 0].astype(BF16), m_ref, l_ref, acc_ref)
        o = acc_ref[...] / l_ref[...]
        if mode == "A":
            lam = _diff_lambda(lam_ref, lam_init)
            pieces = []
            for h in range(n_heads // 2):
                cols = slice(h * LANES, (h + 1) * LANES)
                o1 = o[(2 * h) * frames:(2 * h + 1) * frames, cols]
                o2 = o[(2 * h + 1) * frames:(2 * h + 2) * frames, cols]
                pieces.append(_subln(o1 - lam * o2, w_ref[...], lam_init))
            o_ref[0] = jnp.concatenate(pieces, axis=1).astype(o_ref.dtype)
        else:
            o_ref[0] = _gather_head_blocks(o, frames, n_heads).astype(o_ref.dtype)


def _stick_step_kernel(q_ref, kc_ref, vc_ref, kn_ref, vn_ref, ones_ref, o_ref,
                       qbd_ref, run_ref, acc_ref, *, frames):
    n_heads = 8
    n_rows = n_heads * frames
    kt = pl.program_id(1)

    @pl.when(kt == 0)
    def _():
        qbd = _block_diag_queries(q_ref[0, 0], n_heads)
        qbd_ref[...] = qbd
        z = _dot_nt(qbd, kn_ref[0, 0].astype(BF16))
        mask = _new_rows_mask(frames, n_rows, strict=True)
        pv, run = _stick_tile(z, mask, vn_ref[0, 0].astype(BF16), jnp.zeros((n_rows, 1), F32),
                              _strict_upper_ones(frames))
        acc_ref[...] = pv
        run_ref[...] = run

    z = _dot_nt(qbd_ref[...], kc_ref[0, 0].astype(BF16))
    pv, run = _stick_tile(z, None, vc_ref[0, 0].astype(BF16), run_ref[...], ones_ref[...])
    acc_ref[...] += pv
    run_ref[...] = run

    @pl.when(kt == pl.num_programs(1) - 1)
    def _():
        o_ref[0] = _gather_head_blocks(acc_ref[...], frames, n_heads).astype(o_ref.dtype)


def step_attention(layer, qkv, caches, cum_q_col, cum_cache_row, cum_new_row, bias_cache, bias_new,
                   lam_vec, subln_w, lam_init):
    _, batch, frames, width = qkv.shape
    n_rows = 8 * frames
    out_shape = jax.ShapeDtypeStruct((batch, frames, width), BF16)
    sem = ("parallel", "arbitrary")

    def new_spec(seg):
        return pl.BlockSpec((1, 1, frames, width), lambda b, kt: (seg, b, 0, 0))

    def cache_spec(tk, tile_of):
        return pl.BlockSpec((1, 1, tk, width), lambda b, kt: (layer, b, tile_of(kt), 0))

    out_spec = pl.BlockSpec((1, frames, width), lambda b, kt: (b, 0, 0))
    soft_scratch = [pltpu.VMEM((n_rows, width), BF16), pltpu.VMEM((n_rows, 1), F32),
                    pltpu.VMEM((n_rows, 1), F32), pltpu.VMEM((n_rows, width), F32)]
    forward = lambda kt: kt

    def softmax_call(mode, q_seg, kc, vc, extra_args, extra_specs, name):
        past = kc.shape[2]
        tk = min(past, 1024)
        assert past % tk == 0
        return pl.pallas_call(
            functools.partial(_softmax_step_kernel, mode=mode, frames=frames, lam_init=lam_init),
            out_shape=out_shape,
            grid=(batch, past // tk),
            in_specs=[new_spec(q_seg), cache_spec(tk, forward), cache_spec(tk, forward),
                      new_spec(q_seg + 1), new_spec(q_seg + 2)] + extra_specs(tk),
            out_specs=out_spec,
            scratch_shapes=soft_scratch,
            compiler_params=_cparams(sem),
            name=name,
        )(qkv, kc, vc, qkv, qkv, *extra_args)

    o_a = softmax_call(
        "A", 0, caches["a_k"], caches["a_v"], (lam_vec, subln_w.reshape(1, LANES)),
        lambda tk: [pl.BlockSpec((4, HEAD_DIM), lambda b, kt: (0, 0)),
                    pl.BlockSpec((1, LANES), lambda b, kt: (0, 0))],
        "diff_attention_step")
    o_b = softmax_call(
        "B", 3, caches["b_k"], caches["b_v"], (bias_cache, bias_new),
        lambda tk: [pl.BlockSpec((n_rows, tk), lambda b, kt: (0, kt)),
                    pl.BlockSpec((n_rows, frames), lambda b, kt: (0, 0))],
        "band_attention_step")
    o_c = softmax_call(
        "C", 6, caches["c_k"], caches["c_v"], (cum_q_col, cum_cache_row, cum_new_row),
        lambda tk: [pl.BlockSpec((1, n_rows, 1), lambda b, kt: (b, 0, 0)),
                    pl.BlockSpec((1, 8, tk), lambda b, kt: (b, 0, kt)),
                    pl.BlockSpec((1, 8, frames), lambda b, kt: (b, 0, 0))],
        "forgetting_attention_step")

    past = caches["d_k"].shape[2]
    tk = min(past, 512)
    assert past % tk == 0
    n_kt = past // tk
    backward = lambda kt: n_kt - 1 - kt
    assert tk % SUFFIX_BLOCK == 0
    ones_after = jnp.tril(jnp.ones((SUFFIX_BLOCK, SUFFIX_BLOCK), F32), -1).astype(BF16)
    o_d = pl.pallas_call(
        functools.partial(_stick_step_kernel, frames=frames),
        out_shape=out_shape,
        grid=(batch, n_kt),
        in_specs=[new_spec(9), cache_spec(tk, backward), cache_spec(tk, backward),
                  new_spec(10), new_spec(11),
                  pl.BlockSpec((SUFFIX_BLOCK, SUFFIX_BLOCK), lambda b, kt: (0, 0))],
        out_specs=out_spec,
        scratch_shapes=[pltpu.VMEM((n_rows, width), BF16), pltpu.VMEM((n_rows, 1), F32),
                        pltpu.VMEM((n_rows, width), F32)],
        compiler_params=_cparams(sem),
        name="stick_breaking_attention_step",
    )(qkv, caches["d_k"], caches["d_v"], qkv, qkv, ones_after)
    return o_a, o_b, o_c, o_d


def _merge_kernel(x_ref, g0_ref, g1_ref, g2_ref, g3_ref, b0_ref, b1_ref, b2_ref, b3_ref, wb_ref, o_ref):
    x = x_ref[...]
    acc = None
    for g_ref, b_ref, n in zip((g0_ref, g1_ref, g2_ref, g3_ref), (b0_ref, b1_ref, b2_ref, b3_ref), range(4)):
        term = _sigmoid(_dot(x, g_ref[...])) * _dot(b_ref[...], wb_ref[n])
        acc = term if acc is None else acc + term
    o_ref[...] = acc.astype(o_ref.dtype)


def gated_merge(xn, w_gate, branches, w_branch):
    t, d = xn.shape
    wcol = 512
    n_col = d // wcol
    tm = _row_tile(t, 512)
    bw = branches[0].shape[1]
    gate_specs = [pl.BlockSpec((d, wcol), functools.partial(lambda c, m, n: (0, n * n_col + c), n=n))
                  for n in range(N_BRANCH)]
    branch_specs = [pl.BlockSpec((tm, bw), lambda c, m: (m, 0)) for _ in range(N_BRANCH)]
    return pl.pallas_call(
        _merge_kernel,
        out_shape=jax.ShapeDtypeStruct((t, d), BF16),
        grid=(n_col, t // tm),
        in_specs=[pl.BlockSpec((tm, d), lambda c, m: (m, 0))] + gate_specs + branch_specs
                 + [pl.BlockSpec((N_BRANCH, bw, wcol), lambda c, m: (0, 0, c))],
        out_specs=pl.BlockSpec((tm, wcol), lambda c, m: (m, c)),
        compiler_params=_cparams(("parallel", "parallel")),
        name="gated_merge",
    )(xn, w_gate, w_gate, w_gate, w_gate, *branches, w_branch)


def _outproj_kernel(a_ref, w_ref, x_ref, o_ref):
    o_ref[...] = x_ref[...] + _dot(a_ref[...], w_ref[...])


def out_projection(merged, w_out, x):
    t, d = x.shape
    tm = _row_tile(t, 512)
    return pl.pallas_call(
        _outproj_kernel,
        out_shape=jax.ShapeDtypeStruct((t, d), F32),
        grid=(t // tm,),
        in_specs=[pl.BlockSpec((tm, d), lambda m: (m, 0)),
                  pl.BlockSpec((d, d), lambda m: (0, 0)),
                  pl.BlockSpec((tm, d), lambda m: (m, 0))],
        out_specs=pl.BlockSpec((tm, d), lambda m: (m, 0)),
        compiler_params=_cparams(("parallel",)),
        name="out_projection",
    )(merged, w_out, x)


def _router_kernel(h_ref, g_ref, w_ref, b_ref, xn_ref, comb_ref):
    h = h_ref[...]
    ms = jnp.mean(h * h, axis=-1, keepdims=True)
    xn = h * lax.rsqrt(ms + RMS_EPS) * g_ref[...]
    xn_ref[...] = xn.astype(xn_ref.dtype)
    logits = jnp.dot(xn, w_ref[...], preferred_element_type=F32, precision=lax.Precision.HIGHEST) + b_ref[...]
    lane = lax.broadcasted_iota(jnp.int32, logits.shape, 1).astype(F32)
    far = float(LANES)

    def first_argmax(vals):
        top = jnp.max(vals, axis=-1, keepdims=True)
        idx = jnp.min(jnp.where(vals == top, lane, far), axis=-1, keepdims=True)
        return top, idx

    group_logits = jnp.where(lane < N_GROUPS, logits, -jnp.inf)
    g_top, g_idx = first_argmax(group_logits)
    g_weight = 1.0 / jnp.sum(jnp.exp(group_logits - g_top), axis=-1, keepdims=True)
    first = N_GROUPS + EXPERTS_PER_GROUP * g_idx
    in_group = (lane >= first) & (lane < first + EXPERTS_PER_GROUP)
    expert_logits = jnp.where(in_group, logits, -jnp.inf)
    top1, idx1 = first_argmax(expert_logits)
    top2, idx2 = first_argmax(jnp.where(lane == idx1, -jnp.inf, expert_logits))
    w1 = 1.0 / (1.0 + jnp.exp(top2 - top1))
    w2 = jnp.exp(top2 - top1) * w1
    comb_ref[...] = g_weight * (jnp.where(lane == idx1, w1, 0.0) + jnp.where(lane == idx2, w2, 0.0))


def router(h, g_ffn, w_router, b_router):
    t, d = h.shape
    tm = _row_tile(t, 256)
    return pl.pallas_call(
        _router_kernel,
        out_shape=(jax.ShapeDtypeStruct((t, d), BF16), jax.ShapeDtypeStruct((t, LANES), F32)),
        grid=(t // tm,),
        in_specs=[pl.BlockSpec((tm, d), lambda m: (m, 0)),
                  pl.BlockSpec((1, d), lambda m: (0, 0)),
                  pl.BlockSpec((d, LANES), lambda m: (0, 0)),
                  pl.BlockSpec((1, LANES), lambda m: (0, 0))],
        out_specs=(pl.BlockSpec((tm, d), lambda m: (m, 0)),
                   pl.BlockSpec((tm, LANES), lambda m: (m, 0))),
        compiler_params=_cparams(("parallel",)),
        name="router",
    )(h, g_ffn.reshape(1, d), w_router, b_router)


def _moe_kernel(x_ref, comb_ref, wg_ref, wu_ref, wd_ref, h_ref, o_ref):
    e = pl.program_id(1)

    @pl.when(e == 0)
    def _():
        o_ref[...] = h_ref[...]

    x = x_ref[...]
    comb = comb_ref[...]
    lane = lax.broadcasted_iota(jnp.int32, comb.shape, 1)
    weight = jnp.sum(jnp.where(lane == N_GROUPS + e, comb, 0.0), axis=-1, keepdims=True)
    gate = _dot(x, wg_ref[0])
    hid = gate * _sigmoid(gate) * _dot(x, wu_ref[0]) * weight
    o_ref[...] += _dot(hid.astype(BF16), wd_ref[0])


def moe(xn, comb, w_gate, w_up, w_down, h):
    t, d = xn.shape
    n_exp, _, f = w_gate.shape
    tm = _row_tile(t, 512)
    return pl.pallas_call(
        _moe_kernel,
        out_shape=jax.ShapeDtypeStruct((t, d), F32),
        grid=(t // tm, n_exp),
        in_specs=[pl.BlockSpec((tm, d), lambda m, e: (m, 0)),
                  pl.BlockSpec((tm, LANES), lambda m, e: (m, 0)),
                  pl.BlockSpec((1, d, f), lambda m, e: (e, 0, 0)),
                  pl.BlockSpec((1, d, f), lambda m, e: (e, 0, 0)),
                  pl.BlockSpec((1, f, d), lambda m, e: (e, 0, 0)),
                  pl.BlockSpec((tm, d), lambda m, e: (m, 0))],
        out_specs=pl.BlockSpec((tm, d), lambda m, e: (m, 0)),
        compiler_params=_cparams(("parallel", "arbitrary")),
        name="moe",
    )(xn, comb, w_gate, w_up, w_down, h)


def _rope_tables(pos, width):
    half = HEAD_DIM // 2
    inv_freq = jnp.exp(jnp.arange(half, dtype=F32) * (-2.0 * math.log(ROPE_THETA) / HEAD_DIM))
    ang = pos.astype(F32)[:, None] * inv_freq[None, :]
    reps = width // half
    cos = jnp.tile(jnp.cos(ang), (1, reps))
    sin = jnp.tile(jnp.sin(ang), (1, reps))
    lower = (jnp.arange(width) % HEAD_DIM) < half
    return cos, jnp.where(lower[None, :], -sin, sin)


def _rel_bias_table(rel_bias, qpos, kpos):
    rel = jnp.clip(qpos[:, None] - kpos[None, :], -REL_CLIP, REL_CLIP) + REL_CLIP
    return rel_bias[:, rel].astype(F32)


def _band_bias_table(rel_bias, tile):
    n = 4 * tile - 1
    rel = LEFT_CHUNKS * CHUNK + (tile - 1) - jnp.arange(n)
    vec = rel_bias[:, jnp.clip(rel, -REL_CLIP, REL_CLIP) + REL_CLIP].astype(F32)
    starts = tile - 1 - jnp.arange(tile)
    rows = jax.vmap(lambda s: lax.dynamic_slice_in_dim(vec, s, 3 * tile, axis=1))(starts)
    return jnp.swapaxes(rows, 0, 1)


def _layer_weights(l, p):
    n_qkv = N_SEG * 512
    w_in = p["w_in"][l]
    n_fg = w_in.shape[1] - n_qkv - N_BRANCH * w_in.shape[0]
    w_f = jnp.pad(w_in[:, n_qkv:n_qkv + n_fg], ((0, 0), (0, LANES - n_fg))).astype(BF16)
    b_f = jnp.pad(p["b_forget"][l], (0, LANES - n_fg)).reshape(1, LANES)
    w_router = jnp.pad(jnp.concatenate([p["router_group_w"][l], p["router_expert_w"][l]], axis=1),
                       ((0, 0), (0, LANES - N_GROUPS - N_EXPERTS)))
    b_router = jnp.pad(jnp.concatenate([p["router_group_b"][l], p["router_expert_b"][l]]),
                       (0, LANES - N_GROUPS - N_EXPERTS)).reshape(1, LANES)
    return dict(
        w_qkv=w_in[:, :n_qkv].astype(BF16), w_f=w_f, b_f=b_f, n_fg=n_fg,
        w_gate=w_in[:, n_qkv + n_fg:].astype(BF16),
        w_branch=p["w_branch"][l].astype(BF16), w_out=p["w_out"][l].astype(BF16),
        w_router=w_router, b_router=b_router,
        e_gate=p["expert_w_gate"][l].astype(BF16), e_up=p["expert_w_up"][l].astype(BF16),
        e_down=p["expert_w_down"][l].astype(BF16),
    )


def _ffn(h, l, p, lw):
    xn2, comb = router(h, p["norm_ffn"][l], lw["w_router"], lw["b_router"])
    return moe(xn2, comb, lw["e_gate"], lw["e_up"], lw["e_down"], h)


def _prompt_layer(x, l, p, lw, lam_init, batch, seq, tables):
    t, d = x.shape
    xn = rmsnorm(x, p["norm_mix"][l], BF16)
    qkv, qkv_bf = in_projection(xn, lw["w_qkv"], tables["cos"], tables["sin"], seq)
    width = qkv.shape[-1]
    n_fg = lw["n_fg"]
    logf = log_forget(xn, lw["w_f"], lw["b_f"])[:, :n_fg].reshape(batch, seq, n_fg)
    cum_row = cumsum_lanes(jnp.swapaxes(logf, 1, 2))
    cum_col = jnp.swapaxes(cum_row.reshape(batch, n_fg // 2, 2, seq), 2, 3)
    branches = prompt_attention(
        qkv_bf.reshape(N_SEG, batch, seq, width), cum_col, tables["band_bias"][l],
        p["diff_lambda"][l], p["diff_subln"][l], lam_init, batch, seq)
    merged = gated_merge(xn, lw["w_gate"], [b.reshape(t, width) for b in branches], lw["w_branch"])
    h = out_projection(merged, lw["w_out"], x)
    h = _ffn(h, l, p, lw)
    seg = lambda i: qkv[i].reshape(batch, seq, width)
    wb = min(LEFT_CHUNKS * CHUNK, seq)
    state = (seg(1), seg(2), seg(4)[:, seq - wb:], seg(5)[:, seq - wb:], seg(7), seg(8), logf, seg(10), seg(11))
    return h, state


def _step_layer(x, l, p, lw, lam_init, batch, frames, caches, tables):
    t, d = x.shape
    xn = rmsnorm(x, p["norm_mix"][l], BF16)
    qkv, _ = in_projection(xn, lw["w_qkv"], tables["cos"], tables["sin"], frames)
    width = qkv.shape[-1]
    n_fg = lw["n_fg"]
    logf = log_forget(xn, lw["w_f"], lw["b_f"])[:, :n_fg].reshape(batch, frames, n_fg)
    past_f = caches["c_logf"][l]
    past = past_f.shape[1]
    total = past + frames
    padded = -(-total // LANES) * LANES
    series = jnp.concatenate([past_f, logf], axis=1)
    cum = cumsum_lanes(jnp.pad(jnp.swapaxes(series, 1, 2), ((0, 0), (0, 0), (0, padded - total))))
    cum_cache_row = cum[:, :, :past]
    cum_new_row = cum[:, :, past:total]
    cum_q_col = cum_new_row.reshape(batch, n_fg * frames, 1)
    branches = step_attention(
        l, qkv.reshape(N_SEG, batch, frames, width), caches, cum_q_col, cum_cache_row, cum_new_row,
        tables["bias_cache"][l], tables["bias_new"][l], p["diff_lambda"][l], p["diff_subln"][l], lam_init)
    merged = gated_merge(xn, lw["w_gate"], [b.reshape(t, width) for b in branches], lw["w_branch"])
    h = out_projection(merged, lw["w_out"], x)
    h = _ffn(h, l, p, lw)
    seg = lambda i: qkv[i].reshape(batch, frames, width)
    state = (seg(1), seg(2), seg(4), seg(5), seg(7), seg(8), logf, seg(10), seg(11))
    return h, state


def kernel(x_prompt, x_sample, cache_a_k, cache_a_v, cache_b_k, cache_b_v, cache_c_k, cache_c_v, cache_c_logf, cache_d_k, cache_d_v, norm_mix, w_in, b_forget, diff_lambda, diff_subln, rel_bias, w_branch, w_out, norm_ffn, router_group_w, router_group_b, router_expert_w, router_expert_b, expert_w_gate, expert_w_up, expert_w_down, norm_final):
    p = dict(norm_mix=norm_mix, w_in=w_in, b_forget=b_forget, diff_lambda=diff_lambda, diff_subln=diff_subln,
             w_branch=w_branch, w_out=w_out, norm_ffn=norm_ffn, router_group_w=router_group_w,
             router_group_b=router_group_b, router_expert_w=router_expert_w, router_expert_b=router_expert_b,
             expert_w_gate=expert_w_gate, expert_w_up=expert_w_up, expert_w_down=expert_w_down)
    depth = w_in.shape[0]
    bp, sp, d = x_prompt.shape
    bs, fs, _ = x_sample.shape
    past = cache_a_k.shape[2]
    width = 512
    n_heads = width // HEAD_DIM

    flat = lambda c: c.reshape(c.shape[0], c.shape[1], c.shape[2], -1)
    caches = dict(a_k=flat(cache_a_k), a_v=flat(cache_a_v), b_k=flat(cache_b_k), b_v=flat(cache_b_v),
                  c_k=flat(cache_c_k), c_v=flat(cache_c_v), c_logf=cache_c_logf,
                  d_k=flat(cache_d_k), d_v=flat(cache_d_v))

    pos_p = jnp.arange(sp, dtype=jnp.int32)
    pos_s = past + jnp.arange(fs, dtype=jnp.int32)
    cos_p, sin_p = _rope_tables(pos_p, width)
    cos_s, sin_s = _rope_tables(pos_s, width)
    tile = 256
    band_bias = jnp.stack([_band_bias_table(rel_bias[l], tile).reshape(n_heads // 2, 2, tile, 3 * tile)
                           for l in range(depth)])
    wb = cache_b_k.shape[2]
    kpos_b = jnp.arange(past - wb, past + fs, dtype=jnp.int32)
    step_bias = jnp.stack([_rel_bias_table(rel_bias[l], pos_s, kpos_b).reshape(n_heads * fs, wb + fs)
                           for l in range(depth)])
    tables_p = dict(cos=cos_p, sin=sin_p, band_bias=band_bias)
    tables_s = dict(cos=cos_s, sin=sin_s, bias_cache=step_bias[:, :, :wb], bias_new=step_bias[:, :, wb:])

    hp = x_prompt.reshape(bp * sp, d)
    hs = x_sample.reshape(bs * fs, d)
    st_p, st_s = [], []
    for l in range(depth):
        lam_init = 0.8 - 0.6 * math.exp(-0.3 * l)
        lw = _layer_weights(l, p)
        hp, state_p = _prompt_layer(hp, l, p, lw, lam_init, bp, sp, tables_p)
        hs, state_s = _step_layer(hs, l, p, lw, lam_init, bs, fs, caches, tables_s)
        st_p.append(state_p)
        st_s.append(state_s)
    y_prompt = rmsnorm(hp, norm_final, F32).reshape(bp, sp, d)
    y_sample = rmsnorm(hs, norm_final, F32).reshape(bs, fs, d)

    h_a = width // (2 * HEAD_DIM)
    state_shapes = lambda b, s, sb: (
        (b, s, h_a, 2, HEAD_DIM), (b, s, h_a, 2 * HEAD_DIM), (b, sb, n_heads, HEAD_DIM), (b, sb, n_heads, HEAD_DIM),
        (b, s, n_heads, HEAD_DIM), (b, s, n_heads, HEAD_DIM), (b, s, n_heads), (b, s, n_heads, HEAD_DIM),
        (b, s, n_heads, HEAD_DIM))

    def stack(states, shapes):
        return tuple(jnp.stack([st[i].reshape(shapes[i]) for st in states], axis=0) for i in range(len(shapes)))

    out_p = stack(st_p, state_shapes(bp, sp, min(LEFT_CHUNKS * CHUNK, sp)))
    out_s = stack(st_s, state_shapes(bs, fs, fs))
    return (y_prompt, y_sample) + out_p + out_s
```

```python
import functools
import math

import jax
import jax.numpy as jnp
from jax import lax
from jax.experimental import pallas as pl
from jax.experimental.pallas import tpu as pltpu

F32 = jnp.float32
BF16 = jnp.bfloat16

CHUNK = 64
HEAD_DIM = 64
LEFT_CHUNKS = 8
REL_CLIP = 128
ROPE_THETA = 10000.0
N_BRANCH = 4
N_GROUPS = 4
EXPERTS_PER_GROUP = 4
N_EXPERTS = N_GROUPS * EXPERTS_PER_GROUP
RMS_EPS = 1e-6
NEG_INF = -1e30
Q_SCALE = HEAD_DIM ** -0.5
N_SEG = 12

LANES = 128
VMEM_LIMIT = 56 * 1024 * 1024

NT_DIMS = (((1,), (1,)), ((), ()))


def _cparams(sem):
    return pltpu.CompilerParams(dimension_semantics=sem, vmem_limit_bytes=VMEM_LIMIT)


def _row_tile(t, pref):
    return pref if t % pref == 0 else t


def _sigmoid(z):
    return 1.0 / (1.0 + jnp.exp(-z))


def _dot(a, b):
    return jnp.dot(a, b, preferred_element_type=F32)


def _dot_nt(a, b):
    return lax.dot_general(a, b, NT_DIMS, preferred_element_type=F32)


def _rms_kernel(x_ref, g_ref, o_ref):
    x = x_ref[...]
    ms = jnp.mean(x * x, axis=-1, keepdims=True)
    o_ref[...] = (x * lax.rsqrt(ms + RMS_EPS) * g_ref[...]).astype(o_ref.dtype)


def rmsnorm(x, g, out_dtype):
    t, d = x.shape
    tm = _row_tile(t, 512)
    return pl.pallas_call(
        _rms_kernel,
        out_shape=jax.ShapeDtypeStruct((t, d), out_dtype),
        grid=(t // tm,),
        in_specs=[pl.BlockSpec((tm, d), lambda m: (m, 0)),
                  pl.BlockSpec((1, d), lambda m: (0, 0))],
        out_specs=pl.BlockSpec((tm, d), lambda m: (m, 0)),
        compiler_params=_cparams(("parallel",)),
        name="rmsnorm",
    )(x, g.reshape(1, d))


def _inproj_kernel(x_ref, w_ref, cos_ref, sin_ref, o_ref, ob_ref, wb_ref):
    n = pl.program_id(0)

    @pl.when(pl.program_id(1) == 0)
    def _():
        wb_ref[...] = w_ref[0].astype(BF16)

    acc = _dot(x_ref[...], wb_ref[...])
    q_scale = jnp.where(n % 3 == 0, Q_SCALE, 1.0)

    @pl.when(n < 2)
    def _():
        width = acc.shape[1]
        half = HEAD_DIM // 2
        lane = lax.broadcasted_iota(jnp.int32, acc.shape, 1)
        lower = (lane % HEAD_DIM) < half
        partner = jnp.where(lower, pltpu.roll(acc, width - half, 1), pltpu.roll(acc, half, 1))
        roped = acc * cos_ref[...] + partner * sin_ref[...]
        o_ref[0] = roped
        ob_ref[0] = (roped * q_scale).astype(ob_ref.dtype)

    @pl.when(n >= 2)
    def _():
        o_ref[0] = acc
        ob_ref[0] = (acc * q_scale).astype(ob_ref.dtype)


def in_projection(xn, w_in, layer, cos_tab, sin_tab, seq):
    t, d = xn.shape
    w = 512
    tm = _row_tile(t, 512)
    if seq % tm == 0:
        n_tab = seq // tm
    else:
        assert tm % seq == 0
        cos_tab = jnp.tile(cos_tab, (tm // seq, 1))
        sin_tab = jnp.tile(sin_tab, (tm // seq, 1))
        n_tab = 1
    return pl.pallas_call(
        _inproj_kernel,
        out_shape=(jax.ShapeDtypeStruct((N_SEG, t, w), F32), jax.ShapeDtypeStruct((N_SEG, t, w), BF16)),
        grid=(N_SEG, t // tm),
        in_specs=[pl.BlockSpec((tm, d), lambda n, m: (m, 0)),
                  pl.BlockSpec((1, d, w), lambda n, m: (layer, 0, n)),
                  pl.BlockSpec((tm, w), lambda n, m: (m % n_tab, 0)),
                  pl.BlockSpec((tm, w), lambda n, m: (m % n_tab, 0))],
        out_specs=(pl.BlockSpec((1, tm, w), lambda n, m: (n, m, 0)),
                   pl.BlockSpec((1, tm, w), lambda n, m: (n, m, 0))),
        scratch_shapes=[pltpu.VMEM((d, w), BF16)],
        compiler_params=_cparams(("parallel", "arbitrary")),
        name="in_projection",
    )(xn, w_in, cos_tab, sin_tab)


def _logf_kernel(x_ref, w_ref, b_ref, o_ref):
    z = _dot(x_ref[...], w_ref[...]) + b_ref[...]
    o_ref[...] = jnp.minimum(z, 0.0) - jnp.log1p(jnp.exp(-jnp.abs(z)))


def log_forget(xn, w_f, b_f):
    t, d = xn.shape
    tm = _row_tile(t, 512)
    return pl.pallas_call(
        _logf_kernel,
        out_shape=jax.ShapeDtypeStruct((t, LANES), F32),
        grid=(t // tm,),
        in_specs=[pl.BlockSpec((tm, d), lambda m: (m, 0)),
                  pl.BlockSpec((d, LANES), lambda m: (0, 0)),
                  pl.BlockSpec((1, LANES), lambda m: (0, 0))],
        out_specs=pl.BlockSpec((tm, LANES), lambda m: (m, 0)),
        compiler_params=_cparams(("parallel",)),
        name="log_forget",
    )(xn, w_f, b_f)


def _cumsum_kernel(x_ref, o_ref):
    x = x_ref[0]
    length = x.shape[1]
    lane = lax.broadcasted_iota(jnp.int32, x.shape, 1)
    shift = 1
    while shift < length:
        x = x + jnp.where(lane >= shift, pltpu.roll(x, shift, 1), 0.0)
        shift *= 2
    o_ref[0] = x


def cumsum_lanes(x):
    b, h, length = x.shape
    return pl.pallas_call(
        _cumsum_kernel,
        out_shape=jax.ShapeDtypeStruct(x.shape, F32),
        grid=(b,),
        in_specs=[pl.BlockSpec((1, h, length), lambda i: (i, 0, 0))],
        out_specs=pl.BlockSpec((1, h, length), lambda i: (i, 0, 0)),
        compiler_params=_cparams(("parallel",)),
        name="cumsum",
    )(x)


def _diff_lambda(lam_ref, lam_init):
    lv = lam_ref[...]
    a = jnp.sum(lv[0:1] * lv[1:2], axis=1, keepdims=True)
    b = jnp.sum(lv[2:3] * lv[3:4], axis=1, keepdims=True)
    return jnp.exp(a) - jnp.exp(b) + lam_init


def _subln(o, w, lam_init):
    ms = jnp.mean(o * o, axis=-1, keepdims=True)
    return o * lax.rsqrt(ms + RMS_EPS) * w * (1.0 - lam_init)


def _head_lanes(shape, j):
    lane = lax.broadcasted_iota(jnp.int32, shape, 1)
    return (lane >= j * HEAD_DIM) & (lane < (j + 1) * HEAD_DIM)


def _split3(x):
    hi = x.astype(BF16).astype(F32)
    mid = (x - hi).astype(BF16).astype(F32)
    lo = (x - hi - mid).astype(BF16).astype(F32)
    return hi, mid, lo


def _with_bias_columns(x, j, cols):
    lane = lax.broadcasted_iota(jnp.int32, x.shape, 1)
    out = jnp.where(_head_lanes(x.shape, j), x.astype(F32), 0.0)
    base = (1 - j) * HEAD_DIM
    for i, c in enumerate(cols):
        out = jnp.where(lane == base + i, c, out)
    return out.astype(BF16)


def _softmax_tile_update(state, s, v):
    m, l, acc = state
    m_new = jnp.maximum(m, jnp.max(s, axis=-1, keepdims=True))
    alpha = jnp.exp(m - m_new)
    p = jnp.exp(s - m_new)
    return (m_new, alpha * l + jnp.sum(p, axis=-1, keepdims=True), alpha * acc + _dot(p.astype(BF16), v))


def _causal_softmax_sweep(qs, k_tile, v_tile, diag_mask, qi, tile):
    init = tuple((jnp.full((tile, 1), NEG_INF, F32), jnp.zeros((tile, 1), F32), jnp.zeros((tile, LANES), F32))
                 for _ in qs)

    def update(states, start, mask):
        v = v_tile(start)
        out = []
        for j, q in enumerate(qs):
            s = _dot_nt(q, k_tile(j, start))
            if mask is not None:
                s = jnp.where(mask, s, NEG_INF)
            out.append(_softmax_tile_update(states[j], s, v))
        return tuple(out)

    states = lax.fori_loop(0, qi, lambda i, st: update(st, pl.multiple_of(i * tile, tile), None), init)
    return update(states, pl.multiple_of(qi * tile, tile), diag_mask)


def _a_prompt_kernel(q_ref, k_ref, v_ref, lam_ref, w_ref, o_ref, *, tile, lam_init):
    qi = pl.program_id(2)
    q = q_ref[0, 0]
    qs = [jnp.where(_head_lanes(q.shape, j), q, jnp.zeros_like(q)) for j in range(2)]
    row = lax.broadcasted_iota(jnp.int32, (tile, tile), 0)
    col = lax.broadcasted_iota(jnp.int32, (tile, tile), 1)
    diag_mask = (col // CHUNK) <= (row // CHUNK)
    k_tile = lambda j, start: k_ref[0, 0, pl.ds(start, tile), :]
    v_tile = lambda start: v_ref[0, 0, pl.ds(start, tile), :]
    (_, l1, acc1), (_, l2, acc2) = _causal_softmax_sweep(qs, k_tile, v_tile, diag_mask, qi, tile)
    lam = _diff_lambda(lam_ref, lam_init)
    o = acc1 / l1 - lam * (acc2 / l2)
    o_ref[0] = _subln(o, w_ref[...], lam_init).astype(o_ref.dtype)


def _c_prompt_kernel(q_ref, k_ref, v_ref, cq_ref, ck_ref, o_ref, kb_ref, *, tile):
    qi = pl.program_id(2)
    one = jnp.float32(1.0)

    @pl.when(qi == 0)
    def _():
        k = k_ref[0, 0]
        ck = ck_ref[0, 0]
        for j in range(2):
            hi, mid, lo = _split3(-ck[:, j:j + 1])
            kb_ref[j] = _with_bias_columns(k, j, [one, one, one, hi, mid, lo])

    q = q_ref[0, 0]
    cq = cq_ref[0, 0]
    qs = []
    for j in range(2):
        hi, mid, lo = _split3(cq[:, j:j + 1])
        qs.append(_with_bias_columns(q, j, [hi, mid, lo, one, one, one]))
    row = lax.broadcasted_iota(jnp.int32, (tile, tile), 0)
    col = lax.broadcasted_iota(jnp.int32, (tile, tile), 1)
    k_tile = lambda j, start: kb_ref[j, pl.ds(start, tile), :]
    v_tile = lambda start: v_ref[0, 0, pl.ds(start, tile), :]
    (_, l1, acc1), (_, l2, acc2) = _causal_softmax_sweep(qs, k_tile, v_tile, col <= row, qi, tile)
    o_ref[0] = jnp.where(_head_lanes(acc1.shape, 0), acc1 / l1, acc2 / l2).astype(o_ref.dtype)


SUFFIX_BLOCK = 256


def _strict_upper_ones(n):
    r = lax.broadcasted_iota(jnp.int32, (n, n), 0)
    c = lax.broadcasted_iota(jnp.int32, (n, n), 1)
    return jnp.where(r > c, 1.0, 0.0).astype(BF16)


def _suffix_sums(lm, ones_after):
    hi = lm.astype(BF16)
    lo = (lm - hi.astype(F32)).astype(BF16)
    return _dot(hi, ones_after) + _dot(lo, ones_after)


def _stick_weights(z, mask, run, ones_after):
    sp = jnp.maximum(z, 0.0) + jnp.log(1.0 + jnp.exp(-jnp.abs(z)))
    lm = -sp if mask is None else jnp.where(mask, -sp, 0.0)
    block = ones_after.shape[0]
    n_block = z.shape[1] // block
    later = [None] * n_block
    for c in reversed(range(n_block)):
        part = lm[:, c * block:(c + 1) * block]
        later[c] = _suffix_sums(part, ones_after) + run
        run = run + jnp.sum(part, axis=-1, keepdims=True)
    later = later[0] if n_block == 1 else jnp.concatenate(later, axis=1)
    a = jnp.exp((z - sp) + later)
    if mask is not None:
        a = jnp.where(mask, a, 0.0)
    return a, run


def _d_prompt_kernel(q_ref, k_ref, v_ref, o_ref, *, tile):
    qi = pl.program_id(2)
    q = q_ref[0, 0]
    qs = [jnp.where(_head_lanes(q.shape, j), q, jnp.zeros_like(q)) for j in range(2)]
    ones_after = _strict_upper_ones(SUFFIX_BLOCK)
    row = lax.broadcasted_iota(jnp.int32, (tile, tile), 0)
    col = lax.broadcasted_iota(jnp.int32, (tile, tile), 1)

    def update(states, start, mask):
        k = k_ref[0, 0, pl.ds(start, tile), :]
        v = v_ref[0, 0, pl.ds(start, tile), :]
        out = []
        for j, q_j in enumerate(qs):
            run, acc = states[j]
            a, run = _stick_weights(_dot_nt(q_j, k), mask, run, ones_after)
            out.append((run, acc + _dot(a.astype(BF16), v)))
        return tuple(out)

    init = tuple((jnp.zeros((tile, 1), F32), jnp.zeros((tile, LANES), F32)) for _ in qs)
    states = update(init, pl.multiple_of(qi * tile, tile), col < row)
    states = lax.fori_loop(
        0, qi, lambda i, st: update(st, pl.multiple_of((qi - 1 - i) * tile, tile), None), states)
    o_ref[0] = jnp.where(_head_lanes((tile, LANES), 0), states[0][1], states[1][1]).astype(o_ref.dtype)


def _b_prompt_kernel(q_ref, k0_ref, k1_ref, k2_ref, v0_ref, v1_ref, v2_ref, bias_ref, o_ref, *, tile):
    qi = pl.program_id(2)
    q = q_ref[0, 0]
    row_chunk = lax.broadcasted_iota(jnp.int32, (tile, tile), 0) // CHUNK
    col_chunk = lax.broadcasted_iota(jnp.int32, (tile, tile), 1) // CHUNK
    back = 2
    ks = [r[0, 0] for r in (k0_ref, k1_ref, k2_ref)]
    vs = [r[0, 0] for r in (v0_ref, v1_ref, v2_ref)]
    outs = []
    for j in range(2):
        q_j = jnp.where(_head_lanes(q.shape, j), q, jnp.zeros_like(q))
        scores = []
        for c in range(3):
            dist = row_chunk - col_chunk + (back - c) * (tile // CHUNK)
            valid = (dist >= 0) & (dist <= LEFT_CHUNKS) & (qi - back + c >= 0)
            s = _dot_nt(q_j, ks[c]) + bias_ref[0, j, :, c * tile:(c + 1) * tile]
            scores.append(jnp.where(valid, s, NEG_INF))
        m = jnp.maximum(jnp.maximum(jnp.max(scores[0], axis=-1, keepdims=True),
                                    jnp.max(scores[1], axis=-1, keepdims=True)),
                        jnp.max(scores[2], axis=-1, keepdims=True))
        ps = [jnp.exp(s - m) for s in scores]
        denom = sum(jnp.sum(p, axis=-1, keepdims=True) for p in ps)
        o = sum(_dot(p.astype(BF16), v) for p, v in zip(ps, vs))
        outs.append(o / denom)
    o_ref[0] = jnp.where(_head_lanes(outs[0].shape, 0), outs[0], outs[1]).astype(o_ref.dtype)


def prompt_attention(qkv, cum_col, band_bias, lam_vec, subln_w, lam_init, batch, seq):
    width = qkv.shape[-1]
    n_pair = width // LANES
    out_shape = jax.ShapeDtypeStruct((batch, seq, width), BF16)
    tile = 512
    assert seq % tile == 0 and tile % CHUNK == 0
    sem = ("parallel", "parallel", "arbitrary")

    q_spec = lambda seg: pl.BlockSpec((1, 1, tile, LANES), lambda b, h, qi: (seg, b, qi, h))
    full_spec = lambda seg: pl.BlockSpec((1, 1, seq, LANES), lambda b, h, qi: (seg, b, 0, h))
    out_spec = pl.BlockSpec((1, tile, LANES), lambda b, h, qi: (b, qi, h))
    grid = (batch, n_pair, seq // tile)

    o_a = pl.pallas_call(
        functools.partial(_a_prompt_kernel, tile=tile, lam_init=lam_init),
        out_shape=out_shape, grid=grid,
        in_specs=[q_spec(0), full_spec(1), full_spec(2),
                  pl.BlockSpec((4, HEAD_DIM), lambda b, h, qi: (0, 0)),
                  pl.BlockSpec((1, LANES), lambda b, h, qi: (0, 0))],
        out_specs=out_spec,
        compiler_params=_cparams(sem),
        name="diff_attention",
    )(qkv, qkv, qkv, lam_vec, subln_w.reshape(1, LANES))

    o_c = pl.pallas_call(
        functools.partial(_c_prompt_kernel, tile=tile),
        out_shape=out_shape, grid=grid,
        in_specs=[q_spec(6), full_spec(7), full_spec(8),
                  pl.BlockSpec((1, 1, tile, 2), lambda b, h, qi: (b, h, qi, 0)),
                  pl.BlockSpec((1, 1, seq, 2), lambda b, h, qi: (b, h, 0, 0))],
        out_specs=out_spec,
        scratch_shapes=[pltpu.VMEM((2, seq, LANES), BF16)],
        compiler_params=_cparams(sem),
        name="forgetting_attention",
    )(qkv, qkv, qkv, cum_col, cum_col)

    o_d = pl.pallas_call(
        functools.partial(_d_prompt_kernel, tile=tile),
        out_shape=out_shape, grid=grid,
        in_specs=[q_spec(9), full_spec(10), full_spec(11)],
        out_specs=out_spec,
        compiler_params=_cparams(sem),
        name="stick_breaking_attention",
    )(qkv, qkv, qkv)

    band_tile = 256
    assert seq % band_tile == 0 and band_tile * 2 == LEFT_CHUNKS * CHUNK

    def band_spec(seg, back):
        return pl.BlockSpec((1, 1, band_tile, LANES), lambda b, h, qi: (seg, b, jnp.maximum(qi - back, 0), h))

    o_b = pl.pallas_call(
        functools.partial(_b_prompt_kernel, tile=band_tile),
        out_shape=out_shape,
        grid=(batch, n_pair, seq // band_tile),
        in_specs=[band_spec(3, 0),
                  band_spec(4, 2), band_spec(4, 1), band_spec(4, 0),
                  band_spec(5, 2), band_spec(5, 1), band_spec(5, 0),
                  pl.BlockSpec((1, 2, band_tile, 3 * band_tile), lambda b, h, qi: (h, 0, 0, 0))],
        out_specs=pl.BlockSpec((1, band_tile, LANES), lambda b, h, qi: (b, qi, h)),
        compiler_params=_cparams(("parallel", "parallel", "parallel")),
        name="band_attention",
    )(qkv, qkv, qkv, qkv, qkv, qkv, qkv, band_bias)
    return o_a, o_b, o_c, o_d


N_STEP_HEADS = 8


def _step_scores(qs_ref, kt_ref):
    rows = [_dot(qs_ref[h], kt_ref[0, 0, h].astype(BF16)) for h in range(N_STEP_HEADS)]
    return jnp.concatenate(rows, axis=0)


def _step_pv(p, v_ref, frames, v_token_major):
    pb = p.astype(BF16)
    rows = []
    for h in range(N_STEP_HEADS):
        ph = pb[h * frames:(h + 1) * frames]
        if v_token_major:
            rows.append(_dot(ph, v_ref[0, 0, :, h // 2, :].astype(BF16)))
        else:
            rows.append(_dot_nt(ph, v_ref[0, 0, h].astype(BF16)))
    return jnp.concatenate(rows, axis=0)


def _store_scaled_queries(q_ref, qs_ref):
    for h in range(N_STEP_HEADS):
        qs_ref[h] = (q_ref[0, h] * Q_SCALE).astype(BF16)


def _rows_per_head(x, frames):
    return jnp.concatenate([jnp.broadcast_to(x[h:h + 1, :], (frames, x.shape[1]))
                            for h in range(x.shape[0])], axis=0)


def _new_rows_mask(frames, n_rows, strict):
    frame = lax.broadcasted_iota(jnp.int32, (n_rows, frames), 0) % frames
    key = lax.broadcasted_iota(jnp.int32, (n_rows, frames), 1)
    return key < frame if strict else key <= frame


def _heads_to_lanes(acc, frames):
    return jnp.concatenate([acc[h * frames:(h + 1) * frames] for h in range(N_STEP_HEADS)], axis=1)


def _softmax_step_kernel(*refs, mode, frames, lam_init):
    if mode == "A":
        (q_ref, kc_ref, vc_ref, kn_ref, vn_ref, lam_ref, w_ref,
         o_ref, qs_ref, m_ref, l_ref, acc_ref) = refs
    elif mode == "B":
        (q_ref, kc_ref, vc_ref, kn_ref, vn_ref, bias_c_ref, bias_n_ref,
         o_ref, qs_ref, m_ref, l_ref, acc_ref) = refs
    else:
        (q_ref, kc_ref, vc_ref, kn_ref, vn_ref, cq_ref, ckc_ref, ckn_ref,
         o_ref, qs_ref, m_ref, l_ref, acc_ref) = refs
    n_rows = N_STEP_HEADS * frames
    token_major_v = mode == "A"
    kt = pl.program_id(1)

    @pl.when(kt == 0)
    def _():
        _store_scaled_queries(q_ref, qs_ref)
        m_ref[...] = jnp.full(m_ref.shape, NEG_INF, F32)
        l_ref[...] = jnp.zeros(l_ref.shape, F32)
        acc_ref[...] = jnp.zeros(acc_ref.shape, F32)

    def softmax_tile(s, v_ref):
        m_prev = m_ref[...]
        m_new = jnp.maximum(m_prev, jnp.max(s, axis=-1, keepdims=True))
        alpha = jnp.exp(m_prev - m_new)
        p = jnp.exp(s - m_new)
        l_ref[...] = alpha * l_ref[...] + jnp.sum(p, axis=-1, keepdims=True)
        acc_ref[...] = alpha * acc_ref[...] + _step_pv(p, v_ref, frames, token_major_v)
        m_ref[...] = m_new

    s = _step_scores(qs_ref, kc_ref)
    if mode == "B":
        s = s + bias_c_ref[...]
    elif mode == "C":
        s = s + cq_ref[0] - _rows_per_head(ckc_ref[0], frames)
    softmax_tile(s, vc_ref)

    @pl.when(kt == pl.num_programs(1) - 1)
    def _():
        s = _step_scores(qs_ref, kn_ref)
        if mode == "B":
            s = s + bias_n_ref[...]
        elif mode == "C":
            s = s + cq_ref[0] - _rows_per_head(ckn_ref[0], frames)
            s = jnp.where(_new_rows_mask(frames, n_rows, strict=False), s, NEG_INF)
        softmax_tile(s, vn_ref)
        o = acc_ref[...] / l_ref[...]
        if mode == "A":
            lam = _diff_lambda(lam_ref, lam_init)
            pieces = []
            for h in range(N_STEP_HEADS // 2):
                o1 = o[(2 * h) * frames:(2 * h + 1) * frames]
                o2 = o[(2 * h + 1) * frames:(2 * h + 2) * frames]
                pieces.append(_subln(o1 - lam * o2, w_ref[...], lam_init))
            o_ref[0] = jnp.concatenate(pieces, axis=1).astype(o_ref.dtype)
        else:
            o_ref[0] = _heads_to_lanes(o, frames).astype(o_ref.dtype)


def _stick_step_kernel(q_ref, kc_ref, vc_ref, kn_ref, vn_ref, ones_ref, o_ref,
                       qs_ref, run_ref, acc_ref, *, frames):
    n_rows = N_STEP_HEADS * frames
    kt = pl.program_id(1)

    @pl.when(kt == 0)
    def _():
        _store_scaled_queries(q_ref, qs_ref)
        z = _step_scores(qs_ref, kn_ref)
        mask = _new_rows_mask(frames, n_rows, strict=True)
        a, run = _stick_weights(z, mask, jnp.zeros((n_rows, 1), F32), _strict_upper_ones(frames))
        acc_ref[...] = _step_pv(a, vn_ref, frames, False)
        run_ref[...] = run

    z = _step_scores(qs_ref, kc_ref)
    a, run = _stick_weights(z, None, run_ref[...], ones_ref[...])
    acc_ref[...] += _step_pv(a, vc_ref, frames, False)
    run_ref[...] = run

    @pl.when(kt == pl.num_programs(1) - 1)
    def _():
        o_ref[0] = _heads_to_lanes(acc_ref[...], frames).astype(o_ref.dtype)


def step_attention(layer, new, caches, cum_q_col, cum_cache_row, cum_new_row, bias_cache, bias_new,
                   lam_vec, subln_w, lam_init):
    batch, _, frames, _ = new["a"][0].shape
    width = N_STEP_HEADS * HEAD_DIM
    n_rows = N_STEP_HEADS * frames
    out_shape = jax.ShapeDtypeStruct((batch, frames, width), BF16)
    sem = ("parallel", "arbitrary")
    out_spec = pl.BlockSpec((1, frames, width), lambda b, kt: (b, 0, 0))
    q_spec = pl.BlockSpec((1, N_STEP_HEADS, frames, HEAD_DIM), lambda b, kt: (b, 0, 0, 0))
    new_t_spec = pl.BlockSpec((1, 1, N_STEP_HEADS, HEAD_DIM, frames), lambda b, kt: (0, b, 0, 0, 0))
    forward = lambda kt: kt

    def cache_t_spec(tk, tile_of):
        return pl.BlockSpec((1, 1, N_STEP_HEADS, HEAD_DIM, tk), lambda b, kt: (layer, b, 0, 0, tile_of(kt)))

    def softmax_call(mode, branch, kc, vc, extra_args, extra_specs, name):
        past = kc.shape[-1]
        tk = min(past, 1024)
        assert past % tk == 0
        q, k_new, v_new = new[branch]
        if mode == "A":
            vc_spec = pl.BlockSpec((1, 1, tk, 4, LANES), lambda b, kt: (layer, b, kt, 0, 0))
            vn_spec = pl.BlockSpec((1, 1, frames, 4, LANES), lambda b, kt: (0, b, 0, 0, 0))
            dv = LANES
        else:
            vc_spec, vn_spec, dv = cache_t_spec(tk, forward), new_t_spec, HEAD_DIM
        return pl.pallas_call(
            functools.partial(_softmax_step_kernel, mode=mode, frames=frames, lam_init=lam_init),
            out_shape=out_shape,
            grid=(batch, past // tk),
            in_specs=[q_spec, cache_t_spec(tk, forward), vc_spec, new_t_spec, vn_spec] + extra_specs(tk),
            out_specs=out_spec,
            scratch_shapes=[pltpu.VMEM((N_STEP_HEADS, frames, HEAD_DIM), BF16), pltpu.VMEM((n_rows, 1), F32),
                            pltpu.VMEM((n_rows, 1), F32), pltpu.VMEM((n_rows, dv), F32)],
            compiler_params=_cparams(sem),
            name=name,
        )(q, kc, vc, k_new[None], v_new[None], *extra_args)

    o_a = softmax_call(
        "A", "a", caches["a_k"], caches["a_v"], (lam_vec, subln_w.reshape(1, LANES)),
        lambda tk: [pl.BlockSpec((4, HEAD_DIM), lambda b, kt: (0, 0)),
                    pl.BlockSpec((1, LANES), lambda b, kt: (0, 0))],
        "diff_attention_step")
    o_b = softmax_call(
        "B", "b", caches["b_k"], caches["b_v"], (bias_cache, bias_new),
        lambda tk: [pl.BlockSpec((n_rows, tk), lambda b, kt: (0, kt)),
                    pl.BlockSpec((n_rows, frames), lambda b, kt: (0, 0))],
        "band_attention_step")
    o_c = softmax_call(
        "C", "c", caches["c_k"], caches["c_v"], (cum_q_col, cum_cache_row, cum_new_row),
        lambda tk: [pl.BlockSpec((1, n_rows, 1), lambda b, kt: (b, 0, 0)),
                    pl.BlockSpec((1, N_STEP_HEADS, tk), lambda b, kt: (b, 0, kt)),
                    pl.BlockSpec((1, N_STEP_HEADS, frames), lambda b, kt: (b, 0, 0))],
        "forgetting_attention_step")

    past = caches["d_k"].shape[-1]
    tk = min(past, 512)
    assert past % tk == 0 and tk % SUFFIX_BLOCK == 0
    n_kt = past // tk
    backward = lambda kt: n_kt - 1 - kt
    ones_after = jnp.tril(jnp.ones((SUFFIX_BLOCK, SUFFIX_BLOCK), F32), -1).astype(BF16)
    q, k_new, v_new = new["d"]
    o_d = pl.pallas_call(
        functools.partial(_stick_step_kernel, frames=frames),
        out_shape=out_shape,
        grid=(batch, n_kt),
        in_specs=[q_spec, cache_t_spec(tk, backward), cache_t_spec(tk, backward), new_t_spec, new_t_spec,
                  pl.BlockSpec((SUFFIX_BLOCK, SUFFIX_BLOCK), lambda b, kt: (0, 0))],
        out_specs=out_spec,
        scratch_shapes=[pltpu.VMEM((N_STEP_HEADS, frames, HEAD_DIM), BF16), pltpu.VMEM((n_rows, 1), F32),
                        pltpu.VMEM((n_rows, HEAD_DIM), F32)],
        compiler_params=_cparams(sem),
        name="stick_breaking_attention_step",
    )(q, caches["d_k"], caches["d_v"], k_new[None], v_new[None], ones_after)
    return o_a, o_b, o_c, o_d


def _merge_kernel(x_ref, g0_ref, g1_ref, g2_ref, g3_ref, b0_ref, b1_ref, b2_ref, b3_ref, wb_ref, o_ref):
    x = x_ref[...]
    acc = None
    for g_ref, b_ref, n in zip((g0_ref, g1_ref, g2_ref, g3_ref), (b0_ref, b1_ref, b2_ref, b3_ref), range(4)):
        term = _sigmoid(_dot(x, g_ref[...])) * _dot(b_ref[...], wb_ref[n])
        acc = term if acc is None else acc + term
    o_ref[...] = acc.astype(o_ref.dtype)


def gated_merge(xn, w_gate, branches, w_branch):
    t, d = xn.shape
    wcol = 512
    n_col = d // wcol
    tm = _row_tile(t, 512)
    bw = branches[0].shape[1]
    gate_specs = [pl.BlockSpec((d, wcol), functools.partial(lambda c, m, n: (0, n * n_col + c), n=n))
                  for n in range(N_BRANCH)]
    branch_specs = [pl.BlockSpec((tm, bw), lambda c, m: (m, 0)) for _ in range(N_BRANCH)]
    return pl.pallas_call(
        _merge_kernel,
        out_shape=jax.ShapeDtypeStruct((t, d), BF16),
        grid=(n_col, t // tm),
        in_specs=[pl.BlockSpec((tm, d), lambda c, m: (m, 0))] + gate_specs + branch_specs
                 + [pl.BlockSpec((N_BRANCH, bw, wcol), lambda c, m: (0, 0, c))],
        out_specs=pl.BlockSpec((tm, wcol), lambda c, m: (m, c)),
        compiler_params=_cparams(("parallel", "parallel")),
        name="gated_merge",
    )(xn, w_gate, w_gate, w_gate, w_gate, *branches, w_branch)


def _outproj_kernel(a_ref, w_ref, x_ref, o_ref):
    o_ref[...] = x_ref[...] + _dot(a_ref[...], w_ref[...])


def out_projection(merged, w_out, x):
    t, d = x.shape
    tm = _row_tile(t, 512)
    return pl.pallas_call(
        _outproj_kernel,
        out_shape=jax.ShapeDtypeStruct((t, d), F32),
        grid=(t // tm,),
        in_specs=[pl.BlockSpec((tm, d), lambda m: (m, 0)),
                  pl.BlockSpec((d, d), lambda m: (0, 0)),
                  pl.BlockSpec((tm, d), lambda m: (m, 0))],
        out_specs=pl.BlockSpec((tm, d), lambda m: (m, 0)),
        compiler_params=_cparams(("parallel",)),
        name="out_projection",
    )(merged, w_out, x)


def _router_kernel(h_ref, g_ref, w_ref, b_ref, xn_ref, comb_ref):
    h = h_ref[...]
    ms = jnp.mean(h * h, axis=-1, keepdims=True)
    xn = h * lax.rsqrt(ms + RMS_EPS) * g_ref[...]
    xn_ref[...] = xn.astype(xn_ref.dtype)
    logits = jnp.dot(xn, w_ref[...], preferred_element_type=F32, precision=lax.Precision.HIGHEST) + b_ref[...]
    lane = lax.broadcasted_iota(jnp.int32, logits.shape, 1).astype(F32)
    far = float(LANES)

    def first_argmax(vals):
        top = jnp.max(vals, axis=-1, keepdims=True)
        idx = jnp.min(jnp.where(vals == top, lane, far), axis=-1, keepdims=True)
        return top, idx

    group_logits = jnp.where(lane < N_GROUPS, logits, -jnp.inf)
    g_top, g_idx = first_argmax(group_logits)
    g_weight = 1.0 / jnp.sum(jnp.exp(group_logits - g_top), axis=-1, keepdims=True)
    first = N_GROUPS + EXPERTS_PER_GROUP * g_idx
    in_group = (lane >= first) & (lane < first + EXPERTS_PER_GROUP)
    expert_logits = jnp.where(in_group, logits, -jnp.inf)
    top1, idx1 = first_argmax(expert_logits)
    top2, idx2 = first_argmax(jnp.where(lane == idx1, -jnp.inf, expert_logits))
    w1 = 1.0 / (1.0 + jnp.exp(top2 - top1))
    w2 = jnp.exp(top2 - top1) * w1
    comb_ref[...] = g_weight * (jnp.where(lane == idx1, w1, 0.0) + jnp.where(lane == idx2, w2, 0.0))


def router(h, g_ffn, w_router, b_router):
    t, d = h.shape
    tm = _row_tile(t, 256)
    return pl.pallas_call(
        _router_kernel,
        out_shape=(jax.ShapeDtypeStruct((t, d), BF16), jax.ShapeDtypeStruct((t, LANES), F32)),
        grid=(t // tm,),
        in_specs=[pl.BlockSpec((tm, d), lambda m: (m, 0)),
                  pl.BlockSpec((1, d), lambda m: (0, 0)),
                  pl.BlockSpec((d, LANES), lambda m: (0, 0)),
                  pl.BlockSpec((1, LANES), lambda m: (0, 0))],
        out_specs=(pl.BlockSpec((tm, d), lambda m: (m, 0)),
                   pl.BlockSpec((tm, LANES), lambda m: (m, 0))),
        compiler_params=_cparams(("parallel",)),
        name="router",
    )(h, g_ffn.reshape(1, d), w_router, b_router)


def _moe_kernel(x_ref, comb_ref, wg_ref, wu_ref, wd_ref, h_ref, o_ref):
    e = pl.program_id(1)

    @pl.when(e == 0)
    def _():
        o_ref[...] = h_ref[...]

    x = x_ref[...]
    comb = comb_ref[...]
    lane = lax.broadcasted_iota(jnp.int32, comb.shape, 1)
    weight = jnp.sum(jnp.where(lane == N_GROUPS + e, comb, 0.0), axis=-1, keepdims=True)
    gate = _dot(x, wg_ref[0])
    hid = gate * _sigmoid(gate) * _dot(x, wu_ref[0]) * weight
    o_ref[...] += _dot(hid.astype(BF16), wd_ref[0])


def moe(xn, comb, w_gate, w_up, w_down, h):
    t, d = xn.shape
    n_exp, _, f = w_gate.shape
    tm = _row_tile(t, 512)
    return pl.pallas_call(
        _moe_kernel,
        out_shape=jax.ShapeDtypeStruct((t, d), F32),
        grid=(t // tm, n_exp),
        in_specs=[pl.BlockSpec((tm, d), lambda m, e: (m, 0)),
                  pl.BlockSpec((tm, LANES), lambda m, e: (m, 0)),
                  pl.BlockSpec((1, d, f), lambda m, e: (e, 0, 0)),
                  pl.BlockSpec((1, d, f), lambda m, e: (e, 0, 0)),
                  pl.BlockSpec((1, f, d), lambda m, e: (e, 0, 0)),
                  pl.BlockSpec((tm, d), lambda m, e: (m, 0))],
        out_specs=pl.BlockSpec((tm, d), lambda m, e: (m, 0)),
        compiler_params=_cparams(("parallel", "arbitrary")),
        name="moe",
    )(xn, comb, w_gate, w_up, w_down, h)


def _rope_tables(pos, width):
    half = HEAD_DIM // 2
    inv_freq = jnp.exp(jnp.arange(half, dtype=F32) * (-2.0 * math.log(ROPE_THETA) / HEAD_DIM))
    ang = pos.astype(F32)[:, None] * inv_freq[None, :]
    reps = width // half
    cos = jnp.tile(jnp.cos(ang), (1, reps))
    sin = jnp.tile(jnp.sin(ang), (1, reps))
    lower = (jnp.arange(width) % HEAD_DIM) < half
    return cos, jnp.where(lower[None, :], -sin, sin)


def _rel_bias_table(rel_bias, qpos, kpos):
    rel = jnp.clip(qpos[:, None] - kpos[None, :], -REL_CLIP, REL_CLIP) + REL_CLIP
    return rel_bias[:, rel].astype(F32)


def _band_bias_table(rel_bias, tile):
    period = 4 * tile + 1
    m = jnp.arange(period)
    diff = jnp.where(m < 3 * tile, m, m - period)
    vec = rel_bias[:, jnp.clip(LEFT_CHUNKS * CHUNK - diff, -REL_CLIP, REL_CLIP) + REL_CLIP].astype(F32)
    flat = jnp.tile(vec, (1, tile))[:, :tile * (period - 1)]
    return flat.reshape(vec.shape[0], tile, period - 1)[:, :, :3 * tile]


def _layer_weights(l, p):
    n_qkv = N_SEG * 512
    w_in = p["w_in"][l]
    n_fg = w_in.shape[1] - n_qkv - N_BRANCH * w_in.shape[0]
    w_f = jnp.pad(w_in[:, n_qkv:n_qkv + n_fg], ((0, 0), (0, LANES - n_fg))).astype(BF16)
    b_f = jnp.pad(p["b_forget"][l], (0, LANES - n_fg)).reshape(1, LANES)
    w_router = jnp.pad(jnp.concatenate([p["router_group_w"][l], p["router_expert_w"][l]], axis=1),
                       ((0, 0), (0, LANES - N_GROUPS - N_EXPERTS)))
    b_router = jnp.pad(jnp.concatenate([p["router_group_b"][l], p["router_expert_b"][l]]),
                       (0, LANES - N_GROUPS - N_EXPERTS)).reshape(1, LANES)
    return dict(
        w_f=w_f, b_f=b_f, n_fg=n_fg,
        w_gate=w_in[:, n_qkv + n_fg:].astype(BF16),
        w_branch=p["w_branch"][l].astype(BF16), w_out=p["w_out"][l].astype(BF16),
        w_router=w_router, b_router=b_router,
        e_gate=p["expert_w_gate"][l].astype(BF16), e_up=p["expert_w_up"][l].astype(BF16),
        e_down=p["expert_w_down"][l].astype(BF16),
    )


def _ffn(h, l, p, lw):
    xn2, comb = router(h, p["norm_ffn"][l], lw["w_router"], lw["b_router"])
    return moe(xn2, comb, lw["e_gate"], lw["e_up"], lw["e_down"], h)


def _prompt_layer(x, l, p, lw, lam_init, batch, seq, tables):
    t, d = x.shape
    xn = rmsnorm(x, p["norm_mix"][l], BF16)
    qkv, qkv_bf = in_projection(xn, p["w_in"], l, tables["cos"], tables["sin"], seq)
    width = qkv.shape[-1]
    n_fg = lw["n_fg"]
    logf = log_forget(xn, lw["w_f"], lw["b_f"])[:, :n_fg].reshape(batch, seq, n_fg)
    cum_row = cumsum_lanes(jnp.swapaxes(logf, 1, 2))
    cum_col = jnp.swapaxes(cum_row.reshape(batch, n_fg // 2, 2, seq), 2, 3)
    branches = prompt_attention(
        qkv_bf.reshape(N_SEG, batch, seq, width), cum_col, tables["band_bias"][l],
        p["diff_lambda"][l], p["diff_subln"][l], lam_init, batch, seq)
    merged = gated_merge(xn, lw["w_gate"], [b.reshape(t, width) for b in branches], lw["w_branch"])
    h = out_projection(merged, lw["w_out"], x)
    h = _ffn(h, l, p, lw)
    seg = lambda i: qkv[i].reshape(batch, seq, width)
    wb = min(LEFT_CHUNKS * CHUNK, seq)
    state = (seg(1), seg(2), seg(4)[:, seq - wb:], seg(5)[:, seq - wb:], seg(7), seg(8), logf, seg(10), seg(11))
    return h, state


def _step_layer(x, l, p, lw, lam_init, batch, frames, caches, tables):
    t, d = x.shape
    xn = rmsnorm(x, p["norm_mix"][l], BF16)
    qkv, _ = in_projection(xn, p["w_in"], l, tables["cos"], tables["sin"], frames)
    width = qkv.shape[-1]
    n_fg = lw["n_fg"]
    logf = log_forget(xn, lw["w_f"], lw["b_f"])[:, :n_fg].reshape(batch, frames, n_fg)
    past_f = caches["c_logf"][l]
    past = past_f.shape[1]
    total = past + frames
    padded = -(-total // LANES) * LANES
    series = jnp.concatenate([past_f, logf], axis=1)
    cum = cumsum_lanes(jnp.pad(jnp.swapaxes(series, 1, 2), ((0, 0), (0, 0), (0, padded - total))))
    cum_cache_row = cum[:, :, :past]
    cum_new_row = cum[:, :, past:total]
    cum_q_col = cum_new_row.reshape(batch, n_fg * frames, 1)
    heads = lambda i: qkv[i].reshape(batch, frames, N_STEP_HEADS, HEAD_DIM)
    q_heads = lambda i: jnp.transpose(heads(i), (0, 2, 1, 3))
    seq_minor = lambda i: jnp.transpose(heads(i), (0, 2, 3, 1))
    new = dict(a=(q_heads(0), seq_minor(1), qkv[2].reshape(batch, frames, 4, LANES)),
               b=(q_heads(3), seq_minor(4), seq_minor(5)),
               c=(q_heads(6), seq_minor(7), seq_minor(8)),
               d=(q_heads(9), seq_minor(10), seq_minor(11)))
    branches = step_attention(
        l, new, caches, cum_q_col, cum_cache_row, cum_new_row,
        tables["bias_cache"][l], tables["bias_new"][l], p["diff_lambda"][l], p["diff_subln"][l], lam_init)
    merged = gated_merge(xn, lw["w_gate"], [b.reshape(t, width) for b in branches], lw["w_branch"])
    h = out_projection(merged, lw["w_out"], x)
    h = _ffn(h, l, p, lw)
    seg = lambda i: qkv[i].reshape(batch, frames, width)
    state = (seg(1), seg(2), seg(4), seg(5), seg(7), seg(8), logf, seg(10), seg(11))
    return h, state


def kernel(x_prompt, x_sample, cache_a_k, cache_a_v, cache_b_k, cache_b_v, cache_c_k, cache_c_v, cache_c_logf, cache_d_k, cache_d_v, norm_mix, w_in, b_forget, diff_lambda, diff_subln, rel_bias, w_branch, w_out, norm_ffn, router_group_w, router_group_b, router_expert_w, router_expert_b, expert_w_gate, expert_w_up, expert_w_down, norm_final):
    p = dict(norm_mix=norm_mix, w_in=w_in, b_forget=b_forget, diff_lambda=diff_lambda, diff_subln=diff_subln,
             w_branch=w_branch, w_out=w_out, norm_ffn=norm_ffn, router_group_w=router_group_w,
             router_group_b=router_group_b, router_expert_w=router_expert_w, router_expert_b=router_expert_b,
             expert_w_gate=expert_w_gate, expert_w_up=expert_w_up, expert_w_down=expert_w_down)
    depth = w_in.shape[0]
    bp, sp, d = x_prompt.shape
    bs, fs, _ = x_sample.shape
    past = cache_a_k.shape[2]
    width = 512
    n_heads = width // HEAD_DIM

    def seq_minor(c):
        c = jnp.moveaxis(c, 2, -1)
        return c.reshape(c.shape[0], c.shape[1], N_STEP_HEADS, HEAD_DIM, c.shape[-1])

    caches = dict(a_k=seq_minor(cache_a_k), a_v=cache_a_v, b_k=seq_minor(cache_b_k), b_v=seq_minor(cache_b_v),
                  c_k=seq_minor(cache_c_k), c_v=seq_minor(cache_c_v), c_logf=cache_c_logf,
                  d_k=seq_minor(cache_d_k), d_v=seq_minor(cache_d_v))

    pos_p = jnp.arange(sp, dtype=jnp.int32)
    pos_s = past + jnp.arange(fs, dtype=jnp.int32)
    cos_p, sin_p = _rope_tables(pos_p, width)
    cos_s, sin_s = _rope_tables(pos_s, width)
    tile = 256
    band_bias = jnp.stack([_band_bias_table(rel_bias[l], tile).reshape(n_heads // 2, 2, tile, 3 * tile)
                           for l in range(depth)])
    wb = cache_b_k.shape[2]
    kpos_b = jnp.arange(past - wb, past + fs, dtype=jnp.int32)
    step_bias = jnp.stack([_rel_bias_table(rel_bias[l], pos_s, kpos_b).reshape(n_heads * fs, wb + fs)
                           for l in range(depth)])
    tables_p = dict(cos=cos_p, sin=sin_p, band_bias=band_bias)
    tables_s = dict(cos=cos_s, sin=sin_s, bias_cache=step_bias[:, :, :wb], bias_new=step_bias[:, :, wb:])

    hp = x_prompt.reshape(bp * sp, d)
    hs = x_sample.reshape(bs * fs, d)
    st_p, st_s = [], []
    for l in range(depth):
        lam_init = 0.8 - 0.6 * math.exp(-0.3 * l)
        lw = _layer_weights(l, p)
        hp, state_p = _prompt_layer(hp, l, p, lw, lam_init, bp, sp, tables_p)
        hs, state_s = _step_layer(hs, l, p, lw, lam_init, bs, fs, caches, tables_s)
        st_p.append(state_p)
        st_s.append(state_s)
    y_prompt = rmsnorm(hp, norm_final, F32).reshape(bp, sp, d)
    y_sample = rmsnorm(hs, norm_final, F32).reshape(bs, fs, d)

    h_a = width // (2 * HEAD_DIM)
    state_shapes = lambda b, s, sb: (
        (b, s, h_a, 2, HEAD_DIM), (b, s, h_a, 2 * HEAD_DIM), (b, sb, n_heads, HEAD_DIM), (b, sb, n_heads, HEAD_DIM),
        (b, s, n_heads, HEAD_DIM), (b, s, n_heads, HEAD_DIM), (b, s, n_heads), (b, s, n_heads, HEAD_DIM),
        (b, s, n_heads, HEAD_DIM))

    def stack(states, shapes):
        return tuple(jnp.stack([st[i].reshape(shapes[i]) for st in states], axis=0) for i in range(len(shapes)))

    out_p = stack(st_p, state_shapes(bp, sp, min(LEFT_CHUNKS * CHUNK, sp)))
    out_s = stack(st_s, state_shapes(bs, fs, fs))
    return (y_prompt, y_sample) + out_p + out_s
```

```python
import functools
import math

import jax
import jax.numpy as jnp
from jax import lax
from jax.experimental import pallas as pl
from jax.experimental.pallas import tpu as pltpu

F32 = jnp.float32
BF16 = jnp.bfloat16

CHUNK = 64
HEAD_DIM = 64
LEFT_CHUNKS = 8
REL_CLIP = 128
ROPE_THETA = 10000.0
N_BRANCH = 4
N_GROUPS = 4
EXPERTS_PER_GROUP = 4
N_EXPERTS = N_GROUPS * EXPERTS_PER_GROUP
RMS_EPS = 1e-6
NEG_INF = -1e30
Q_SCALE = HEAD_DIM ** -0.5
N_SEG = 12

LANES = 128
VMEM_LIMIT = 56 * 1024 * 1024

NT_DIMS = (((1,), (1,)), ((), ()))


def _cparams(sem):
    return pltpu.CompilerParams(dimension_semantics=sem, vmem_limit_bytes=VMEM_LIMIT)


def _row_tile(t, pref):
    return pref if t % pref == 0 else t


def _sigmoid(z):
    return 1.0 / (1.0 + jnp.exp(-z))


def _dot(a, b):
    return jnp.dot(a, b, preferred_element_type=F32)


def _dot_nt(a, b):
    return lax.dot_general(a, b, NT_DIMS, preferred_element_type=F32)


def _rms_kernel(x_ref, g_ref, o_ref):
    x = x_ref[...]
    ms = jnp.mean(x * x, axis=-1, keepdims=True)
    o_ref[...] = (x * lax.rsqrt(ms + RMS_EPS) * g_ref[...]).astype(o_ref.dtype)


def rmsnorm(x, g, out_dtype):
    t, d = x.shape
    tm = _row_tile(t, 512)
    return pl.pallas_call(
        _rms_kernel,
        out_shape=jax.ShapeDtypeStruct((t, d), out_dtype),
        grid=(t // tm,),
        in_specs=[pl.BlockSpec((tm, d), lambda m: (m, 0)),
                  pl.BlockSpec((1, d), lambda m: (0, 0))],
        out_specs=pl.BlockSpec((tm, d), lambda m: (m, 0)),
        compiler_params=_cparams(("parallel",)),
        name="rmsnorm",
    )(x, g.reshape(1, d))


def _inproj_kernel(x_ref, w_ref, cos_ref, sin_ref, o_ref, ob_ref, wb_ref):
    n = pl.program_id(0)

    @pl.when(pl.program_id(1) == 0)
    def _():
        wb_ref[...] = w_ref[0].astype(BF16)

    acc = _dot_nt(x_ref[...], wb_ref[...])
    q_scale = jnp.where(n % 3 == 0, Q_SCALE, 1.0)

    @pl.when(n < 2)
    def _():
        width = acc.shape[1]
        half = HEAD_DIM // 2
        lane = lax.broadcasted_iota(jnp.int32, acc.shape, 1)
        lower = (lane % HEAD_DIM) < half
        partner = jnp.where(lower, pltpu.roll(acc, width - half, 1), pltpu.roll(acc, half, 1))
        roped = acc * cos_ref[...] + partner * sin_ref[...]
        o_ref[0] = roped
        ob_ref[0] = (roped * q_scale).astype(ob_ref.dtype)

    @pl.when(n >= 2)
    def _():
        o_ref[0] = acc
        ob_ref[0] = (acc * q_scale).astype(ob_ref.dtype)


def in_projection(xn, w_in_t, layer, cos_tab, sin_tab, seq):
    t, d = xn.shape
    w = 512
    tm = _row_tile(t, 512)
    if seq % tm == 0:
        n_tab = seq // tm
    else:
        assert tm % seq == 0
        cos_tab = jnp.tile(cos_tab, (tm // seq, 1))
        sin_tab = jnp.tile(sin_tab, (tm // seq, 1))
        n_tab = 1
    return pl.pallas_call(
        _inproj_kernel,
        out_shape=(jax.ShapeDtypeStruct((N_SEG, t, w), F32), jax.ShapeDtypeStruct((N_SEG, t, w), BF16)),
        grid=(N_SEG, t // tm),
        in_specs=[pl.BlockSpec((tm, d), lambda n, m: (m, 0)),
                  pl.BlockSpec((1, w, d), lambda n, m: (layer, n, 0)),
                  pl.BlockSpec((tm, w), lambda n, m: (m % n_tab, 0)),
                  pl.BlockSpec((tm, w), lambda n, m: (m % n_tab, 0))],
        out_specs=(pl.BlockSpec((1, tm, w), lambda n, m: (n, m, 0)),
                   pl.BlockSpec((1, tm, w), lambda n, m: (n, m, 0))),
        scratch_shapes=[pltpu.VMEM((w, d), BF16)],
        compiler_params=_cparams(("parallel", "arbitrary")),
        name="in_projection",
    )(xn, w_in_t, cos_tab, sin_tab)


STATE_SEGS = (1, 2, 4, 5, 7, 8, 10, 11)
VALUE_A_SEG = 2
BAND_SEGS = (4, 5)


def _inproj_prompt_kernel(*refs, n_alias, tiles_per_seq):
    x_ref, w_ref, cos_ref, sin_ref = refs[:4]
    ob_ref = refs[4 + n_alias]
    st_refs = refs[5 + n_alias:5 + n_alias + len(STATE_SEGS)]
    wb_ref = refs[-1]
    n, m = pl.program_id(0), pl.program_id(1)

    @pl.when(m == 0)
    def _():
        wb_ref[...] = w_ref[0].astype(BF16)

    acc = _dot_nt(x_ref[...], wb_ref[...])
    q_scale = jnp.where(n % 3 == 0, Q_SCALE, 1.0)

    def emit(val):
        ob_ref[0] = (val * q_scale).astype(ob_ref.dtype)
        for seg, st_ref in zip(STATE_SEGS, st_refs):
            if seg == VALUE_A_SEG:
                @pl.when(n == seg)
                def _(st_ref=st_ref):
                    for h in range(st_ref.shape[3]):
                        st_ref[0, 0, :, h, :] = val[:, h * LANES:(h + 1) * LANES]
            elif seg in BAND_SEGS:
                @pl.when(jnp.logical_and(n == seg, m % tiles_per_seq == tiles_per_seq - 1))
                def _(st_ref=st_ref):
                    st_ref[0, 0] = val.T
            else:
                @pl.when(n == seg)
                def _(st_ref=st_ref):
                    st_ref[0, 0] = val.T

    @pl.when(n < 2)
    def _():
        width = acc.shape[1]
        half = HEAD_DIM // 2
        lane = lax.broadcasted_iota(jnp.int32, acc.shape, 1)
        lower = (lane % HEAD_DIM) < half
        partner = jnp.where(lower, pltpu.roll(acc, width - half, 1), pltpu.roll(acc, half, 1))
        emit(acc * cos_ref[...] + partner * sin_ref[...])

    @pl.when(n >= 2)
    def _():
        emit(acc)


def prompt_state_buffers(depth, batch, seq, w=512):
    band = LEFT_CHUNKS * CHUNK
    shapes = []
    for seg in STATE_SEGS:
        if seg == VALUE_A_SEG:
            shapes.append((depth, batch, seq, w // LANES, LANES))
        elif seg in BAND_SEGS:
            shapes.append((depth, batch, w, band))
        else:
            shapes.append((depth, batch, w, seq))
    return tuple(jnp.zeros(s, F32) for s in shapes)


def in_projection_prompt(xn, w_in_t, layer, cos_tab, sin_tab, batch, seq, state_bufs):
    t, d = xn.shape
    w = 512
    tm = 512
    band = LEFT_CHUNKS * CHUNK
    assert seq % tm == 0 and band == tm and t == batch * seq
    per_seq = seq // tm
    n_tiles = t // tm

    def parked(seg, n, m):
        return jnp.where(n < seg, 0, jnp.where(n > seg, n_tiles - 1, m))

    specs = []
    for seg in STATE_SEGS:
        if seg == VALUE_A_SEG:
            specs.append(pl.BlockSpec((1, 1, tm, w // LANES, LANES), functools.partial(
                lambda n, m, seg: (layer, parked(seg, n, m) // per_seq, parked(seg, n, m) % per_seq, 0, 0), seg=seg)))
        elif seg in BAND_SEGS:
            specs.append(pl.BlockSpec((1, 1, w, band), functools.partial(
                lambda n, m, seg: (layer, parked(seg, n, m) // per_seq, 0, 0), seg=seg)))
        else:
            specs.append(pl.BlockSpec((1, 1, w, tm), functools.partial(
                lambda n, m, seg: (layer, parked(seg, n, m) // per_seq, 0, parked(seg, n, m) % per_seq), seg=seg)))
    alias_in = list(state_bufs)
    shapes = [jax.ShapeDtypeStruct(b.shape, b.dtype) for b in alias_in]
    n_fixed = 4
    out = pl.pallas_call(
        functools.partial(_inproj_prompt_kernel, n_alias=len(alias_in), tiles_per_seq=per_seq),
        out_shape=[jax.ShapeDtypeStruct((N_SEG, t, w), BF16)] + shapes,
        grid=(N_SEG, n_tiles),
        in_specs=[pl.BlockSpec((tm, d), lambda n, m: (m, 0)),
                  pl.BlockSpec((1, w, d), lambda n, m: (layer, n, 0)),
                  pl.BlockSpec((tm, w), lambda n, m: (m % per_seq, 0)),
                  pl.BlockSpec((tm, w), lambda n, m: (m % per_seq, 0))]
                 + [pl.BlockSpec(memory_space=pl.ANY) for _ in alias_in],
        out_specs=[pl.BlockSpec((1, tm, w), lambda n, m: (n, m, 0))] + specs,
        input_output_aliases={n_fixed + i: 1 + i for i in range(len(alias_in))},
        scratch_shapes=[pltpu.VMEM((w, d), BF16)],
        compiler_params=_cparams(("arbitrary", "arbitrary")),
        name="in_projection_prompt",
    )(xn, w_in_t, cos_tab, sin_tab, *alias_in)
    return out[0], tuple(out[1:])


def _logf_kernel(x_ref, w_ref, b_ref, o_ref):
    z = _dot_nt(x_ref[...], w_ref[...].astype(BF16)) + b_ref[...]
    o_ref[...] = jnp.minimum(z, 0.0) - jnp.log1p(jnp.exp(-jnp.abs(z)))


def log_forget(xn, w_f, b_f):
    t, d = xn.shape
    tm = _row_tile(t, 512)
    return pl.pallas_call(
        _logf_kernel,
        out_shape=jax.ShapeDtypeStruct((t, LANES), F32),
        grid=(t // tm,),
        in_specs=[pl.BlockSpec((tm, d), lambda m: (m, 0)),
                  pl.BlockSpec((LANES, d), lambda m: (0, 0)),
                  pl.BlockSpec((1, LANES), lambda m: (0, 0))],
        out_specs=pl.BlockSpec((tm, LANES), lambda m: (m, 0)),
        compiler_params=_cparams(("parallel",)),
        name="log_forget",
    )(xn, w_f, b_f)


def _cumsum_kernel(x_ref, o_ref):
    x = x_ref[0]
    length = x.shape[1]
    lane = lax.broadcasted_iota(jnp.int32, x.shape, 1)
    shift = 1
    while shift < length:
        x = x + jnp.where(lane >= shift, pltpu.roll(x, shift, 1), 0.0)
        shift *= 2
    o_ref[0] = x


def cumsum_lanes(x):
    b, h, length = x.shape
    return pl.pallas_call(
        _cumsum_kernel,
        out_shape=jax.ShapeDtypeStruct(x.shape, F32),
        grid=(b,),
        in_specs=[pl.BlockSpec((1, h, length), lambda i: (i, 0, 0))],
        out_specs=pl.BlockSpec((1, h, length), lambda i: (i, 0, 0)),
        compiler_params=_cparams(("parallel",)),
        name="cumsum",
    )(x)


def _diff_lambda(lam_ref, lam_init):
    lv = lam_ref[...]
    a = jnp.sum(lv[0:1] * lv[1:2], axis=1, keepdims=True)
    b = jnp.sum(lv[2:3] * lv[3:4], axis=1, keepdims=True)
    return jnp.exp(a) - jnp.exp(b) + lam_init


def _subln(o, w, lam_init):
    ms = jnp.mean(o * o, axis=-1, keepdims=True)
    return o * lax.rsqrt(ms + RMS_EPS) * w * (1.0 - lam_init)


def _head_lanes(shape, j):
    lane = lax.broadcasted_iota(jnp.int32, shape, 1)
    return (lane >= j * HEAD_DIM) & (lane < (j + 1) * HEAD_DIM)


def _split3(x):
    hi = x.astype(BF16).astype(F32)
    mid = (x - hi).astype(BF16).astype(F32)
    lo = (x - hi - mid).astype(BF16).astype(F32)
    return hi, mid, lo


def _with_bias_columns(x, j, cols):
    lane = lax.broadcasted_iota(jnp.int32, x.shape, 1)
    out = jnp.where(_head_lanes(x.shape, j), x.astype(F32), 0.0)
    base = (1 - j) * HEAD_DIM
    for i, c in enumerate(cols):
        out = jnp.where(lane == base + i, c, out)
    return out.astype(BF16)


def _softmax_tile_update(state, s, v):
    m, l, acc = state
    m_new = jnp.maximum(m, jnp.max(s, axis=-1, keepdims=True))
    alpha = jnp.exp(m - m_new)
    p = jnp.exp(s - m_new)
    return (m_new, alpha * l + jnp.sum(p, axis=-1, keepdims=True), alpha * acc + _dot(p.astype(BF16), v))


def _causal_softmax_sweep(qs, k_tile, v_tile, diag_mask, qi, tile):
    init = tuple((jnp.full((tile, 1), NEG_INF, F32), jnp.zeros((tile, 1), F32), jnp.zeros((tile, LANES), F32))
                 for _ in qs)

    def update(states, start, mask):
        v = v_tile(start)
        out = []
        for j, q in enumerate(qs):
            s = _dot_nt(q, k_tile(j, start))
            if mask is not None:
                s = jnp.where(mask, s, NEG_INF)
            out.append(_softmax_tile_update(states[j], s, v))
        return tuple(out)

    states = lax.fori_loop(0, qi, lambda i, st: update(st, pl.multiple_of(i * tile, tile), None), init)
    return update(states, pl.multiple_of(qi * tile, tile), diag_mask)


def _a_prompt_kernel(q_ref, k_ref, v_ref, lam_ref, w_ref, o_ref, *, tile, lam_init):
    qi = pl.program_id(2)
    q = q_ref[0, 0]
    qs = [jnp.where(_head_lanes(q.shape, j), q, jnp.zeros_like(q)) for j in range(2)]
    row = lax.broadcasted_iota(jnp.int32, (tile, tile), 0)
    col = lax.broadcasted_iota(jnp.int32, (tile, tile), 1)
    diag_mask = (col // CHUNK) <= (row // CHUNK)
    k_tile = lambda j, start: k_ref[0, 0, pl.ds(start, tile), :]
    v_tile = lambda start: v_ref[0, 0, pl.ds(start, tile), :]
    (_, l1, acc1), (_, l2, acc2) = _causal_softmax_sweep(qs, k_tile, v_tile, diag_mask, qi, tile)
    lam = _diff_lambda(lam_ref, lam_init)
    o = acc1 / l1 - lam * (acc2 / l2)
    o_ref[0] = _subln(o, w_ref[...], lam_init).astype(o_ref.dtype)


def _c_prompt_kernel(q_ref, k_ref, v_ref, cq_ref, ck_ref, o_ref, kb_ref, *, tile):
    qi = pl.program_id(2)
    one = jnp.float32(1.0)

    @pl.when(qi == 0)
    def _():
        k = k_ref[0, 0]
        ck = ck_ref[0, 0]
        for j in range(2):
            hi, mid, lo = _split3(-ck[:, j:j + 1])
            kb_ref[j] = _with_bias_columns(k, j, [one, one, one, hi, mid, lo])

    q = q_ref[0, 0]
    cq = cq_ref[0, 0]
    qs = []
    for j in range(2):
        hi, mid, lo = _split3(cq[:, j:j + 1])
        qs.append(_with_bias_columns(q, j, [hi, mid, lo, one, one, one]))
    row = lax.broadcasted_iota(jnp.int32, (tile, tile), 0)
    col = lax.broadcasted_iota(jnp.int32, (tile, tile), 1)
    k_tile = lambda j, start: kb_ref[j, pl.ds(start, tile), :]
    v_tile = lambda start: v_ref[0, 0, pl.ds(start, tile), :]
    (_, l1, acc1), (_, l2, acc2) = _causal_softmax_sweep(qs, k_tile, v_tile, col <= row, qi, tile)
    o_ref[0] = jnp.where(_head_lanes(acc1.shape, 0), acc1 / l1, acc2 / l2).astype(o_ref.dtype)


SUFFIX_BLOCK = 256


def _strict_upper_ones(n):
    r = lax.broadcasted_iota(jnp.int32, (n, n), 0)
    c = lax.broadcasted_iota(jnp.int32, (n, n), 1)
    return jnp.where(r > c, 1.0, 0.0).astype(BF16)


def _suffix_sums(lm, ones_after):
    hi = lm.astype(BF16)
    lo = (lm - hi.astype(F32)).astype(BF16)
    return _dot(hi, ones_after) + _dot(lo, ones_after)


def _stick_weights(z, mask, run, ones_after):
    sp = jnp.maximum(z, 0.0) + jnp.log(1.0 + jnp.exp(-jnp.abs(z)))
    lm = -sp if mask is None else jnp.where(mask, -sp, 0.0)
    block = ones_after.shape[0]
    n_block = z.shape[1] // block
    later = [None] * n_block
    for c in reversed(range(n_block)):
        part = lm[:, c * block:(c + 1) * block]
        later[c] = _suffix_sums(part, ones_after) + run
        run = run + jnp.sum(part, axis=-1, keepdims=True)
    later = later[0] if n_block == 1 else jnp.concatenate(later, axis=1)
    a = jnp.exp((z - sp) + later)
    if mask is not None:
        a = jnp.where(mask, a, 0.0)
    return a, run


def _d_prompt_kernel(q_ref, k_ref, v_ref, o_ref, *, tile):
    qi = pl.program_id(2)
    q = q_ref[0, 0]
    qs = [jnp.where(_head_lanes(q.shape, j), q, jnp.zeros_like(q)) for j in range(2)]
    ones_after = _strict_upper_ones(SUFFIX_BLOCK)
    row = lax.broadcasted_iota(jnp.int32, (tile, tile), 0)
    col = lax.broadcasted_iota(jnp.int32, (tile, tile), 1)

    def update(states, start, mask):
        k = k_ref[0, 0, pl.ds(start, tile), :]
        v = v_ref[0, 0, pl.ds(start, tile), :]
        out = []
        for j, q_j in enumerate(qs):
            run, acc = states[j]
            a, run = _stick_weights(_dot_nt(q_j, k), mask, run, ones_after)
            out.append((run, acc + _dot(a.astype(BF16), v)))
        return tuple(out)

    init = tuple((jnp.zeros((tile, 1), F32), jnp.zeros((tile, LANES), F32)) for _ in qs)
    states = update(init, pl.multiple_of(qi * tile, tile), col < row)
    states = lax.fori_loop(
        0, qi, lambda i, st: update(st, pl.multiple_of((qi - 1 - i) * tile, tile), None), states)
    o_ref[0] = jnp.where(_head_lanes((tile, LANES), 0), states[0][1], states[1][1]).astype(o_ref.dtype)


def _b_prompt_kernel(q_ref, k0_ref, k1_ref, k2_ref, v0_ref, v1_ref, v2_ref, bias_ref, o_ref, *, tile):
    qi = pl.program_id(2)
    q = q_ref[0, 0]
    row_chunk = lax.broadcasted_iota(jnp.int32, (tile, tile), 0) // CHUNK
    col_chunk = lax.broadcasted_iota(jnp.int32, (tile, tile), 1) // CHUNK
    back = 2
    ks = [r[0, 0] for r in (k0_ref, k1_ref, k2_ref)]
    vs = [r[0, 0] for r in (v0_ref, v1_ref, v2_ref)]
    outs = []
    for j in range(2):
        q_j = jnp.where(_head_lanes(q.shape, j), q, jnp.zeros_like(q))
        scores = []
        for c in range(3):
            dist = row_chunk - col_chunk + (back - c) * (tile // CHUNK)
            valid = (dist >= 0) & (dist <= LEFT_CHUNKS) & (qi - back + c >= 0)
            s = _dot_nt(q_j, ks[c]) + bias_ref[0, j, :, c * tile:(c + 1) * tile]
            scores.append(jnp.where(valid, s, NEG_INF))
        m = jnp.maximum(jnp.maximum(jnp.max(scores[0], axis=-1, keepdims=True),
                                    jnp.max(scores[1], axis=-1, keepdims=True)),
                        jnp.max(scores[2], axis=-1, keepdims=True))
        ps = [jnp.exp(s - m) for s in scores]
        denom = sum(jnp.sum(p, axis=-1, keepdims=True) for p in ps)
        o = sum(_dot(p.astype(BF16), v) for p, v in zip(ps, vs))
        outs.append(o / denom)
    o_ref[0] = jnp.where(_head_lanes(outs[0].shape, 0), outs[0], outs[1]).astype(o_ref.dtype)


def prompt_attention(qkv, cum_col, band_bias, lam_vec, subln_w, lam_init, batch, seq):
    width = qkv.shape[-1]
    n_pair = width // LANES
    out_shape = jax.ShapeDtypeStruct((batch, seq, width), BF16)
    tile = 512
    assert seq % tile == 0 and tile % CHUNK == 0
    sem = ("parallel", "parallel", "arbitrary")

    q_spec = lambda seg: pl.BlockSpec((1, 1, tile, LANES), lambda b, h, qi: (seg, b, qi, h))
    full_spec = lambda seg: pl.BlockSpec((1, 1, seq, LANES), lambda b, h, qi: (seg, b, 0, h))
    out_spec = pl.BlockSpec((1, tile, LANES), lambda b, h, qi: (b, qi, h))
    grid = (batch, n_pair, seq // tile)

    o_a = pl.pallas_call(
        functools.partial(_a_prompt_kernel, tile=tile, lam_init=lam_init),
        out_shape=out_shape, grid=grid,
        in_specs=[q_spec(0), full_spec(1), full_spec(2),
                  pl.BlockSpec((4, HEAD_DIM), lambda b, h, qi: (0, 0)),
                  pl.BlockSpec((1, LANES), lambda b, h, qi: (0, 0))],
        out_specs=out_spec,
        compiler_params=_cparams(sem),
        name="diff_attention",
    )(qkv, qkv, qkv, lam_vec, subln_w.reshape(1, LANES))

    o_c = pl.pallas_call(
        functools.partial(_c_prompt_kernel, tile=tile),
        out_shape=out_shape, grid=grid,
        in_specs=[q_spec(6), full_spec(7), full_spec(8),
                  pl.BlockSpec((1, 1, tile, 2), lambda b, h, qi: (b, h, qi, 0)),
                  pl.BlockSpec((1, 1, seq, 2), lambda b, h, qi: (b, h, 0, 0))],
        out_specs=out_spec,
        scratch_shapes=[pltpu.VMEM((2, seq, LANES), BF16)],
        compiler_params=_cparams(sem),
        name="forgetting_attention",
    )(qkv, qkv, qkv, cum_col, cum_col)

    o_d = pl.pallas_call(
        functools.partial(_d_prompt_kernel, tile=tile),
        out_shape=out_shape, grid=grid,
        in_specs=[q_spec(9), full_spec(10), full_spec(11)],
        out_specs=out_spec,
        compiler_params=_cparams(sem),
        name="stick_breaking_attention",
    )(qkv, qkv, qkv)

    band_tile = 256
    assert seq % band_tile == 0 and band_tile * 2 == LEFT_CHUNKS * CHUNK

    def band_spec(seg, back):
        return pl.BlockSpec((1, 1, band_tile, LANES), lambda b, h, qi: (seg, b, jnp.maximum(qi - back, 0), h))

    o_b = pl.pallas_call(
        functools.partial(_b_prompt_kernel, tile=band_tile),
        out_shape=out_shape,
        grid=(batch, n_pair, seq // band_tile),
        in_specs=[band_spec(3, 0),
                  band_spec(4, 2), band_spec(4, 1), band_spec(4, 0),
                  band_spec(5, 2), band_spec(5, 1), band_spec(5, 0),
                  pl.BlockSpec((1, 2, band_tile, 3 * band_tile), lambda b, h, qi: (h, 0, 0, 0))],
        out_specs=pl.BlockSpec((1, band_tile, LANES), lambda b, h, qi: (b, qi, h)),
        compiler_params=_cparams(("parallel", "parallel", "parallel")),
        name="band_attention",
    )(qkv, qkv, qkv, qkv, qkv, qkv, qkv, band_bias)
    return o_a, o_b, o_c, o_d


N_STEP_HEADS = 8


def _step_scores(qs_ref, kt_ref):
    rows = [_dot(qs_ref[h], kt_ref[0, 0, h].astype(BF16)) for h in range(N_STEP_HEADS)]
    return jnp.concatenate(rows, axis=0)


def _step_pv(p, v_ref, frames, v_token_major):
    pb = p.astype(BF16)
    if v_token_major:
        rows = [_dot(pb[2 * g * frames:(2 * g + 2) * frames], v_ref[0, 0, :, g, :].astype(BF16))
                for g in range(N_STEP_HEADS // 2)]
    else:
        rows = [_dot_nt(pb[h * frames:(h + 1) * frames], v_ref[0, 0, h].astype(BF16))
                for h in range(N_STEP_HEADS)]
    return jnp.concatenate(rows, axis=0)


def _store_scaled_queries(q_ref, qs_ref):
    for h in range(N_STEP_HEADS):
        qs_ref[h] = (q_ref[0, h] * Q_SCALE).astype(BF16)


def _rows_per_head(x, frames):
    return jnp.concatenate([jnp.broadcast_to(x[h:h + 1, :], (frames, x.shape[1]))
                            for h in range(x.shape[0])], axis=0)


def _new_rows_mask(frames, n_rows, strict):
    frame = lax.broadcasted_iota(jnp.int32, (n_rows, frames), 0) % frames
    key = lax.broadcasted_iota(jnp.int32, (n_rows, frames), 1)
    return key < frame if strict else key <= frame


def _heads_to_lanes(acc, frames):
    return jnp.concatenate([acc[h * frames:(h + 1) * frames] for h in range(N_STEP_HEADS)], axis=1)


def _softmax_step_kernel(*refs, mode, frames, lam_init):
    if mode == "A":
        (q_ref, kc_ref, vc_ref, kn_ref, vn_ref, lam_ref, w_ref,
         o_ref, qs_ref, m_ref, l_ref, acc_ref) = refs
    elif mode == "B":
        (q_ref, kc_ref, vc_ref, kn_ref, vn_ref, bias_c_ref, bias_n_ref,
         o_ref, qs_ref, m_ref, l_ref, acc_ref) = refs
    else:
        (q_ref, kc_ref, vc_ref, kn_ref, vn_ref, cq_ref, ckc_ref, ckn_ref,
         o_ref, qs_ref, m_ref, l_ref, acc_ref) = refs
    n_rows = N_STEP_HEADS * frames
    token_major_v = mode == "A"
    kt = pl.program_id(1)

    @pl.when(kt == 0)
    def _():
        _store_scaled_queries(q_ref, qs_ref)
        m_ref[...] = jnp.full(m_ref.shape, NEG_INF, F32)
        l_ref[...] = jnp.zeros(l_ref.shape, F32)
        acc_ref[...] = jnp.zeros(acc_ref.shape, F32)

    def softmax_tile(s, v_ref):
        m_prev = m_ref[...]
        m_new = jnp.maximum(m_prev, jnp.max(s, axis=-1, keepdims=True))
        alpha = jnp.exp(m_prev - m_new)
        p = jnp.exp(s - m_new)
        l_ref[...] = alpha * l_ref[...] + jnp.sum(p, axis=-1, keepdims=True)
        acc_ref[...] = alpha * acc_ref[...] + _step_pv(p, v_ref, frames, token_major_v)
        m_ref[...] = m_new

    s = _step_scores(qs_ref, kc_ref)
    if mode == "B":
        s = s + bias_c_ref[...]
    elif mode == "C":
        s = s + cq_ref[0] - _rows_per_head(ckc_ref[0], frames)
    softmax_tile(s, vc_ref)

    @pl.when(kt == pl.num_programs(1) - 1)
    def _():
        s = _step_scores(qs_ref, kn_ref)
        if mode == "B":
            s = s + bias_n_ref[...]
        elif mode == "C":
            s = s + cq_ref[0] - _rows_per_head(ckn_ref[0], frames)
            s = jnp.where(_new_rows_mask(frames, n_rows, strict=False), s, NEG_INF)
        softmax_tile(s, vn_ref)
        o = acc_ref[...] / l_ref[...]
        if mode == "A":
            lam = _diff_lambda(lam_ref, lam_init)
            pieces = []
            for h in range(N_STEP_HEADS // 2):
                o1 = o[(2 * h) * frames:(2 * h + 1) * frames]
                o2 = o[(2 * h + 1) * frames:(2 * h + 2) * frames]
                pieces.append(_subln(o1 - lam * o2, w_ref[...], lam_init))
            o_ref[0] = jnp.concatenate(pieces, axis=1).astype(o_ref.dtype)
        else:
            o_ref[0] = _heads_to_lanes(o, frames).astype(o_ref.dtype)


def _stick_step_kernel(q_ref, kc_ref, vc_ref, kn_ref, vn_ref, ones_ref, o_ref,
                       qs_ref, run_ref, acc_ref, *, frames):
    n_rows = N_STEP_HEADS * frames
    kt = pl.program_id(1)

    @pl.when(kt == 0)
    def _():
        _store_scaled_queries(q_ref, qs_ref)
        z = _step_scores(qs_ref, kn_ref)
        mask = _new_rows_mask(frames, n_rows, strict=True)
        a, run = _stick_weights(z, mask, jnp.zeros((n_rows, 1), F32), _strict_upper_ones(frames))
        acc_ref[...] = _step_pv(a, vn_ref, frames, False)
        run_ref[...] = run

    z = _step_scores(qs_ref, kc_ref)
    a, run = _stick_weights(z, None, run_ref[...], ones_ref[...])
    acc_ref[...] += _step_pv(a, vc_ref, frames, False)
    run_ref[...] = run

    @pl.when(kt == pl.num_programs(1) - 1)
    def _():
        o_ref[0] = _heads_to_lanes(acc_ref[...], frames).astype(o_ref.dtype)


def step_attention(layer, new, caches, cum_q_col, cum_cache_row, cum_new_row, bias_cache, bias_new,
                   lam_vec, subln_w, lam_init):
    batch, _, frames, _ = new["a"][0].shape
    width = N_STEP_HEADS * HEAD_DIM
    n_rows = N_STEP_HEADS * frames
    out_shape = jax.ShapeDtypeStruct((batch, frames, width), BF16)
    sem = ("parallel", "arbitrary")
    out_spec = pl.BlockSpec((1, frames, width), lambda b, kt: (b, 0, 0))
    q_spec = pl.BlockSpec((1, N_STEP_HEADS, frames, HEAD_DIM), lambda b, kt: (b, 0, 0, 0))
    new_t_spec = pl.BlockSpec((1, 1, N_STEP_HEADS, HEAD_DIM, frames), lambda b, kt: (0, b, 0, 0, 0))
    forward = lambda kt: kt

    def cache_t_spec(tk, tile_of):
        return pl.BlockSpec((1, 1, N_STEP_HEADS, HEAD_DIM, tk), lambda b, kt: (layer, b, 0, 0, tile_of(kt)))

    def softmax_call(mode, branch, kc, vc, extra_args, extra_specs, name):
        past = kc.shape[-1]
        tk = min(past, 1024)
        assert past % tk == 0
        q, k_new, v_new = new[branch]
        if mode == "A":
            vc_spec = pl.BlockSpec((1, 1, tk, 4, LANES), lambda b, kt: (layer, b, kt, 0, 0))
            vn_spec = pl.BlockSpec((1, 1, frames, 4, LANES), lambda b, kt: (0, b, 0, 0, 0))
            dv = LANES
        else:
            vc_spec, vn_spec, dv = cache_t_spec(tk, forward), new_t_spec, HEAD_DIM
        return pl.pallas_call(
            functools.partial(_softmax_step_kernel, mode=mode, frames=frames, lam_init=lam_init),
            out_shape=out_shape,
            grid=(batch, past // tk),
            in_specs=[q_spec, cache_t_spec(tk, forward), vc_spec, new_t_spec, vn_spec] + extra_specs(tk),
            out_specs=out_spec,
            scratch_shapes=[pltpu.VMEM((N_STEP_HEADS, frames, HEAD_DIM), BF16), pltpu.VMEM((n_rows, 1), F32),
                            pltpu.VMEM((n_rows, 1), F32), pltpu.VMEM((n_rows, dv), F32)],
            compiler_params=_cparams(sem),
            name=name,
        )(q, kc, vc, k_new[None], v_new[None], *extra_args)

    o_a = softmax_call(
        "A", "a", caches["a_k"], caches["a_v"], (lam_vec, subln_w.reshape(1, LANES)),
        lambda tk: [pl.BlockSpec((4, HEAD_DIM), lambda b, kt: (0, 0)),
                    pl.BlockSpec((1, LANES), lambda b, kt: (0, 0))],
        "diff_attention_step")
    o_b = softmax_call(
        "B", "b", caches["b_k"], caches["b_v"], (bias_cache, bias_new),
        lambda tk: [pl.BlockSpec((n_rows, tk), lambda b, kt: (0, kt)),
                    pl.BlockSpec((n_rows, frames), lambda b, kt: (0, 0))],
        "band_attention_step")
    o_c = softmax_call(
        "C", "c", caches["c_k"], caches["c_v"], (cum_q_col, cum_cache_row, cum_new_row),
        lambda tk: [pl.BlockSpec((1, n_rows, 1), lambda b, kt: (b, 0, 0)),
                    pl.BlockSpec((1, N_STEP_HEADS, tk), lambda b, kt: (b, 0, kt)),
                    pl.BlockSpec((1, N_STEP_HEADS, frames), lambda b, kt: (b, 0, 0))],
        "forgetting_attention_step")

    past = caches["d_k"].shape[-1]
    tk = min(past, 512)
    assert past % tk == 0 and tk % SUFFIX_BLOCK == 0
    n_kt = past // tk
    backward = lambda kt: n_kt - 1 - kt
    ones_after = jnp.tril(jnp.ones((SUFFIX_BLOCK, SUFFIX_BLOCK), F32), -1).astype(BF16)
    q, k_new, v_new = new["d"]
    o_d = pl.pallas_call(
        functools.partial(_stick_step_kernel, frames=frames),
        out_shape=out_shape,
        grid=(batch, n_kt),
        in_specs=[q_spec, cache_t_spec(tk, backward), cache_t_spec(tk, backward), new_t_spec, new_t_spec,
                  pl.BlockSpec((SUFFIX_BLOCK, SUFFIX_BLOCK), lambda b, kt: (0, 0))],
        out_specs=out_spec,
        scratch_shapes=[pltpu.VMEM((N_STEP_HEADS, frames, HEAD_DIM), BF16), pltpu.VMEM((n_rows, 1), F32),
                        pltpu.VMEM((n_rows, HEAD_DIM), F32)],
        compiler_params=_cparams(sem),
        name="stick_breaking_attention_step",
    )(q, caches["d_k"], caches["d_v"], k_new[None], v_new[None], ones_after)
    return o_a, o_b, o_c, o_d


def _merge_kernel(x_ref, g0_ref, g1_ref, g2_ref, g3_ref, b0_ref, b1_ref, b2_ref, b3_ref, wb_ref, o_ref,
                  gs_ref, ws_ref):
    @pl.when(pl.program_id(1) == 0)
    def _():
        for n, g_ref in enumerate((g0_ref, g1_ref, g2_ref, g3_ref)):
            gs_ref[n] = g_ref[0].astype(BF16)
        ws_ref[...] = wb_ref[0].astype(BF16)

    x = x_ref[...]
    acc = None
    for n, b_ref in enumerate((b0_ref, b1_ref, b2_ref, b3_ref)):
        term = _sigmoid(_dot_nt(x, gs_ref[n])) * _dot(b_ref[...], ws_ref[n])
        acc = term if acc is None else acc + term
    o_ref[...] = acc.astype(o_ref.dtype)


def gated_merge(xn, w_in_t, gate_row0, w_branch, layer, branches):
    t, d = xn.shape
    wcol = 256
    tm = _row_tile(t, 512)
    bw = branches[0].shape[1]
    gate_specs = [pl.BlockSpec((pl.Element(1), pl.Element(wcol), pl.Element(d)),
                               functools.partial(lambda c, m, n: (layer, pl.multiple_of(gate_row0 + n * d + c * wcol, 8), 0), n=n))
                  for n in range(N_BRANCH)]
    branch_specs = [pl.BlockSpec((tm, bw), lambda c, m: (m, 0)) for _ in range(N_BRANCH)]
    return pl.pallas_call(
        _merge_kernel,
        out_shape=jax.ShapeDtypeStruct((t, d), BF16),
        grid=(d // wcol, t // tm),
        in_specs=[pl.BlockSpec((tm, d), lambda c, m: (m, 0))] + gate_specs + branch_specs
                 + [pl.BlockSpec((1, N_BRANCH, bw, wcol), lambda c, m: (layer, 0, 0, c))],
        out_specs=pl.BlockSpec((tm, wcol), lambda c, m: (m, c)),
        scratch_shapes=[pltpu.VMEM((N_BRANCH, wcol, d), BF16), pltpu.VMEM((N_BRANCH, bw, wcol), BF16)],
        compiler_params=_cparams(("parallel", "arbitrary")),
        name="gated_merge",
    )(xn, w_in_t, w_in_t, w_in_t, w_in_t, *branches, w_branch)


def _outproj_kernel(a_ref, w_ref, x_ref, o_ref):
    o_ref[...] = x_ref[...] + _dot(a_ref[...], w_ref[0])


def out_projection(merged, w_out, layer, x):
    t, d = x.shape
    tm = _row_tile(t, 512)
    return pl.pallas_call(
        _outproj_kernel,
        out_shape=jax.ShapeDtypeStruct((t, d), F32),
        grid=(t // tm,),
        in_specs=[pl.BlockSpec((tm, d), lambda m: (m, 0)),
                  pl.BlockSpec((1, d, d), lambda m: (layer, 0, 0)),
                  pl.BlockSpec((tm, d), lambda m: (m, 0))],
        out_specs=pl.BlockSpec((tm, d), lambda m: (m, 0)),
        compiler_params=_cparams(("parallel",)),
        name="out_projection",
    )(merged, w_out, x)


def _router_kernel(h_ref, g_ref, w_ref, b_ref, xn_ref, comb_ref):
    h = h_ref[...]
    ms = jnp.mean(h * h, axis=-1, keepdims=True)
    xn = h * lax.rsqrt(ms + RMS_EPS) * g_ref[...]
    xn_ref[...] = xn.astype(xn_ref.dtype)
    logits = jnp.dot(xn, w_ref[...], preferred_element_type=F32, precision=lax.Precision.HIGHEST) + b_ref[...]
    lane = lax.broadcasted_iota(jnp.int32, logits.shape, 1).astype(F32)
    far = float(LANES)

    def first_argmax(vals):
        top = jnp.max(vals, axis=-1, keepdims=True)
        idx = jnp.min(jnp.where(vals == top, lane, far), axis=-1, keepdims=True)
        return top, idx

    group_logits = jnp.where(lane < N_GROUPS, logits, -jnp.inf)
    g_top, g_idx = first_argmax(group_logits)
    g_weight = 1.0 / jnp.sum(jnp.exp(group_logits - g_top), axis=-1, keepdims=True)
    first = N_GROUPS + EXPERTS_PER_GROUP * g_idx
    in_group = (lane >= first) & (lane < first + EXPERTS_PER_GROUP)
    expert_logits = jnp.where(in_group, logits, -jnp.inf)
    top1, idx1 = first_argmax(expert_logits)
    top2, idx2 = first_argmax(jnp.where(lane == idx1, -jnp.inf, expert_logits))
    w1 = 1.0 / (1.0 + jnp.exp(top2 - top1))
    w2 = jnp.exp(top2 - top1) * w1
    comb_ref[...] = g_weight * (jnp.where(lane == idx1, w1, 0.0) + jnp.where(lane == idx2, w2, 0.0))


def router(h, g_ffn, w_router, b_router):
    t, d = h.shape
    tm = _row_tile(t, 256)
    return pl.pallas_call(
        _router_kernel,
        out_shape=(jax.ShapeDtypeStruct((t, d), BF16), jax.ShapeDtypeStruct((t, LANES), F32)),
        grid=(t // tm,),
        in_specs=[pl.BlockSpec((tm, d), lambda m: (m, 0)),
                  pl.BlockSpec((1, d), lambda m: (0, 0)),
                  pl.BlockSpec((d, LANES), lambda m: (0, 0)),
                  pl.BlockSpec((1, LANES), lambda m: (0, 0))],
        out_specs=(pl.BlockSpec((tm, d), lambda m: (m, 0)),
                   pl.BlockSpec((tm, LANES), lambda m: (m, 0))),
        compiler_params=_cparams(("parallel",)),
        name="router",
    )(h, g_ffn.reshape(1, d), w_router, b_router)


EXPERTS_PER_STEP = 2


def _moe_kernel(x_ref, comb_ref, wg_ref, wu_ref, wd_ref, h_ref, o_ref):
    step = pl.program_id(1)

    @pl.when(step == 0)
    def _():
        o_ref[...] = h_ref[...]

    x = x_ref[...]
    comb = comb_ref[...]
    lane = lax.broadcasted_iota(jnp.int32, comb.shape, 1)
    total = None
    for i in range(EXPERTS_PER_STEP):
        e = step * EXPERTS_PER_STEP + i
        weight = jnp.sum(jnp.where(lane == N_GROUPS + e, comb, 0.0), axis=-1, keepdims=True)
        gate = _dot(x, wg_ref[0, i])
        hid = gate * _sigmoid(gate) * _dot(x, wu_ref[0, i]) * weight
        y = _dot(hid.astype(BF16), wd_ref[0, i])
        total = y if total is None else total + y
    o_ref[...] += total


def moe(xn, comb, w_gate, w_up, w_down, layer, h):
    t, d = xn.shape
    _, n_exp, _, f = w_gate.shape
    tm = _row_tile(t, 512)
    return pl.pallas_call(
        _moe_kernel,
        out_shape=jax.ShapeDtypeStruct((t, d), F32),
        grid=(t // tm, n_exp // EXPERTS_PER_STEP),
        in_specs=[pl.BlockSpec((tm, d), lambda m, e: (m, 0)),
                  pl.BlockSpec((tm, LANES), lambda m, e: (m, 0)),
                  pl.BlockSpec((1, EXPERTS_PER_STEP, d, f), lambda m, e: (layer, e, 0, 0)),
                  pl.BlockSpec((1, EXPERTS_PER_STEP, d, f), lambda m, e: (layer, e, 0, 0)),
                  pl.BlockSpec((1, EXPERTS_PER_STEP, f, d), lambda m, e: (layer, e, 0, 0)),
                  pl.BlockSpec((tm, d), lambda m, e: (m, 0))],
        out_specs=pl.BlockSpec((tm, d), lambda m, e: (m, 0)),
        compiler_params=_cparams(("parallel", "arbitrary")),
        name="moe",
    )(xn, comb, w_gate, w_up, w_down, h)


def _rope_tables(pos, width):
    half = HEAD_DIM // 2
    inv_freq = jnp.exp(jnp.arange(half, dtype=F32) * (-2.0 * math.log(ROPE_THETA) / HEAD_DIM))
    ang = pos.astype(F32)[:, None] * inv_freq[None, :]
    reps = width // half
    cos = jnp.tile(jnp.cos(ang), (1, reps))
    sin = jnp.tile(jnp.sin(ang), (1, reps))
    lower = (jnp.arange(width) % HEAD_DIM) < half
    return cos, jnp.where(lower[None, :], -sin, sin)


def _rel_bias_table(rel_bias, qpos, kpos):
    rel = jnp.clip(qpos[:, None] - kpos[None, :], -REL_CLIP, REL_CLIP) + REL_CLIP
    return rel_bias[:, rel].astype(F32)


def _band_bias_table(rel_bias, tile):
    period = 4 * tile + 1
    m = jnp.arange(period)
    diff = jnp.where(m < 3 * tile, m, m - period)
    vec = rel_bias[:, jnp.clip(LEFT_CHUNKS * CHUNK - diff, -REL_CLIP, REL_CLIP) + REL_CLIP].astype(F32)
    flat = jnp.tile(vec, (1, tile))[:, :tile * (period - 1)]
    return flat.reshape(vec.shape[0], tile, period - 1)[:, :, :3 * tile]


def _layer_weights(l, p):
    n_qkv = N_SEG * 512
    w_in = p["w_in"][l]
    n_fg = w_in.shape[1] - n_qkv - N_BRANCH * w_in.shape[0]
    w_f = jnp.pad(p["w_in_t"][l, n_qkv:n_qkv + n_fg], ((0, LANES - n_fg), (0, 0)))
    b_f = jnp.pad(p["b_forget"][l], (0, LANES - n_fg)).reshape(1, LANES)
    w_router = jnp.pad(jnp.concatenate([p["router_group_w"][l], p["router_expert_w"][l]], axis=1),
                       ((0, 0), (0, LANES - N_GROUPS - N_EXPERTS)))
    b_router = jnp.pad(jnp.concatenate([p["router_group_b"][l], p["router_expert_b"][l]]),
                       (0, LANES - N_GROUPS - N_EXPERTS)).reshape(1, LANES)
    return dict(
        w_f=w_f, b_f=b_f, n_fg=n_fg,
        gate_row0=n_qkv + n_fg, w_router=w_router, b_router=b_router,
    )


def _ffn(h, l, p, lw):
    xn2, comb = router(h, p["norm_ffn"][l], lw["w_router"], lw["b_router"])
    return moe(xn2, comb, p["e_gate"], p["e_up"], p["e_down"], l, h)


def _prompt_layer(x, l, p, lw, lam_init, batch, seq, tables, state_bufs):
    t, d = x.shape
    xn = rmsnorm(x, p["norm_mix"][l], BF16)
    qkv_bf, state_bufs = in_projection_prompt(xn, p["w_in_t"], l, tables["cos"], tables["sin"], batch, seq, state_bufs)
    width = qkv_bf.shape[-1]
    n_fg = lw["n_fg"]
    logf = log_forget(xn, lw["w_f"], lw["b_f"])[:, :n_fg].reshape(batch, seq, n_fg)
    cum_row = cumsum_lanes(jnp.swapaxes(logf, 1, 2))
    cum_col = jnp.swapaxes(cum_row.reshape(batch, n_fg // 2, 2, seq), 2, 3)
    branches = prompt_attention(
        qkv_bf.reshape(N_SEG, batch, seq, width), cum_col, tables["band_bias"][l],
        p["diff_lambda"][l], p["diff_subln"][l], lam_init, batch, seq)
    merged = gated_merge(xn, p["w_in_t"], lw["gate_row0"], p["w_branch"], l, [b.reshape(t, width) for b in branches])
    h = out_projection(merged, p["w_out_bf"], l, x)
    h = _ffn(h, l, p, lw)
    return h, state_bufs, logf


def _step_layer(x, l, p, lw, lam_init, batch, frames, caches, tables):
    t, d = x.shape
    xn = rmsnorm(x, p["norm_mix"][l], BF16)
    qkv, _ = in_projection(xn, p["w_in_t"], l, tables["cos"], tables["sin"], frames)
    width = qkv.shape[-1]
    n_fg = lw["n_fg"]
    logf = log_forget(xn, lw["w_f"], lw["b_f"])[:, :n_fg].reshape(batch, frames, n_fg)
    past_f = caches["c_logf"][l]
    past = past_f.shape[1]
    total = past + frames
    padded = -(-total // LANES) * LANES
    series = jnp.concatenate([past_f, logf], axis=1)
    cum = cumsum_lanes(jnp.pad(jnp.swapaxes(series, 1, 2), ((0, 0), (0, 0), (0, padded - total))))
    cum_cache_row = cum[:, :, :past]
    cum_new_row = cum[:, :, past:total]
    cum_q_col = cum_new_row.reshape(batch, n_fg * frames, 1)
    heads = lambda i: qkv[i].reshape(batch, frames, N_STEP_HEADS, HEAD_DIM)
    q_heads = lambda i: jnp.transpose(heads(i), (0, 2, 1, 3))
    seq_minor = lambda i: jnp.transpose(heads(i), (0, 2, 3, 1))
    new = dict(a=(q_heads(0), seq_minor(1), qkv[2].reshape(batch, frames, 4, LANES)),
               b=(q_heads(3), seq_minor(4), seq_minor(5)),
               c=(q_heads(6), seq_minor(7), seq_minor(8)),
               d=(q_heads(9), seq_minor(10), seq_minor(11)))
    branches = step_attention(
        l, new, caches, cum_q_col, cum_cache_row, cum_new_row,
        tables["bias_cache"][l], tables["bias_new"][l], p["diff_lambda"][l], p["diff_subln"][l], lam_init)
    merged = gated_merge(xn, p["w_in_t"], lw["gate_row0"], p["w_branch"], l, [b.reshape(t, width) for b in branches])
    h = out_projection(merged, p["w_out_bf"], l, x)
    h = _ffn(h, l, p, lw)
    seg = lambda i: qkv[i].reshape(batch, frames, width)
    state = (seg(1), seg(2), seg(4), seg(5), seg(7), seg(8), logf, seg(10), seg(11))
    return h, state


def kernel(x_prompt, x_sample, cache_a_k, cache_a_v, cache_b_k, cache_b_v, cache_c_k, cache_c_v, cache_c_logf, cache_d_k, cache_d_v, norm_mix, w_in, b_forget, diff_lambda, diff_subln, rel_bias, w_branch, w_out, norm_ffn, router_group_w, router_group_b, router_expert_w, router_expert_b, expert_w_gate, expert_w_up, expert_w_down, norm_final):
    p = dict(norm_mix=norm_mix, w_in=w_in, w_in_t=jnp.swapaxes(w_in, 1, 2), b_forget=b_forget, diff_lambda=diff_lambda, diff_subln=diff_subln,
             w_branch=w_branch, w_out=w_out, norm_ffn=norm_ffn, router_group_w=router_group_w,
             router_group_b=router_group_b, router_expert_w=router_expert_w, router_expert_b=router_expert_b,
             w_out_bf=w_out.astype(BF16), e_gate=expert_w_gate.astype(BF16), e_up=expert_w_up.astype(BF16),
             e_down=expert_w_down.astype(BF16))
    depth = w_in.shape[0]
    bp, sp, d = x_prompt.shape
    bs, fs, _ = x_sample.shape
    past = cache_a_k.shape[2]
    width = 512
    n_heads = width // HEAD_DIM

    def seq_minor(c):
        c = jnp.moveaxis(c, 2, -1)
        return c.reshape(c.shape[0], c.shape[1], N_STEP_HEADS, HEAD_DIM, c.shape[-1])

    caches = dict(a_k=seq_minor(cache_a_k), a_v=cache_a_v, b_k=seq_minor(cache_b_k), b_v=seq_minor(cache_b_v),
                  c_k=seq_minor(cache_c_k), c_v=seq_minor(cache_c_v), c_logf=cache_c_logf,
                  d_k=seq_minor(cache_d_k), d_v=seq_minor(cache_d_v))

    pos_p = jnp.arange(sp, dtype=jnp.int32)
    pos_s = past + jnp.arange(fs, dtype=jnp.int32)
    cos_p, sin_p = _rope_tables(pos_p, width)
    cos_s, sin_s = _rope_tables(pos_s, width)
    tile = 256
    band_bias = jnp.stack([_band_bias_table(rel_bias[l], tile).reshape(n_heads // 2, 2, tile, 3 * tile)
                           for l in range(depth)])
    wb = cache_b_k.shape[2]
    kpos_b = jnp.arange(past - wb, past + fs, dtype=jnp.int32)
    step_bias = jnp.stack([_rel_bias_table(rel_bias[l], pos_s, kpos_b).reshape(n_heads * fs, wb + fs)
                           for l in range(depth)])
    tables_p = dict(cos=cos_p, sin=sin_p, band_bias=band_bias)
    tables_s = dict(cos=cos_s, sin=sin_s, bias_cache=step_bias[:, :, :wb], bias_new=step_bias[:, :, wb:])

    hp = x_prompt.reshape(bp * sp, d)
    hs = x_sample.reshape(bs * fs, d)
    bufs_p, logf_p, st_s = prompt_state_buffers(depth, bp, sp), [], []
    for l in range(depth):
        lam_init = 0.8 - 0.6 * math.exp(-0.3 * l)
        lw = _layer_weights(l, p)
        hp, bufs_p, logf = _prompt_layer(hp, l, p, lw, lam_init, bp, sp, tables_p, bufs_p)
        hs, state_s = _step_layer(hs, l, p, lw, lam_init, bs, fs, caches, tables_s)
        logf_p.append(logf)
        st_s.append(state_s)
    y_prompt = rmsnorm(hp, norm_final, F32).reshape(bp, sp, d)
    y_sample = rmsnorm(hs, norm_final, F32).reshape(bs, fs, d)

    h_a = width // (2 * HEAD_DIM)
    state_shapes = lambda b, s, sb: (
        (b, s, h_a, 2, HEAD_DIM), (b, s, h_a, 2 * HEAD_DIM), (b, sb, n_heads, HEAD_DIM), (b, sb, n_heads, HEAD_DIM),
        (b, s, n_heads, HEAD_DIM), (b, s, n_heads, HEAD_DIM), (b, s, n_heads), (b, s, n_heads, HEAD_DIM),
        (b, s, n_heads, HEAD_DIM))

    def stack(states, shapes):
        return tuple(jnp.stack([st[i].reshape(shapes[i]) for st in states], axis=0) for i in range(len(shapes)))

    token_major = lambda buf, dims: jnp.moveaxis(buf.reshape(buf.shape[:2] + dims + buf.shape[-1:]), -1, 2)
    a_k, a_v, b_k, b_v, c_k, c_v, d_k, d_v = bufs_p
    per_head = (n_heads, HEAD_DIM)
    out_p = (token_major(a_k, (h_a, 2, HEAD_DIM)), a_v, token_major(b_k, per_head), token_major(b_v, per_head),
             token_major(c_k, per_head), token_major(c_v, per_head), jnp.stack(logf_p, axis=0),
             token_major(d_k, per_head), token_major(d_v, per_head))
    out_s = stack(st_s, state_shapes(bs, fs, fs))
    return (y_prompt, y_sample) + out_p + out_s
```

```python
import functools
import math

import jax
import jax.numpy as jnp
from jax import lax
from jax.experimental import pallas as pl
from jax.experimental.pallas import tpu as pltpu

F32 = jnp.float32
BF16 = jnp.bfloat16

CHUNK = 64
HEAD_DIM = 64
LEFT_CHUNKS = 8
REL_CLIP = 128
ROPE_THETA = 10000.0
N_BRANCH = 4
N_GROUPS = 4
EXPERTS_PER_GROUP = 4
N_EXPERTS = N_GROUPS * EXPERTS_PER_GROUP
RMS_EPS = 1e-6
NEG_INF = -1e30
Q_SCALE = HEAD_DIM ** -0.5
N_SEG = 12

LANES = 128
VMEM_LIMIT = 56 * 1024 * 1024

NT_DIMS = (((1,), (1,)), ((), ()))


def _cparams(sem):
    return pltpu.CompilerParams(dimension_semantics=sem, vmem_limit_bytes=VMEM_LIMIT)


def _row_tile(t, pref):
    return pref if t % pref == 0 else t


def _sigmoid(z):
    return 1.0 / (1.0 + jnp.exp(-z))


def _dot(a, b):
    return jnp.dot(a, b, preferred_element_type=F32)


def _dot_nt(a, b):
    return lax.dot_general(a, b, NT_DIMS, preferred_element_type=F32)


def _rms_kernel(x_ref, g_ref, o_ref):
    x = x_ref[...]
    ms = jnp.mean(x * x, axis=-1, keepdims=True)
    o_ref[...] = (x * lax.rsqrt(ms + RMS_EPS) * g_ref[...]).astype(o_ref.dtype)


def rmsnorm(x, g, out_dtype):
    t, d = x.shape
    tm = _row_tile(t, 512)
    return pl.pallas_call(
        _rms_kernel,
        out_shape=jax.ShapeDtypeStruct((t, d), out_dtype),
        grid=(t // tm,),
        in_specs=[pl.BlockSpec((tm, d), lambda m: (m, 0)),
                  pl.BlockSpec((1, d), lambda m: (0, 0))],
        out_specs=pl.BlockSpec((tm, d), lambda m: (m, 0)),
        compiler_params=_cparams(("parallel",)),
        name="rmsnorm",
    )(x, g.reshape(1, d))


def _inproj_kernel(x_ref, w_ref, cos_ref, sin_ref, o_ref, ob_ref, wb_ref):
    n = pl.program_id(0)

    @pl.when(pl.program_id(1) == 0)
    def _():
        wb_ref[...] = w_ref[0].astype(BF16)

    acc = _dot_nt(x_ref[...], wb_ref[...])
    q_scale = jnp.where(n % 3 == 0, Q_SCALE, 1.0)

    @pl.when(n < 2)
    def _():
        width = acc.shape[1]
        half = HEAD_DIM // 2
        lane = lax.broadcasted_iota(jnp.int32, acc.shape, 1)
        lower = (lane % HEAD_DIM) < half
        partner = jnp.where(lower, pltpu.roll(acc, width - half, 1), pltpu.roll(acc, half, 1))
        roped = acc * cos_ref[...] + partner * sin_ref[...]
        o_ref[0] = roped
        ob_ref[0] = (roped * q_scale).astype(ob_ref.dtype)

    @pl.when(n >= 2)
    def _():
        o_ref[0] = acc
        ob_ref[0] = (acc * q_scale).astype(ob_ref.dtype)


def in_projection(xn, w_in_t, layer, cos_tab, sin_tab, seq):
    t, d = xn.shape
    w = 512
    tm = _row_tile(t, 512)
    if seq % tm == 0:
        n_tab = seq // tm
    else:
        assert tm % seq == 0
        cos_tab = jnp.tile(cos_tab, (tm // seq, 1))
        sin_tab = jnp.tile(sin_tab, (tm // seq, 1))
        n_tab = 1
    return pl.pallas_call(
        _inproj_kernel,
        out_shape=(jax.ShapeDtypeStruct((N_SEG, t, w), F32), jax.ShapeDtypeStruct((N_SEG, t, w), BF16)),
        grid=(N_SEG, t // tm),
        in_specs=[pl.BlockSpec((tm, d), lambda n, m: (m, 0)),
                  pl.BlockSpec((1, w, d), lambda n, m: (layer, n, 0)),
                  pl.BlockSpec((tm, w), lambda n, m: (m % n_tab, 0)),
                  pl.BlockSpec((tm, w), lambda n, m: (m % n_tab, 0))],
        out_specs=(pl.BlockSpec((1, tm, w), lambda n, m: (n, m, 0)),
                   pl.BlockSpec((1, tm, w), lambda n, m: (n, m, 0))),
        scratch_shapes=[pltpu.VMEM((w, d), BF16)],
        compiler_params=_cparams(("parallel", "arbitrary")),
        name="in_projection",
    )(xn, w_in_t, cos_tab, sin_tab)


STATE_SEGS = (1, 2, 4, 5, 7, 8, 10, 11)
VALUE_A_SEG = 2
BAND_SEGS = (4, 5)


def _inproj_prompt_kernel(*refs, n_alias, tiles_per_seq):
    x_ref, w_ref, cos_ref, sin_ref = refs[:4]
    ob_ref = refs[4 + n_alias]
    st_refs = refs[5 + n_alias:5 + n_alias + len(STATE_SEGS)]
    wb_ref = refs[-1]
    n, m = pl.program_id(0), pl.program_id(1)

    @pl.when(m == 0)
    def _():
        wb_ref[...] = w_ref[0].astype(BF16)

    acc = _dot_nt(x_ref[...], wb_ref[...])
    q_scale = jnp.where(n % 3 == 0, Q_SCALE, 1.0)

    def emit(val):
        ob_ref[0] = (val * q_scale).astype(ob_ref.dtype)
        for seg, st_ref in zip(STATE_SEGS, st_refs):
            if seg == VALUE_A_SEG:
                @pl.when(n == seg)
                def _(st_ref=st_ref):
                    for h in range(st_ref.shape[3]):
                        st_ref[0, 0, :, h, :] = val[:, h * LANES:(h + 1) * LANES]
            elif seg in BAND_SEGS:
                @pl.when(jnp.logical_and(n == seg, m % tiles_per_seq == tiles_per_seq - 1))
                def _(st_ref=st_ref):
                    st_ref[0, 0] = val.T
            else:
                @pl.when(n == seg)
                def _(st_ref=st_ref):
                    st_ref[0, 0] = val.T

    @pl.when(n < 2)
    def _():
        width = acc.shape[1]
        half = HEAD_DIM // 2
        lane = lax.broadcasted_iota(jnp.int32, acc.shape, 1)
        lower = (lane % HEAD_DIM) < half
        partner = jnp.where(lower, pltpu.roll(acc, width - half, 1), pltpu.roll(acc, half, 1))
        emit(acc * cos_ref[...] + partner * sin_ref[...])

    @pl.when(n >= 2)
    def _():
        emit(acc)


def prompt_state_buffers(depth, batch, seq, w=512):
    band = LEFT_CHUNKS * CHUNK
    shapes = []
    for seg in STATE_SEGS:
        if seg == VALUE_A_SEG:
            shapes.append((depth, batch, seq, w // LANES, LANES))
        elif seg in BAND_SEGS:
            shapes.append((depth, batch, w, band))
        else:
            shapes.append((depth, batch, w, seq))
    return tuple(jnp.zeros(s, F32) for s in shapes)


def in_projection_prompt(xn, w_in_t, layer, cos_tab, sin_tab, batch, seq, state_bufs):
    t, d = xn.shape
    w = 512
    tm = 512
    band = LEFT_CHUNKS * CHUNK
    assert seq % tm == 0 and band == tm and t == batch * seq
    per_seq = seq // tm
    n_tiles = t // tm

    def parked(seg, n, m):
        return jnp.where(n < seg, 0, jnp.where(n > seg, n_tiles - 1, m))

    specs = []
    for seg in STATE_SEGS:
        if seg == VALUE_A_SEG:
            specs.append(pl.BlockSpec((1, 1, tm, w // LANES, LANES), functools.partial(
                lambda n, m, seg: (layer, parked(seg, n, m) // per_seq, parked(seg, n, m) % per_seq, 0, 0), seg=seg)))
        elif seg in BAND_SEGS:
            specs.append(pl.BlockSpec((1, 1, w, band), functools.partial(
                lambda n, m, seg: (layer, parked(seg, n, m) // per_seq, 0, 0), seg=seg)))
        else:
            specs.append(pl.BlockSpec((1, 1, w, tm), functools.partial(
                lambda n, m, seg: (layer, parked(seg, n, m) // per_seq, 0, parked(seg, n, m) % per_seq), seg=seg)))
    alias_in = list(state_bufs)
    shapes = [jax.ShapeDtypeStruct(b.shape, b.dtype) for b in alias_in]
    n_fixed = 4
    out = pl.pallas_call(
        functools.partial(_inproj_prompt_kernel, n_alias=len(alias_in), tiles_per_seq=per_seq),
        out_shape=[jax.ShapeDtypeStruct((N_SEG, t, w), BF16)] + shapes,
        grid=(N_SEG, n_tiles),
        in_specs=[pl.BlockSpec((tm, d), lambda n, m: (m, 0)),
                  pl.BlockSpec((1, w, d), lambda n, m: (layer, n, 0)),
                  pl.BlockSpec((tm, w), lambda n, m: (m % per_seq, 0)),
                  pl.BlockSpec((tm, w), lambda n, m: (m % per_seq, 0))]
                 + [pl.BlockSpec(memory_space=pl.ANY) for _ in alias_in],
        out_specs=[pl.BlockSpec((1, tm, w), lambda n, m: (n, m, 0))] + specs,
        input_output_aliases={n_fixed + i: 1 + i for i in range(len(alias_in))},
        scratch_shapes=[pltpu.VMEM((w, d), BF16)],
        compiler_params=_cparams(("arbitrary", "arbitrary")),
        name="in_projection_prompt",
    )(xn, w_in_t, cos_tab, sin_tab, *alias_in)
    return out[0], tuple(out[1:])


def _logf_kernel(x_ref, w_ref, b_ref, o_ref):
    z = _dot_nt(x_ref[...], w_ref[...].astype(BF16)) + b_ref[...]
    o_ref[...] = jnp.minimum(z, 0.0) - jnp.log1p(jnp.exp(-jnp.abs(z)))


def log_forget(xn, w_f, b_f):
    t, d = xn.shape
    tm = _row_tile(t, 512)
    return pl.pallas_call(
        _logf_kernel,
        out_shape=jax.ShapeDtypeStruct((t, LANES), F32),
        grid=(t // tm,),
        in_specs=[pl.BlockSpec((tm, d), lambda m: (m, 0)),
                  pl.BlockSpec((LANES, d), lambda m: (0, 0)),
                  pl.BlockSpec((1, LANES), lambda m: (0, 0))],
        out_specs=pl.BlockSpec((tm, LANES), lambda m: (m, 0)),
        compiler_params=_cparams(("parallel",)),
        name="log_forget",
    )(xn, w_f, b_f)


def _cumsum_kernel(x_ref, o_ref):
    x = x_ref[0]
    length = x.shape[1]
    lane = lax.broadcasted_iota(jnp.int32, x.shape, 1)
    shift = 1
    while shift < length:
        x = x + jnp.where(lane >= shift, pltpu.roll(x, shift, 1), 0.0)
        shift *= 2
    o_ref[0] = x


def cumsum_lanes(x):
    b, h, length = x.shape
    return pl.pallas_call(
        _cumsum_kernel,
        out_shape=jax.ShapeDtypeStruct(x.shape, F32),
        grid=(b,),
        in_specs=[pl.BlockSpec((1, h, length), lambda i: (i, 0, 0))],
        out_specs=pl.BlockSpec((1, h, length), lambda i: (i, 0, 0)),
        compiler_params=_cparams(("parallel",)),
        name="cumsum",
    )(x)


def _diff_lambda(lam_ref, lam_init):
    lv = lam_ref[...]
    a = jnp.sum(lv[0:1] * lv[1:2], axis=1, keepdims=True)
    b = jnp.sum(lv[2:3] * lv[3:4], axis=1, keepdims=True)
    return jnp.exp(a) - jnp.exp(b) + lam_init


def _subln(o, w, lam_init):
    ms = jnp.mean(o * o, axis=-1, keepdims=True)
    return o * lax.rsqrt(ms + RMS_EPS) * w * (1.0 - lam_init)


def _head_lanes(shape, j):
    lane = lax.broadcasted_iota(jnp.int32, shape, 1)
    return (lane >= j * HEAD_DIM) & (lane < (j + 1) * HEAD_DIM)


def _split3(x):
    hi = x.astype(BF16).astype(F32)
    mid = (x - hi).astype(BF16).astype(F32)
    lo = (x - hi - mid).astype(BF16).astype(F32)
    return hi, mid, lo


def _with_bias_columns(x, j, cols):
    lane = lax.broadcasted_iota(jnp.int32, x.shape, 1)
    out = jnp.where(_head_lanes(x.shape, j), x.astype(F32), 0.0)
    base = (1 - j) * HEAD_DIM
    for i, c in enumerate(cols):
        out = jnp.where(lane == base + i, c, out)
    return out.astype(BF16)


def _softmax_tile_update(state, s, v):
    m, l, acc = state
    m_new = jnp.maximum(m, jnp.max(s, axis=-1, keepdims=True))
    alpha = jnp.exp(m - m_new)
    p = jnp.exp(s - m_new)
    return (m_new, alpha * l + jnp.sum(p, axis=-1, keepdims=True), alpha * acc + _dot(p.astype(BF16), v))


def _causal_softmax_sweep(qs, k_rows, v_rows, diag_mask, qi, tile):
    init = tuple((jnp.full((tile, 1), NEG_INF, F32), jnp.zeros((tile, 1), F32), jnp.zeros((tile, LANES), F32))
                 for _ in qs)

    def update(states, start, size, mask):
        start = pl.multiple_of(start, tile)
        v = v_rows(start, size)
        out = []
        for j, q in enumerate(qs):
            s = _dot_nt(q, k_rows(j, start, size))
            if mask is not None:
                s = jnp.where(mask, s, NEG_INF)
            out.append(_softmax_tile_update(states[j], s, v))
        return tuple(out)

    pairs = qi // 2
    states = lax.fori_loop(0, pairs, lambda i, st: update(st, i * 2 * tile, 2 * tile, None), init)
    states = lax.fori_loop(0, qi - 2 * pairs, lambda i, st: update(st, pairs * 2 * tile, tile, None), states)
    return update(states, qi * tile, tile, diag_mask)


def _a_prompt_kernel(q_ref, k_ref, v_ref, lam_ref, w_ref, o_ref, *, tile, lam_init):
    qi = pl.program_id(2)
    q = q_ref[0, 0]
    qs = [jnp.where(_head_lanes(q.shape, j), q, jnp.zeros_like(q)) for j in range(2)]
    row = lax.broadcasted_iota(jnp.int32, (tile, tile), 0)
    col = lax.broadcasted_iota(jnp.int32, (tile, tile), 1)
    diag_mask = (col // CHUNK) <= (row // CHUNK)
    k_rows = lambda j, start, size: k_ref[0, 0, pl.ds(start, size), :]
    v_rows = lambda start, size: v_ref[0, 0, pl.ds(start, size), :]
    (_, l1, acc1), (_, l2, acc2) = _causal_softmax_sweep(qs, k_rows, v_rows, diag_mask, qi, tile)
    lam = _diff_lambda(lam_ref, lam_init)
    o = acc1 / l1 - lam * (acc2 / l2)
    o_ref[0] = _subln(o, w_ref[...], lam_init).astype(o_ref.dtype)


def _c_prompt_kernel(q_ref, k_ref, v_ref, cq_ref, ck_ref, o_ref, kb_ref, *, tile):
    qi = pl.program_id(2)
    one = jnp.float32(1.0)

    @pl.when(qi == 0)
    def _():
        k = k_ref[0, 0]
        ck = ck_ref[0, 0]
        for j in range(2):
            hi, mid, lo = _split3(-ck[:, j:j + 1])
            kb_ref[j] = _with_bias_columns(k, j, [one, one, one, hi, mid, lo])

    q = q_ref[0, 0]
    cq = cq_ref[0, 0]
    qs = []
    for j in range(2):
        hi, mid, lo = _split3(cq[:, j:j + 1])
        qs.append(_with_bias_columns(q, j, [hi, mid, lo, one, one, one]))
    row = lax.broadcasted_iota(jnp.int32, (tile, tile), 0)
    col = lax.broadcasted_iota(jnp.int32, (tile, tile), 1)
    k_rows = lambda j, start, size: kb_ref[j, pl.ds(start, size), :]
    v_rows = lambda start, size: v_ref[0, 0, pl.ds(start, size), :]
    (_, l1, acc1), (_, l2, acc2) = _causal_softmax_sweep(qs, k_rows, v_rows, col <= row, qi, tile)
    o_ref[0] = jnp.where(_head_lanes(acc1.shape, 0), acc1 / l1, acc2 / l2).astype(o_ref.dtype)


SUFFIX_BLOCK = 256


def _strict_upper_ones(n):
    r = lax.broadcasted_iota(jnp.int32, (n, n), 0)
    c = lax.broadcasted_iota(jnp.int32, (n, n), 1)
    return jnp.where(r > c, 1.0, 0.0).astype(BF16)


def _suffix_sums(lm, ones_after, two_pass):
    hi = lm.astype(BF16)
    out = _dot(hi, ones_after)
    if two_pass:
        out = out + _dot((lm - hi.astype(F32)).astype(BF16), ones_after)
    return out


def _stick_weights(z, mask, run, ones_after, two_pass=True):
    sp = jnp.maximum(z, 0.0) + jnp.log(1.0 + jnp.exp(-jnp.abs(z)))
    lm = -sp if mask is None else jnp.where(mask, -sp, 0.0)
    block = ones_after.shape[0]
    n_block = z.shape[1] // block
    later = [None] * n_block
    for c in reversed(range(n_block)):
        part = lm[:, c * block:(c + 1) * block]
        later[c] = _suffix_sums(part, ones_after, two_pass) + run
        run = run + jnp.sum(part, axis=-1, keepdims=True)
    later = later[0] if n_block == 1 else jnp.concatenate(later, axis=1)
    a = jnp.exp((z - sp) + later)
    if mask is not None:
        a = jnp.where(mask, a, 0.0)
    return a, run


def _d_prompt_kernel(q_ref, k_ref, v_ref, o_ref, *, tile):
    qi = pl.program_id(2)
    q = q_ref[0, 0]
    qs = [jnp.where(_head_lanes(q.shape, j), q, jnp.zeros_like(q)) for j in range(2)]
    ones_after = _strict_upper_ones(SUFFIX_BLOCK)
    row = lax.broadcasted_iota(jnp.int32, (tile, tile), 0)
    col = lax.broadcasted_iota(jnp.int32, (tile, tile), 1)

    def update(states, start, size, mask):
        start = pl.multiple_of(start, tile)
        k = k_ref[0, 0, pl.ds(start, size), :]
        v = v_ref[0, 0, pl.ds(start, size), :]
        out = []
        for j, q_j in enumerate(qs):
            run, acc = states[j]
            a, run = _stick_weights(_dot_nt(q_j, k), mask, run, ones_after, two_pass=False)
            out.append((run, acc + _dot(a.astype(BF16), v)))
        return tuple(out)

    init = tuple((jnp.zeros((tile, 1), F32), jnp.zeros((tile, LANES), F32)) for _ in qs)
    states = update(init, qi * tile, tile, col < row)
    pairs = qi // 2
    states = lax.fori_loop(0, pairs, lambda i, st: update(st, (qi - 2 - 2 * i) * tile, 2 * tile, None), states)
    states = lax.fori_loop(0, qi - 2 * pairs, lambda i, st: update(st, 0, tile, None), states)
    o_ref[0] = jnp.where(_head_lanes((tile, LANES), 0), states[0][1], states[1][1]).astype(o_ref.dtype)


def _b_prompt_kernel(q_ref, k0_ref, k1_ref, k2_ref, v0_ref, v1_ref, v2_ref, bias_ref, o_ref, *, tile):
    qi = pl.program_id(2)
    q = q_ref[0, 0]
    row_chunk = lax.broadcasted_iota(jnp.int32, (tile, tile), 0) // CHUNK
    col_chunk = lax.broadcasted_iota(jnp.int32, (tile, tile), 1) // CHUNK
    back = 2
    ks = [r[0, 0] for r in (k0_ref, k1_ref, k2_ref)]
    vs = [r[0, 0] for r in (v0_ref, v1_ref, v2_ref)]
    outs = []
    for j in range(2):
        q_j = jnp.where(_head_lanes(q.shape, j), q, jnp.zeros_like(q))
        scores = []
        for c in range(3):
            dist = row_chunk - col_chunk + (back - c) * (tile // CHUNK)
            valid = (dist >= 0) & (dist <= LEFT_CHUNKS) & (qi - back + c >= 0)
            s = _dot_nt(q_j, ks[c]) + bias_ref[0, j, :, c * tile:(c + 1) * tile]
            scores.append(jnp.where(valid, s, NEG_INF))
        m = jnp.maximum(jnp.maximum(jnp.max(scores[0], axis=-1, keepdims=True),
                                    jnp.max(scores[1], axis=-1, keepdims=True)),
                        jnp.max(scores[2], axis=-1, keepdims=True))
        ps = [jnp.exp(s - m) for s in scores]
        denom = sum(jnp.sum(p, axis=-1, keepdims=True) for p in ps)
        o = sum(_dot(p.astype(BF16), v) for p, v in zip(ps, vs))
        outs.append(o / denom)
    o_ref[0] = jnp.where(_head_lanes(outs[0].shape, 0), outs[0], outs[1]).astype(o_ref.dtype)


def prompt_attention(qkv, cum_col, band_bias, lam_vec, subln_w, lam_init, batch, seq):
    width = qkv.shape[-1]
    n_pair = width // LANES
    out_shape = jax.ShapeDtypeStruct((batch, seq, width), BF16)
    tile = 512
    assert seq % tile == 0 and tile % CHUNK == 0
    sem = ("parallel", "parallel", "arbitrary")

    q_spec = lambda seg: pl.BlockSpec((1, 1, tile, LANES), lambda b, h, qi: (seg, b, qi, h))
    full_spec = lambda seg: pl.BlockSpec((1, 1, seq, LANES), lambda b, h, qi: (seg, b, 0, h))
    out_spec = pl.BlockSpec((1, tile, LANES), lambda b, h, qi: (b, qi, h))
    grid = (batch, n_pair, seq // tile)

    o_a = pl.pallas_call(
        functools.partial(_a_prompt_kernel, tile=tile, lam_init=lam_init),
        out_shape=out_shape, grid=grid,
        in_specs=[q_spec(0), full_spec(1), full_spec(2),
                  pl.BlockSpec((4, HEAD_DIM), lambda b, h, qi: (0, 0)),
                  pl.BlockSpec((1, LANES), lambda b, h, qi: (0, 0))],
        out_specs=out_spec,
        compiler_params=_cparams(sem),
        name="diff_attention",
    )(qkv, qkv, qkv, lam_vec, subln_w.reshape(1, LANES))

    o_c = pl.pallas_call(
        functools.partial(_c_prompt_kernel, tile=tile),
        out_shape=out_shape, grid=grid,
        in_specs=[q_spec(6), full_spec(7), full_spec(8),
                  pl.BlockSpec((1, 1, tile, 2), lambda b, h, qi: (b, h, qi, 0)),
                  pl.BlockSpec((1, 1, seq, 2), lambda b, h, qi: (b, h, 0, 0))],
        out_specs=out_spec,
        scratch_shapes=[pltpu.VMEM((2, seq, LANES), BF16)],
        compiler_params=_cparams(sem),
        name="forgetting_attention",
    )(qkv, qkv, qkv, cum_col, cum_col)

    o_d = pl.pallas_call(
        functools.partial(_d_prompt_kernel, tile=tile),
        out_shape=out_shape, grid=grid,
        in_specs=[q_spec(9), full_spec(10), full_spec(11)],
        out_specs=out_spec,
        compiler_params=_cparams(sem),
        name="stick_breaking_attention",
    )(qkv, qkv, qkv)

    band_tile = 256
    assert seq % band_tile == 0 and band_tile * 2 == LEFT_CHUNKS * CHUNK

    def band_spec(seg, back):
        return pl.BlockSpec((1, 1, band_tile, LANES), lambda b, h, qi: (seg, b, jnp.maximum(qi - back, 0), h))

    o_b = pl.pallas_call(
        functools.partial(_b_prompt_kernel, tile=band_tile),
        out_shape=out_shape,
        grid=(batch, n_pair, seq // band_tile),
        in_specs=[band_spec(3, 0),
                  band_spec(4, 2), band_spec(4, 1), band_spec(4, 0),
                  band_spec(5, 2), band_spec(5, 1), band_spec(5, 0),
                  pl.BlockSpec((1, 2, band_tile, 3 * band_tile), lambda b, h, qi: (h, 0, 0, 0))],
        out_specs=pl.BlockSpec((1, band_tile, LANES), lambda b, h, qi: (b, qi, h)),
        compiler_params=_cparams(("parallel", "parallel", "parallel")),
        name="band_attention",
    )(qkv, qkv, qkv, qkv, qkv, qkv, qkv, band_bias)
    return o_a, o_b, o_c, o_d


N_STEP_HEADS = 8


def _step_scores(qs_ref, kt_ref):
    rows = [_dot(qs_ref[h], kt_ref[0, 0, h].astype(BF16)) for h in range(N_STEP_HEADS)]
    return jnp.concatenate(rows, axis=0)


def _step_pv(p, v_ref, frames, v_token_major):
    pb = p.astype(BF16)
    if v_token_major:
        rows = [_dot(pb[2 * g * frames:(2 * g + 2) * frames], v_ref[0, 0, :, g, :].astype(BF16))
                for g in range(N_STEP_HEADS // 2)]
    else:
        rows = [_dot_nt(pb[h * frames:(h + 1) * frames], v_ref[0, 0, h].astype(BF16))
                for h in range(N_STEP_HEADS)]
    return jnp.concatenate(rows, axis=0)


def _store_scaled_queries(q_ref, qs_ref):
    for h in range(N_STEP_HEADS):
        qs_ref[h] = (q_ref[0, h] * Q_SCALE).astype(BF16)


def _rows_per_head(x, frames):
    return jnp.concatenate([jnp.broadcast_to(x[h:h + 1, :], (frames, x.shape[1]))
                            for h in range(x.shape[0])], axis=0)


def _new_rows_mask(frames, n_rows, strict):
    frame = lax.broadcasted_iota(jnp.int32, (n_rows, frames), 0) % frames
    key = lax.broadcasted_iota(jnp.int32, (n_rows, frames), 1)
    return key < frame if strict else key <= frame


def _heads_to_lanes(acc, frames):
    return jnp.concatenate([acc[h * frames:(h + 1) * frames] for h in range(N_STEP_HEADS)], axis=1)


def _softmax_step_kernel(*refs, mode, frames, lam_init):
    if mode == "A":
        (q_ref, kc_ref, vc_ref, kn_ref, vn_ref, lam_ref, w_ref,
         o_ref, qs_ref, m_ref, l_ref, acc_ref) = refs
    elif mode == "B":
        (q_ref, kc_ref, vc_ref, kn_ref, vn_ref, bias_c_ref, bias_n_ref,
         o_ref, qs_ref, m_ref, l_ref, acc_ref) = refs
    else:
        (q_ref, kc_ref, vc_ref, kn_ref, vn_ref, cq_ref, ckc_ref, ckn_ref,
         o_ref, qs_ref, m_ref, l_ref, acc_ref) = refs
    n_rows = N_STEP_HEADS * frames
    token_major_v = mode == "A"
    kt = pl.program_id(1)

    @pl.when(kt == 0)
    def _():
        _store_scaled_queries(q_ref, qs_ref)
        m_ref[...] = jnp.full(m_ref.shape, NEG_INF, F32)
        l_ref[...] = jnp.zeros(l_ref.shape, F32)
        acc_ref[...] = jnp.zeros(acc_ref.shape, F32)

    def softmax_tile(s, v_ref):
        m_prev = m_ref[...]
        m_new = jnp.maximum(m_prev, jnp.max(s, axis=-1, keepdims=True))
        alpha = jnp.exp(m_prev - m_new)
        p = jnp.exp(s - m_new)
        l_ref[...] = alpha * l_ref[...] + jnp.sum(p, axis=-1, keepdims=True)
        acc_ref[...] = alpha * acc_ref[...] + _step_pv(p, v_ref, frames, token_major_v)
        m_ref[...] = m_new

    s = _step_scores(qs_ref, kc_ref)
    if mode == "B":
        s = s + bias_c_ref[...]
    elif mode == "C":
        s = s + cq_ref[0] - _rows_per_head(ckc_ref[0], frames)
    softmax_tile(s, vc_ref)

    @pl.when(kt == pl.num_programs(1) - 1)
    def _():
        s = _step_scores(qs_ref, kn_ref)
        if mode == "B":
            s = s + bias_n_ref[...]
        elif mode == "C":
            s = s + cq_ref[0] - _rows_per_head(ckn_ref[0], frames)
            s = jnp.where(_new_rows_mask(frames, n_rows, strict=False), s, NEG_INF)
        softmax_tile(s, vn_ref)
        o = acc_ref[...] / l_ref[...]
        if mode == "A":
            lam = _diff_lambda(lam_ref, lam_init)
            pieces = []
            for h in range(N_STEP_HEADS // 2):
                o1 = o[(2 * h) * frames:(2 * h + 1) * frames]
                o2 = o[(2 * h + 1) * frames:(2 * h + 2) * frames]
                pieces.append(_subln(o1 - lam * o2, w_ref[...], lam_init))
            o_ref[0] = jnp.concatenate(pieces, axis=1).astype(o_ref.dtype)
        else:
            o_ref[0] = _heads_to_lanes(o, frames).astype(o_ref.dtype)


def _stick_step_kernel(q_ref, kc_ref, vc_ref, kn_ref, vn_ref, ones_ref, o_ref,
                       qs_ref, run_ref, acc_ref, *, frames):
    n_rows = N_STEP_HEADS * frames
    kt = pl.program_id(1)

    @pl.when(kt == 0)
    def _():
        _store_scaled_queries(q_ref, qs_ref)
        z = _step_scores(qs_ref, kn_ref)
        mask = _new_rows_mask(frames, n_rows, strict=True)
        a, run = _stick_weights(z, mask, jnp.zeros((n_rows, 1), F32), _strict_upper_ones(frames))
        acc_ref[...] = _step_pv(a, vn_ref, frames, False)
        run_ref[...] = run

    z = _step_scores(qs_ref, kc_ref)
    a, run = _stick_weights(z, None, run_ref[...], ones_ref[...])
    acc_ref[...] += _step_pv(a, vc_ref, frames, False)
    run_ref[...] = run

    @pl.when(kt == pl.num_programs(1) - 1)
    def _():
        o_ref[0] = _heads_to_lanes(acc_ref[...], frames).astype(o_ref.dtype)


def step_attention(layer, new, caches, cum_q_col, cum_cache_row, cum_new_row, bias_cache, bias_new,
                   lam_vec, subln_w, lam_init):
    batch, _, frames, _ = new["a"][0].shape
    width = N_STEP_HEADS * HEAD_DIM
    n_rows = N_STEP_HEADS * frames
    out_shape = jax.ShapeDtypeStruct((batch, frames, width), BF16)
    sem = ("parallel", "arbitrary")
    out_spec = pl.BlockSpec((1, frames, width), lambda b, kt: (b, 0, 0))
    q_spec = pl.BlockSpec((1, N_STEP_HEADS, frames, HEAD_DIM), lambda b, kt: (b, 0, 0, 0))
    new_t_spec = pl.BlockSpec((1, 1, N_STEP_HEADS, HEAD_DIM, frames), lambda b, kt: (0, b, 0, 0, 0))
    forward = lambda kt: kt

    def cache_t_spec(tk, tile_of):
        return pl.BlockSpec((1, 1, N_STEP_HEADS, HEAD_DIM, tk), lambda b, kt: (layer, b, 0, 0, tile_of(kt)))

    def softmax_call(mode, branch, kc, vc, extra_args, extra_specs, name):
        past = kc.shape[-1]
        tk = min(past, 1024)
        assert past % tk == 0
        q, k_new, v_new = new[branch]
        if mode == "A":
            vc_spec = pl.BlockSpec((1, 1, tk, 4, LANES), lambda b, kt: (layer, b, kt, 0, 0))
            vn_spec = pl.BlockSpec((1, 1, frames, 4, LANES), lambda b, kt: (0, b, 0, 0, 0))
            dv = LANES
        else:
            vc_spec, vn_spec, dv = cache_t_spec(tk, forward), new_t_spec, HEAD_DIM
        return pl.pallas_call(
            functools.partial(_softmax_step_kernel, mode=mode, frames=frames, lam_init=lam_init),
            out_shape=out_shape,
            grid=(batch, past // tk),
            in_specs=[q_spec, cache_t_spec(tk, forward), vc_spec, new_t_spec, vn_spec] + extra_specs(tk),
            out_specs=out_spec,
            scratch_shapes=[pltpu.VMEM((N_STEP_HEADS, frames, HEAD_DIM), BF16), pltpu.VMEM((n_rows, 1), F32),
                            pltpu.VMEM((n_rows, 1), F32), pltpu.VMEM((n_rows, dv), F32)],
            compiler_params=_cparams(sem),
            name=name,
        )(q, kc, vc, k_new[None], v_new[None], *extra_args)

    o_a = softmax_call(
        "A", "a", caches["a_k"], caches["a_v"], (lam_vec, subln_w.reshape(1, LANES)),
        lambda tk: [pl.BlockSpec((4, HEAD_DIM), lambda b, kt: (0, 0)),
                    pl.BlockSpec((1, LANES), lambda b, kt: (0, 0))],
        "diff_attention_step")
    o_b = softmax_call(
        "B", "b", caches["b_k"], caches["b_v"], (bias_cache, bias_new),
        lambda tk: [pl.BlockSpec((n_rows, tk), lambda b, kt: (0, kt)),
                    pl.BlockSpec((n_rows, frames), lambda b, kt: (0, 0))],
        "band_attention_step")
    o_c = softmax_call(
        "C", "c", caches["c_k"], caches["c_v"], (cum_q_col, cum_cache_row, cum_new_row),
        lambda tk: [pl.BlockSpec((1, n_rows, 1), lambda b, kt: (b, 0, 0)),
                    pl.BlockSpec((1, N_STEP_HEADS, tk), lambda b, kt: (b, 0, kt)),
                    pl.BlockSpec((1, N_STEP_HEADS, frames), lambda b, kt: (b, 0, 0))],
        "forgetting_attention_step")

    past = caches["d_k"].shape[-1]
    tk = min(past, 512)
    assert past % tk == 0 and tk % SUFFIX_BLOCK == 0
    n_kt = past // tk
    backward = lambda kt: n_kt - 1 - kt
    ones_after = jnp.tril(jnp.ones((SUFFIX_BLOCK, SUFFIX_BLOCK), F32), -1).astype(BF16)
    q, k_new, v_new = new["d"]
    o_d = pl.pallas_call(
        functools.partial(_stick_step_kernel, frames=frames),
        out_shape=out_shape,
        grid=(batch, n_kt),
        in_specs=[q_spec, cache_t_spec(tk, backward), cache_t_spec(tk, backward), new_t_spec, new_t_spec,
                  pl.BlockSpec((SUFFIX_BLOCK, SUFFIX_BLOCK), lambda b, kt: (0, 0))],
        out_specs=out_spec,
        scratch_shapes=[pltpu.VMEM((N_STEP_HEADS, frames, HEAD_DIM), BF16), pltpu.VMEM((n_rows, 1), F32),
                        pltpu.VMEM((n_rows, HEAD_DIM), F32)],
        compiler_params=_cparams(sem),
        name="stick_breaking_attention_step",
    )(q, caches["d_k"], caches["d_v"], k_new[None], v_new[None], ones_after)
    return o_a, o_b, o_c, o_d


def _merge_kernel(x_ref, g0_ref, g1_ref, g2_ref, g3_ref, b0_ref, b1_ref, b2_ref, b3_ref, wb_ref, o_ref,
                  gs_ref, ws_ref):
    @pl.when(pl.program_id(1) == 0)
    def _():
        for n, g_ref in enumerate((g0_ref, g1_ref, g2_ref, g3_ref)):
            gs_ref[n] = g_ref[0].astype(BF16)
        ws_ref[...] = wb_ref[0].astype(BF16)

    x = x_ref[...]
    acc = None
    for n, b_ref in enumerate((b0_ref, b1_ref, b2_ref, b3_ref)):
        term = _sigmoid(_dot_nt(x, gs_ref[n])) * _dot(b_ref[...], ws_ref[n])
        acc = term if acc is None else acc + term
    o_ref[...] = acc.astype(o_ref.dtype)


def gated_merge(xn, w_in_t, gate_row0, w_branch, layer, branches):
    t, d = xn.shape
    wcol = 256
    tm = _row_tile(t, 1024)
    bw = branches[0].shape[1]
    gate_specs = [pl.BlockSpec((pl.Element(1), pl.Element(wcol), pl.Element(d)),
                               functools.partial(lambda c, m, n: (layer, pl.multiple_of(gate_row0 + n * d + c * wcol, 8), 0), n=n))
                  for n in range(N_BRANCH)]
    branch_specs = [pl.BlockSpec((tm, bw), lambda c, m: (m, 0)) for _ in range(N_BRANCH)]
    return pl.pallas_call(
        _merge_kernel,
        out_shape=jax.ShapeDtypeStruct((t, d), BF16),
        grid=(d // wcol, t // tm),
        in_specs=[pl.BlockSpec((tm, d), lambda c, m: (m, 0))] + gate_specs + branch_specs
                 + [pl.BlockSpec((1, N_BRANCH, bw, wcol), lambda c, m: (layer, 0, 0, c))],
        out_specs=pl.BlockSpec((tm, wcol), lambda c, m: (m, c)),
        scratch_shapes=[pltpu.VMEM((N_BRANCH, wcol, d), BF16), pltpu.VMEM((N_BRANCH, bw, wcol), BF16)],
        compiler_params=_cparams(("parallel", "arbitrary")),
        name="gated_merge",
    )(xn, w_in_t, w_in_t, w_in_t, w_in_t, *branches, w_branch)


def _outproj_kernel(a_ref, w_ref, x_ref, o_ref):
    o_ref[...] = x_ref[...] + _dot(a_ref[...], w_ref[0])


def out_projection(merged, w_out, layer, x):
    t, d = x.shape
    tm = _row_tile(t, 512)
    return pl.pallas_call(
        _outproj_kernel,
        out_shape=jax.ShapeDtypeStruct((t, d), F32),
        grid=(t // tm,),
        in_specs=[pl.BlockSpec((tm, d), lambda m: (m, 0)),
                  pl.BlockSpec((1, d, d), lambda m: (layer, 0, 0)),
                  pl.BlockSpec((tm, d), lambda m: (m, 0))],
        out_specs=pl.BlockSpec((tm, d), lambda m: (m, 0)),
        compiler_params=_cparams(("parallel",)),
        name="out_projection",
    )(merged, w_out, x)


def _router_kernel(h_ref, g_ref, w_ref, b_ref, xn_ref, comb_ref):
    h = h_ref[...]
    ms = jnp.mean(h * h, axis=-1, keepdims=True)
    xn = h * lax.rsqrt(ms + RMS_EPS) * g_ref[...]
    xn_ref[...] = xn.astype(xn_ref.dtype)
    logits = jnp.dot(xn, w_ref[...], preferred_element_type=F32, precision=lax.Precision.HIGHEST) + b_ref[...]
    lane = lax.broadcasted_iota(jnp.int32, logits.shape, 1).astype(F32)
    far = float(LANES)

    def first_argmax(vals):
        top = jnp.max(vals, axis=-1, keepdims=True)
        idx = jnp.min(jnp.where(vals == top, lane, far), axis=-1, keepdims=True)
        return top, idx

    group_logits = jnp.where(lane < N_GROUPS, logits, -jnp.inf)
    g_top, g_idx = first_argmax(group_logits)
    g_weight = 1.0 / jnp.sum(jnp.exp(group_logits - g_top), axis=-1, keepdims=True)
    first = N_GROUPS + EXPERTS_PER_GROUP * g_idx
    in_group = (lane >= first) & (lane < first + EXPERTS_PER_GROUP)
    expert_logits = jnp.where(in_group, logits, -jnp.inf)
    top1, idx1 = first_argmax(expert_logits)
    top2, idx2 = first_argmax(jnp.where(lane == idx1, -jnp.inf, expert_logits))
    w1 = 1.0 / (1.0 + jnp.exp(top2 - top1))
    w2 = jnp.exp(top2 - top1) * w1
    comb_ref[...] = g_weight * (jnp.where(lane == idx1, w1, 0.0) + jnp.where(lane == idx2, w2, 0.0))


def router(h, g_ffn, w_router, b_router):
    t, d = h.shape
    tm = _row_tile(t, 256)
    return pl.pallas_call(
        _router_kernel,
        out_shape=(jax.ShapeDtypeStruct((t, d), BF16), jax.ShapeDtypeStruct((t, LANES), F32)),
        grid=(t // tm,),
        in_specs=[pl.BlockSpec((tm, d), lambda m: (m, 0)),
                  pl.BlockSpec((1, d), lambda m: (0, 0)),
                  pl.BlockSpec((d, LANES), lambda m: (0, 0)),
                  pl.BlockSpec((1, LANES), lambda m: (0, 0))],
        out_specs=(pl.BlockSpec((tm, d), lambda m: (m, 0)),
                   pl.BlockSpec((tm, LANES), lambda m: (m, 0))),
        compiler_params=_cparams(("parallel",)),
        name="router",
    )(h, g_ffn.reshape(1, d), w_router, b_router)


EXPERTS_PER_STEP = 2


def _moe_kernel(x_ref, comb_ref, wg_ref, wu_ref, wd_ref, h_ref, o_ref):
    step = pl.program_id(1)

    @pl.when(step == 0)
    def _():
        o_ref[...] = h_ref[...]

    x = x_ref[...]
    comb = comb_ref[...]
    lane = lax.broadcasted_iota(jnp.int32, comb.shape, 1)
    total = None
    for i in range(EXPERTS_PER_STEP):
        e = step * EXPERTS_PER_STEP + i
        weight = jnp.sum(jnp.where(lane == N_GROUPS + e, comb, 0.0), axis=-1, keepdims=True)
        gate = _dot(x, wg_ref[0, i])
        hid = gate * _sigmoid(gate) * _dot(x, wu_ref[0, i]) * weight
        y = _dot(hid.astype(BF16), wd_ref[0, i])
        total = y if total is None else total + y
    o_ref[...] += total


def moe(xn, comb, w_gate, w_up, w_down, layer, h):
    t, d = xn.shape
    _, n_exp, _, f = w_gate.shape
    tm = _row_tile(t, 512)
    return pl.pallas_call(
        _moe_kernel,
        out_shape=jax.ShapeDtypeStruct((t, d), F32),
        grid=(t // tm, n_exp // EXPERTS_PER_STEP),
        in_specs=[pl.BlockSpec((tm, d), lambda m, e: (m, 0)),
                  pl.BlockSpec((tm, LANES), lambda m, e: (m, 0)),
                  pl.BlockSpec((1, EXPERTS_PER_STEP, d, f), lambda m, e: (layer, e, 0, 0)),
                  pl.BlockSpec((1, EXPERTS_PER_STEP, d, f), lambda m, e: (layer, e, 0, 0)),
                  pl.BlockSpec((1, EXPERTS_PER_STEP, f, d), lambda m, e: (layer, e, 0, 0)),
                  pl.BlockSpec((tm, d), lambda m, e: (m, 0))],
        out_specs=pl.BlockSpec((tm, d), lambda m, e: (m, 0)),
        compiler_params=_cparams(("parallel", "arbitrary")),
        name="moe",
    )(xn, comb, w_gate, w_up, w_down, h)


def _rope_tables(pos, width):
    half = HEAD_DIM // 2
    inv_freq = jnp.exp(jnp.arange(half, dtype=F32) * (-2.0 * math.log(ROPE_THETA) / HEAD_DIM))
    ang = pos.astype(F32)[:, None] * inv_freq[None, :]
    reps = width // half
    cos = jnp.tile(jnp.cos(ang), (1, reps))
    sin = jnp.tile(jnp.sin(ang), (1, reps))
    lower = (jnp.arange(width) % HEAD_DIM) < half
    return cos, jnp.where(lower[None, :], -sin, sin)


def _rel_bias_table(rel_bias, qpos, kpos):
    rel = jnp.clip(qpos[:, None] - kpos[None, :], -REL_CLIP, REL_CLIP) + REL_CLIP
    return rel_bias[:, rel].astype(F32)


def _band_bias_table(rel_bias, tile):
    period = 4 * tile + 1
    m = jnp.arange(period)
    diff = jnp.where(m < 3 * tile, m, m - period)
    vec = rel_bias[:, jnp.clip(LEFT_CHUNKS * CHUNK - diff, -REL_CLIP, REL_CLIP) + REL_CLIP].astype(F32)
    flat = jnp.tile(vec, (1, tile))[:, :tile * (period - 1)]
    return flat.reshape(vec.shape[0], tile, period - 1)[:, :, :3 * tile]


def _layer_weights(l, p):
    n_qkv = N_SEG * 512
    w_in = p["w_in"][l]
    n_fg = w_in.shape[1] - n_qkv - N_BRANCH * w_in.shape[0]
    w_f = jnp.pad(p["w_in_t"][l, n_qkv:n_qkv + n_fg], ((0, LANES - n_fg), (0, 0)))
    b_f = jnp.pad(p["b_forget"][l], (0, LANES - n_fg)).reshape(1, LANES)
    w_router = jnp.pad(jnp.concatenate([p["router_group_w"][l], p["router_expert_w"][l]], axis=1),
                       ((0, 0), (0, LANES - N_GROUPS - N_EXPERTS)))
    b_router = jnp.pad(jnp.concatenate([p["router_group_b"][l], p["router_expert_b"][l]]),
                       (0, LANES - N_GROUPS - N_EXPERTS)).reshape(1, LANES)
    return dict(
        w_f=w_f, b_f=b_f, n_fg=n_fg,
        gate_row0=n_qkv + n_fg, w_router=w_router, b_router=b_router,
    )


def _ffn(h, l, p, lw):
    xn2, comb = router(h, p["norm_ffn"][l], lw["w_router"], lw["b_router"])
    return moe(xn2, comb, p["e_gate"], p["e_up"], p["e_down"], l, h)


def _prompt_layer(x, l, p, lw, lam_init, batch, seq, tables, state_bufs):
    t, d = x.shape
    xn = rmsnorm(x, p["norm_mix"][l], BF16)
    qkv_bf, state_bufs = in_projection_prompt(xn, p["w_in_t"], l, tables["cos"], tables["sin"], batch, seq, state_bufs)
    width = qkv_bf.shape[-1]
    n_fg = lw["n_fg"]
    logf = log_forget(xn, lw["w_f"], lw["b_f"])[:, :n_fg].reshape(batch, seq, n_fg)
    cum_row = cumsum_lanes(jnp.swapaxes(logf, 1, 2))
    cum_col = jnp.swapaxes(cum_row.reshape(batch, n_fg // 2, 2, seq), 2, 3)
    branches = prompt_attention(
        qkv_bf.reshape(N_SEG, batch, seq, width), cum_col, tables["band_bias"][l],
        p["diff_lambda"][l], p["diff_subln"][l], lam_init, batch, seq)
    merged = gated_merge(xn, p["w_in_t"], lw["gate_row0"], p["w_branch"], l, [b.reshape(t, width) for b in branches])
    h = out_projection(merged, p["w_out_bf"], l, x)
    h = _ffn(h, l, p, lw)
    return h, state_bufs, logf


def _step_layer(x, l, p, lw, lam_init, batch, frames, caches, tables):
    t, d = x.shape
    xn = rmsnorm(x, p["norm_mix"][l], BF16)
    qkv, _ = in_projection(xn, p["w_in_t"], l, tables["cos"], tables["sin"], frames)
    width = qkv.shape[-1]
    n_fg = lw["n_fg"]
    logf = log_forget(xn, lw["w_f"], lw["b_f"])[:, :n_fg].reshape(batch, frames, n_fg)
    past_f = caches["c_logf"][l]
    past = past_f.shape[1]
    total = past + frames
    padded = -(-total // LANES) * LANES
    series = jnp.concatenate([past_f, logf], axis=1)
    cum = cumsum_lanes(jnp.pad(jnp.swapaxes(series, 1, 2), ((0, 0), (0, 0), (0, padded - total))))
    cum_cache_row = cum[:, :, :past]
    cum_new_row = cum[:, :, past:total]
    cum_q_col = cum_new_row.reshape(batch, n_fg * frames, 1)
    heads = lambda i: qkv[i].reshape(batch, frames, N_STEP_HEADS, HEAD_DIM)
    q_heads = lambda i: jnp.transpose(heads(i), (0, 2, 1, 3))
    seq_minor = lambda i: jnp.transpose(heads(i), (0, 2, 3, 1))
    new = dict(a=(q_heads(0), seq_minor(1), qkv[2].reshape(batch, frames, 4, LANES)),
               b=(q_heads(3), seq_minor(4), seq_minor(5)),
               c=(q_heads(6), seq_minor(7), seq_minor(8)),
               d=(q_heads(9), seq_minor(10), seq_minor(11)))
    branches = step_attention(
        l, new, caches, cum_q_col, cum_cache_row, cum_new_row,
        tables["bias_cache"][l], tables["bias_new"][l], p["diff_lambda"][l], p["diff_subln"][l], lam_init)
    merged = gated_merge(xn, p["w_in_t"], lw["gate_row0"], p["w_branch"], l, [b.reshape(t, width) for b in branches])
    h = out_projection(merged, p["w_out_bf"], l, x)
    h = _ffn(h, l, p, lw)
    seg = lambda i: qkv[i].reshape(batch, frames, width)
    state = (seg(1), seg(2), seg(4), seg(5), seg(7), seg(8), logf, seg(10), seg(11))
    return h, state


def kernel(x_prompt, x_sample, cache_a_k, cache_a_v, cache_b_k, cache_b_v, cache_c_k, cache_c_v, cache_c_logf, cache_d_k, cache_d_v, norm_mix, w_in, b_forget, diff_lambda, diff_subln, rel_bias, w_branch, w_out, norm_ffn, router_group_w, router_group_b, router_expert_w, router_expert_b, expert_w_gate, expert_w_up, expert_w_down, norm_final):
    p = dict(norm_mix=norm_mix, w_in=w_in, w_in_t=jnp.swapaxes(w_in, 1, 2), b_forget=b_forget, diff_lambda=diff_lambda, diff_subln=diff_subln,
             w_branch=w_branch, w_out=w_out, norm_ffn=norm_ffn, router_group_w=router_group_w,
             router_group_b=router_group_b, router_expert_w=router_expert_w, router_expert_b=router_expert_b,
             w_out_bf=w_out.astype(BF16), e_gate=expert_w_gate.astype(BF16), e_up=expert_w_up.astype(BF16),
             e_down=expert_w_down.astype(BF16))
    depth = w_in.shape[0]
    bp, sp, d = x_prompt.shape
    bs, fs, _ = x_sample.shape
    past = cache_a_k.shape[2]
    width = 512
    n_heads = width // HEAD_DIM

    def seq_minor(c):
        c = jnp.moveaxis(c, 2, -1)
        return c.reshape(c.shape[0], c.shape[1], N_STEP_HEADS, HEAD_DIM, c.shape[-1])

    caches = dict(a_k=seq_minor(cache_a_k), a_v=cache_a_v, b_k=seq_minor(cache_b_k), b_v=seq_minor(cache_b_v),
                  c_k=seq_minor(cache_c_k), c_v=seq_minor(cache_c_v), c_logf=cache_c_logf,
                  d_k=seq_minor(cache_d_k), d_v=seq_minor(cache_d_v))

    pos_p = jnp.arange(sp, dtype=jnp.int32)
    pos_s = past + jnp.arange(fs, dtype=jnp.int32)
    cos_p, sin_p = _rope_tables(pos_p, width)
    cos_s, sin_s = _rope_tables(pos_s, width)
    tile = 256
    band_bias = jnp.stack([_band_bias_table(rel_bias[l], tile).reshape(n_heads // 2, 2, tile, 3 * tile)
                           for l in range(depth)])
    wb = cache_b_k.shape[2]
    kpos_b = jnp.arange(past - wb, past + fs, dtype=jnp.int32)
    step_bias = jnp.stack([_rel_bias_table(rel_bias[l], pos_s, kpos_b).reshape(n_heads * fs, wb + fs)
                           for l in range(depth)])
    tables_p = dict(cos=cos_p, sin=sin_p, band_bias=band_bias)
    tables_s = dict(cos=cos_s, sin=sin_s, bias_cache=step_bias[:, :, :wb], bias_new=step_bias[:, :, wb:])

    hp = x_prompt.reshape(bp * sp, d)
    hs = x_sample.reshape(bs * fs, d)
    bufs_p, logf_p, st_s = prompt_state_buffers(depth, bp, sp), [], []
    for l in range(depth):
        lam_init = 0.8 - 0.6 * math.exp(-0.3 * l)
        lw = _layer_weights(l, p)
        hp, bufs_p, logf = _prompt_layer(hp, l, p, lw, lam_init, bp, sp, tables_p, bufs_p)
        hs, state_s = _step_layer(hs, l, p, lw, lam_init, bs, fs, caches, tables_s)
        logf_p.append(logf)
        st_s.append(state_s)
    y_prompt = rmsnorm(hp, norm_final, F32).reshape(bp, sp, d)
    y_sample = rmsnorm(hs, norm_final, F32).reshape(bs, fs, d)

    h_a = width // (2 * HEAD_DIM)
    state_shapes = lambda b, s, sb: (
        (b, s, h_a, 2, HEAD_DIM), (b, s, h_a, 2 * HEAD_DIM), (b, sb, n_heads, HEAD_DIM), (b, sb, n_heads, HEAD_DIM),
        (b, s, n_heads, HEAD_DIM), (b, s, n_heads, HEAD_DIM), (b, s, n_heads), (b, s, n_heads, HEAD_DIM),
        (b, s, n_heads, HEAD_DIM))

    def stack(states, shapes):
        return tuple(jnp.stack([st[i].reshape(shapes[i]) for st in states], axis=0) for i in range(len(shapes)))

    token_major = lambda buf, dims: jnp.moveaxis(buf.reshape(buf.shape[:2] + dims + buf.shape[-1:]), -1, 2)
    a_k, a_v, b_k, b_v, c_k, c_v, d_k, d_v = bufs_p
    per_head = (n_heads, HEAD_DIM)
    out_p = (token_major(a_k, (h_a, 2, HEAD_DIM)), a_v, token_major(b_k, per_head), token_major(b_v, per_head),
             token_major(c_k, per_head), token_major(c_v, per_head), jnp.stack(logf_p, axis=0),
             token_major(d_k, per_head), token_major(d_v, per_head))
    out_s = stack(st_s, state_shapes(bs, fs, fs))
    return (y_prompt, y_sample) + out_p + out_s
```

```python
import functools
import math

import jax
import jax.numpy as jnp
from jax import lax
from jax.experimental import pallas as pl
from jax.experimental.pallas import tpu as pltpu

F32 = jnp.float32
BF16 = jnp.bfloat16

CHUNK = 64
HEAD_DIM = 64
LEFT_CHUNKS = 8
REL_CLIP = 128
ROPE_THETA = 10000.0
N_BRANCH = 4
N_GROUPS = 4
EXPERTS_PER_GROUP = 4
N_EXPERTS = N_GROUPS * EXPERTS_PER_GROUP
RMS_EPS = 1e-6
NEG_INF = -1e30
Q_SCALE = HEAD_DIM ** -0.5
N_SEG = 12

LANES = 128
VMEM_LIMIT = 56 * 1024 * 1024

NT_DIMS = (((1,), (1,)), ((), ()))


def _cparams(sem):
    return pltpu.CompilerParams(dimension_semantics=sem, vmem_limit_bytes=VMEM_LIMIT)


def _row_tile(t, pref):
    return pref if t % pref == 0 else t


def _sigmoid(z):
    return 1.0 / (1.0 + jnp.exp(-z))


def _dot(a, b):
    return jnp.dot(a, b, preferred_element_type=F32)


def _dot_nt(a, b):
    return lax.dot_general(a, b, NT_DIMS, preferred_element_type=F32)


def _rms_kernel(x_ref, g_ref, o_ref):
    x = x_ref[...]
    ms = jnp.mean(x * x, axis=-1, keepdims=True)
    o_ref[...] = (x * lax.rsqrt(ms + RMS_EPS) * g_ref[...]).astype(o_ref.dtype)


def rmsnorm(x, g, out_dtype):
    t, d = x.shape
    tm = _row_tile(t, 512)
    return pl.pallas_call(
        _rms_kernel,
        out_shape=jax.ShapeDtypeStruct((t, d), out_dtype),
        grid=(t // tm,),
        in_specs=[pl.BlockSpec((tm, d), lambda m: (m, 0)),
                  pl.BlockSpec((1, d), lambda m: (0, 0))],
        out_specs=pl.BlockSpec((tm, d), lambda m: (m, 0)),
        compiler_params=_cparams(("parallel",)),
        name="rmsnorm",
    )(x, g.reshape(1, d))


def _inproj_kernel(x_ref, w_ref, cos_ref, sin_ref, o_ref, ob_ref, wb_ref):
    n = pl.program_id(0)

    @pl.when(pl.program_id(1) == 0)
    def _():
        wb_ref[...] = w_ref[0].astype(BF16)

    acc = _dot_nt(x_ref[...], wb_ref[...])
    q_scale = jnp.where(n % 3 == 0, Q_SCALE, 1.0)

    @pl.when(n < 2)
    def _():
        width = acc.shape[1]
        half = HEAD_DIM // 2
        lane = lax.broadcasted_iota(jnp.int32, acc.shape, 1)
        lower = (lane % HEAD_DIM) < half
        partner = jnp.where(lower, pltpu.roll(acc, width - half, 1), pltpu.roll(acc, half, 1))
        roped = acc * cos_ref[...] + partner * sin_ref[...]
        o_ref[0] = roped
        ob_ref[0] = (roped * q_scale).astype(ob_ref.dtype)

    @pl.when(n >= 2)
    def _():
        o_ref[0] = acc
        ob_ref[0] = (acc * q_scale).astype(ob_ref.dtype)


def in_projection(xn, w_in_t, layer, cos_tab, sin_tab, seq):
    t, d = xn.shape
    w = 512
    tm = _row_tile(t, 512)
    if seq % tm == 0:
        n_tab = seq // tm
    else:
        assert tm % seq == 0
        cos_tab = jnp.tile(cos_tab, (tm // seq, 1))
        sin_tab = jnp.tile(sin_tab, (tm // seq, 1))
        n_tab = 1
    return pl.pallas_call(
        _inproj_kernel,
        out_shape=(jax.ShapeDtypeStruct((N_SEG, t, w), F32), jax.ShapeDtypeStruct((N_SEG, t, w), BF16)),
        grid=(N_SEG, t // tm),
        in_specs=[pl.BlockSpec((tm, d), lambda n, m: (m, 0)),
                  pl.BlockSpec((1, w, d), lambda n, m: (layer, n, 0)),
                  pl.BlockSpec((tm, w), lambda n, m: (m % n_tab, 0)),
                  pl.BlockSpec((tm, w), lambda n, m: (m % n_tab, 0))],
        out_specs=(pl.BlockSpec((1, tm, w), lambda n, m: (n, m, 0)),
                   pl.BlockSpec((1, tm, w), lambda n, m: (n, m, 0))),
        scratch_shapes=[pltpu.VMEM((w, d), BF16)],
        compiler_params=_cparams(("parallel", "arbitrary")),
        name="in_projection",
    )(xn, w_in_t, cos_tab, sin_tab)


STATE_SEGS = (1, 2, 4, 5, 7, 8, 10, 11)
VALUE_A_SEG = 2
BAND_SEGS = (4, 5)


def _inproj_prompt_kernel(*refs, n_alias, tiles_per_seq):
    x_ref, w_ref, cos_ref, sin_ref = refs[:4]
    ob_ref = refs[4 + n_alias]
    st_refs = refs[5 + n_alias:5 + n_alias + len(STATE_SEGS)]
    wb_ref = refs[-1]
    n, m = pl.program_id(0), pl.program_id(1)

    @pl.when(m == 0)
    def _():
        wb_ref[...] = w_ref[0].astype(BF16)

    acc = _dot_nt(x_ref[...], wb_ref[...])
    q_scale = jnp.where(n % 3 == 0, Q_SCALE, 1.0)

    def emit(val):
        ob_ref[0] = (val * q_scale).astype(ob_ref.dtype)
        for seg, st_ref in zip(STATE_SEGS, st_refs):
            if seg == VALUE_A_SEG:
                @pl.when(n == seg)
                def _(st_ref=st_ref):
                    for h in range(st_ref.shape[3]):
                        st_ref[0, 0, :, h, :] = val[:, h * LANES:(h + 1) * LANES]
            elif seg in BAND_SEGS:
                @pl.when(jnp.logical_and(n == seg, m % tiles_per_seq == tiles_per_seq - 1))
                def _(st_ref=st_ref):
                    st_ref[0, 0] = val.T
            else:
                @pl.when(n == seg)
                def _(st_ref=st_ref):
                    st_ref[0, 0] = val.T

    @pl.when(n < 2)
    def _():
        width = acc.shape[1]
        half = HEAD_DIM // 2
        lane = lax.broadcasted_iota(jnp.int32, acc.shape, 1)
        lower = (lane % HEAD_DIM) < half
        partner = jnp.where(lower, pltpu.roll(acc, width - half, 1), pltpu.roll(acc, half, 1))
        emit(acc * cos_ref[...] + partner * sin_ref[...])

    @pl.when(n >= 2)
    def _():
        emit(acc)


def prompt_state_buffers(depth, batch, seq, w=512):
    band = LEFT_CHUNKS * CHUNK
    shapes = []
    for seg in STATE_SEGS:
        if seg == VALUE_A_SEG:
            shapes.append((depth, batch, seq, w // LANES, LANES))
        elif seg in BAND_SEGS:
            shapes.append((depth, batch, w, band))
        else:
            shapes.append((depth, batch, w, seq))
    return tuple(jnp.zeros(s, F32) for s in shapes)


def in_projection_prompt(xn, w_in_t, layer, cos_tab, sin_tab, batch, seq, state_bufs):
    t, d = xn.shape
    w = 512
    tm = 512
    band = LEFT_CHUNKS * CHUNK
    assert seq % tm == 0 and band == tm and t == batch * seq
    per_seq = seq // tm
    n_tiles = t // tm

    def parked(seg, n, m):
        return jnp.where(n < seg, 0, jnp.where(n > seg, n_tiles - 1, m))

    specs = []
    for seg in STATE_SEGS:
        if seg == VALUE_A_SEG:
            specs.append(pl.BlockSpec((1, 1, tm, w // LANES, LANES), functools.partial(
                lambda n, m, seg: (layer, parked(seg, n, m) // per_seq, parked(seg, n, m) % per_seq, 0, 0), seg=seg)))
        elif seg in BAND_SEGS:
            specs.append(pl.BlockSpec((1, 1, w, band), functools.partial(
                lambda n, m, seg: (layer, parked(seg, n, m) // per_seq, 0, 0), seg=seg)))
        else:
            specs.append(pl.BlockSpec((1, 1, w, tm), functools.partial(
                lambda n, m, seg: (layer, parked(seg, n, m) // per_seq, 0, parked(seg, n, m) % per_seq), seg=seg)))
    alias_in = list(state_bufs)
    shapes = [jax.ShapeDtypeStruct(b.shape, b.dtype) for b in alias_in]
    n_fixed = 4
    out = pl.pallas_call(
        functools.partial(_inproj_prompt_kernel, n_alias=len(alias_in), tiles_per_seq=per_seq),
        out_shape=[jax.ShapeDtypeStruct((N_SEG, t, w), BF16)] + shapes,
        grid=(N_SEG, n_tiles),
        in_specs=[pl.BlockSpec((tm, d), lambda n, m: (m, 0)),
                  pl.BlockSpec((1, w, d), lambda n, m: (layer, n, 0)),
                  pl.BlockSpec((tm, w), lambda n, m: (m % per_seq, 0)),
                  pl.BlockSpec((tm, w), lambda n, m: (m % per_seq, 0))]
                 + [pl.BlockSpec(memory_space=pl.ANY) for _ in alias_in],
        out_specs=[pl.BlockSpec((1, tm, w), lambda n, m: (n, m, 0))] + specs,
        input_output_aliases={n_fixed + i: 1 + i for i in range(len(alias_in))},
        scratch_shapes=[pltpu.VMEM((w, d), BF16)],
        compiler_params=_cparams(("arbitrary", "arbitrary")),
        name="in_projection_prompt",
    )(xn, w_in_t, cos_tab, sin_tab, *alias_in)
    return out[0], tuple(out[1:])


def _logf_kernel(x_ref, w_ref, b_ref, o_ref):
    z = _dot_nt(x_ref[...], w_ref[...].astype(BF16)) + b_ref[...]
    o_ref[...] = jnp.minimum(z, 0.0) - jnp.log1p(jnp.exp(-jnp.abs(z)))


def log_forget(xn, w_f, b_f):
    t, d = xn.shape
    tm = _row_tile(t, 512)
    return pl.pallas_call(
        _logf_kernel,
        out_shape=jax.ShapeDtypeStruct((t, LANES), F32),
        grid=(t // tm,),
        in_specs=[pl.BlockSpec((tm, d), lambda m: (m, 0)),
                  pl.BlockSpec((LANES, d), lambda m: (0, 0)),
                  pl.BlockSpec((1, LANES), lambda m: (0, 0))],
        out_specs=pl.BlockSpec((tm, LANES), lambda m: (m, 0)),
        compiler_params=_cparams(("parallel",)),
        name="log_forget",
    )(xn, w_f, b_f)


def _cumsum_kernel(x_ref, o_ref):
    x = x_ref[0]
    length = x.shape[1]
    lane = lax.broadcasted_iota(jnp.int32, x.shape, 1)
    shift = 1
    while shift < length:
        x = x + jnp.where(lane >= shift, pltpu.roll(x, shift, 1), 0.0)
        shift *= 2
    o_ref[0] = x


def cumsum_lanes(x):
    b, h, length = x.shape
    return pl.pallas_call(
        _cumsum_kernel,
        out_shape=jax.ShapeDtypeStruct(x.shape, F32),
        grid=(b,),
        in_specs=[pl.BlockSpec((1, h, length), lambda i: (i, 0, 0))],
        out_specs=pl.BlockSpec((1, h, length), lambda i: (i, 0, 0)),
        compiler_params=_cparams(("parallel",)),
        name="cumsum",
    )(x)


def _diff_lambda(lam_ref, lam_init):
    lv = lam_ref[...]
    a = jnp.sum(lv[0:1] * lv[1:2], axis=1, keepdims=True)
    b = jnp.sum(lv[2:3] * lv[3:4], axis=1, keepdims=True)
    return jnp.exp(a) - jnp.exp(b) + lam_init


def _subln(o, w, lam_init):
    ms = jnp.mean(o * o, axis=-1, keepdims=True)
    return o * lax.rsqrt(ms + RMS_EPS) * w * (1.0 - lam_init)


def _head_lanes(shape, j):
    lane = lax.broadcasted_iota(jnp.int32, shape, 1)
    return (lane >= j * HEAD_DIM) & (lane < (j + 1) * HEAD_DIM)


def _split3(x):
    hi = x.astype(BF16).astype(F32)
    mid = (x - hi).astype(BF16).astype(F32)
    lo = (x - hi - mid).astype(BF16).astype(F32)
    return hi, mid, lo


def _with_bias_columns(x, j, cols):
    lane = lax.broadcasted_iota(jnp.int32, x.shape, 1)
    out = jnp.where(_head_lanes(x.shape, j), x.astype(F32), 0.0)
    base = (1 - j) * HEAD_DIM
    for i, c in enumerate(cols):
        out = jnp.where(lane == base + i, c, out)
    return out.astype(BF16)


def _softmax_tile_update(state, s, v):
    m, l, acc = state
    m_new = jnp.maximum(m, jnp.max(s, axis=-1, keepdims=True))
    alpha = jnp.exp(m - m_new)
    p = jnp.exp(s - m_new)
    return (m_new, alpha * l + jnp.sum(p, axis=-1, keepdims=True), alpha * acc + _dot(p.astype(BF16), v))


def _causal_softmax_sweep(qs, k_rows, v_rows, diag_mask, qi, tile):
    init = tuple((jnp.full((tile, 1), NEG_INF, F32), jnp.zeros((tile, 1), F32), jnp.zeros((tile, LANES), F32))
                 for _ in qs)

    def update(states, start, size, mask):
        start = pl.multiple_of(start, tile)
        v = v_rows(start, size)
        out = []
        for j, q in enumerate(qs):
            s = _dot_nt(q, k_rows(j, start, size))
            if mask is not None:
                s = jnp.where(mask, s, NEG_INF)
            out.append(_softmax_tile_update(states[j], s, v))
        return tuple(out)

    pairs = qi // 2
    states = lax.fori_loop(0, pairs, lambda i, st: update(st, i * 2 * tile, 2 * tile, None), init)
    states = lax.fori_loop(0, qi - 2 * pairs, lambda i, st: update(st, pairs * 2 * tile, tile, None), states)
    return update(states, qi * tile, tile, diag_mask)


def _a_prompt_kernel(q_ref, k_ref, v_ref, lam_ref, w_ref, o_ref, *, tile, lam_init):
    qi = pl.program_id(2)
    q = q_ref[0, 0]
    qs = [jnp.where(_head_lanes(q.shape, j), q, jnp.zeros_like(q)) for j in range(2)]
    row = lax.broadcasted_iota(jnp.int32, (tile, tile), 0)
    col = lax.broadcasted_iota(jnp.int32, (tile, tile), 1)
    diag_mask = (col // CHUNK) <= (row // CHUNK)
    k_rows = lambda j, start, size: k_ref[0, 0, pl.ds(start, size), :]
    v_rows = lambda start, size: v_ref[0, 0, pl.ds(start, size), :]
    (_, l1, acc1), (_, l2, acc2) = _causal_softmax_sweep(qs, k_rows, v_rows, diag_mask, qi, tile)
    lam = _diff_lambda(lam_ref, lam_init)
    o = acc1 / l1 - lam * (acc2 / l2)
    o_ref[0] = _subln(o, w_ref[...], lam_init).astype(o_ref.dtype)


def _c_prompt_kernel(q_ref, k_ref, v_ref, cq_ref, ck_ref, o_ref, kb_ref, *, tile):
    qi = pl.program_id(2)
    one = jnp.float32(1.0)

    @pl.when(qi == 0)
    def _():
        k = k_ref[0, 0]
        ck = ck_ref[0, 0]
        for j in range(2):
            hi, mid, lo = _split3(-ck[:, j:j + 1])
            kb_ref[j] = _with_bias_columns(k, j, [one, one, one, hi, mid, lo])

    q = q_ref[0, 0]
    cq = cq_ref[0, 0]
    qs = []
    for j in range(2):
        hi, mid, lo = _split3(cq[:, j:j + 1])
        qs.append(_with_bias_columns(q, j, [hi, mid, lo, one, one, one]))
    row = lax.broadcasted_iota(jnp.int32, (tile, tile), 0)
    col = lax.broadcasted_iota(jnp.int32, (tile, tile), 1)
    k_rows = lambda j, start, size: kb_ref[j, pl.ds(start, size), :]
    v_rows = lambda start, size: v_ref[0, 0, pl.ds(start, size), :]
    (_, l1, acc1), (_, l2, acc2) = _causal_softmax_sweep(qs, k_rows, v_rows, col <= row, qi, tile)
    o_ref[0] = jnp.where(_head_lanes(acc1.shape, 0), acc1 / l1, acc2 / l2).astype(o_ref.dtype)


SUFFIX_BLOCK = 256


def _strict_upper_ones(n):
    r = lax.broadcasted_iota(jnp.int32, (n, n), 0)
    c = lax.broadcasted_iota(jnp.int32, (n, n), 1)
    return jnp.where(r > c, 1.0, 0.0).astype(BF16)


def _suffix_sums(lm, ones_after, two_pass):
    hi = lm.astype(BF16)
    out = _dot(hi, ones_after)
    if two_pass:
        out = out + _dot((lm - hi.astype(F32)).astype(BF16), ones_after)
    return out


def _stick_weights(z, mask, run, ones_after, two_pass=True):
    sp = jnp.maximum(z, 0.0) + jnp.log(1.0 + jnp.exp(-jnp.abs(z)))
    lm = -sp if mask is None else jnp.where(mask, -sp, 0.0)
    block = ones_after.shape[0]
    n_block = z.shape[1] // block
    later = [None] * n_block
    for c in reversed(range(n_block)):
        part = lm[:, c * block:(c + 1) * block]
        later[c] = _suffix_sums(part, ones_after, two_pass) + run
        run = run + jnp.sum(part, axis=-1, keepdims=True)
    later = later[0] if n_block == 1 else jnp.concatenate(later, axis=1)
    a = jnp.exp((z - sp) + later)
    if mask is not None:
        a = jnp.where(mask, a, 0.0)
    return a, run


def _d_prompt_kernel(q_ref, k_ref, v_ref, o_ref, *, tile):
    qi = pl.program_id(2)
    q = q_ref[0, 0]
    qs = [jnp.where(_head_lanes(q.shape, j), q, jnp.zeros_like(q)) for j in range(2)]
    ones_after = _strict_upper_ones(SUFFIX_BLOCK)
    row = lax.broadcasted_iota(jnp.int32, (tile, tile), 0)
    col = lax.broadcasted_iota(jnp.int32, (tile, tile), 1)

    def update(states, start, size, mask):
        start = pl.multiple_of(start, tile)
        k = k_ref[0, 0, pl.ds(start, size), :]
        v = v_ref[0, 0, pl.ds(start, size), :]
        out = []
        for j, q_j in enumerate(qs):
            run, acc = states[j]
            a, run = _stick_weights(_dot_nt(q_j, k), mask, run, ones_after, two_pass=False)
            out.append((run, acc + _dot(a.astype(BF16), v)))
        return tuple(out)

    init = tuple((jnp.zeros((tile, 1), F32), jnp.zeros((tile, LANES), F32)) for _ in qs)
    states = update(init, qi * tile, tile, col < row)
    pairs = qi // 2
    states = lax.fori_loop(0, pairs, lambda i, st: update(st, (qi - 2 - 2 * i) * tile, 2 * tile, None), states)
    states = lax.fori_loop(0, qi - 2 * pairs, lambda i, st: update(st, 0, tile, None), states)
    o_ref[0] = jnp.where(_head_lanes((tile, LANES), 0), states[0][1], states[1][1]).astype(o_ref.dtype)


def _b_prompt_kernel(q_ref, k0_ref, k1_ref, k2_ref, v0_ref, v1_ref, v2_ref, bias_ref, o_ref, *, tile):
    qi = pl.program_id(2)
    q = q_ref[0, 0]
    row_chunk = lax.broadcasted_iota(jnp.int32, (tile, tile), 0) // CHUNK
    col_chunk = lax.broadcasted_iota(jnp.int32, (tile, tile), 1) // CHUNK
    back = 2
    ks = [r[0, 0] for r in (k0_ref, k1_ref, k2_ref)]
    vs = [r[0, 0] for r in (v0_ref, v1_ref, v2_ref)]
    valids = []
    for c in range(3):
        dist = row_chunk - col_chunk + (back - c) * (tile // CHUNK)
        valids.append((dist >= 0) & (dist <= LEFT_CHUNKS) & (qi - back + c >= 0))
    outs = []
    for j in range(2):
        q_j = jnp.where(_head_lanes(q.shape, j), q, jnp.zeros_like(q))
        scores = []
        for c in range(3):
            s = _dot_nt(q_j, ks[c]) + bias_ref[0, j, :, c * tile:(c + 1) * tile]
            scores.append(jnp.where(valids[c], s, NEG_INF))
        m = jnp.maximum(jnp.maximum(jnp.max(scores[0], axis=-1, keepdims=True),
                                    jnp.max(scores[1], axis=-1, keepdims=True)),
                        jnp.max(scores[2], axis=-1, keepdims=True))
        ps = [jnp.exp(s - m) for s in scores]
        denom = sum(jnp.sum(p, axis=-1, keepdims=True) for p in ps)
        o = sum(_dot(p.astype(BF16), v) for p, v in zip(ps, vs))
        outs.append(o / denom)
    o_ref[0] = jnp.where(_head_lanes(outs[0].shape, 0), outs[0], outs[1]).astype(o_ref.dtype)


def prompt_attention(qkv, cum_col, band_bias, lam_vec, subln_w, lam_init, batch, seq):
    width = qkv.shape[-1]
    n_pair = width // LANES
    out_shape = jax.ShapeDtypeStruct((batch, seq, width), BF16)
    tile = 512
    assert seq % tile == 0 and tile % CHUNK == 0
    sem = ("parallel", "parallel", "arbitrary")

    q_spec = lambda seg: pl.BlockSpec((1, 1, tile, LANES), lambda b, h, qi: (seg, b, qi, h))
    full_spec = lambda seg: pl.BlockSpec((1, 1, seq, LANES), lambda b, h, qi: (seg, b, 0, h))
    out_spec = pl.BlockSpec((1, tile, LANES), lambda b, h, qi: (b, qi, h))
    grid = (batch, n_pair, seq // tile)

    o_a = pl.pallas_call(
        functools.partial(_a_prompt_kernel, tile=tile, lam_init=lam_init),
        out_shape=out_shape, grid=grid,
        in_specs=[q_spec(0), full_spec(1), full_spec(2),
                  pl.BlockSpec((4, HEAD_DIM), lambda b, h, qi: (0, 0)),
                  pl.BlockSpec((1, LANES), lambda b, h, qi: (0, 0))],
        out_specs=out_spec,
        compiler_params=_cparams(sem),
        name="diff_attention",
    )(qkv, qkv, qkv, lam_vec, subln_w.reshape(1, LANES))

    o_c = pl.pallas_call(
        functools.partial(_c_prompt_kernel, tile=tile),
        out_shape=out_shape, grid=grid,
        in_specs=[q_spec(6), full_spec(7), full_spec(8),
                  pl.BlockSpec((1, 1, tile, 2), lambda b, h, qi: (b, h, qi, 0)),
                  pl.BlockSpec((1, 1, seq, 2), lambda b, h, qi: (b, h, 0, 0))],
        out_specs=out_spec,
        scratch_shapes=[pltpu.VMEM((2, seq, LANES), BF16)],
        compiler_params=_cparams(sem),
        name="forgetting_attention",
    )(qkv, qkv, qkv, cum_col, cum_col)

    o_d = pl.pallas_call(
        functools.partial(_d_prompt_kernel, tile=tile),
        out_shape=out_shape, grid=grid,
        in_specs=[q_spec(9), full_spec(10), full_spec(11)],
        out_specs=out_spec,
        compiler_params=_cparams(sem),
        name="stick_breaking_attention",
    )(qkv, qkv, qkv)

    band_tile = 256
    assert seq % band_tile == 0 and band_tile * 2 == LEFT_CHUNKS * CHUNK

    def band_spec(seg, back):
        return pl.BlockSpec((1, 1, band_tile, LANES), lambda b, h, qi: (seg, b, jnp.maximum(qi - back, 0), h))

    o_b = pl.pallas_call(
        functools.partial(_b_prompt_kernel, tile=band_tile),
        out_shape=out_shape,
        grid=(batch, n_pair, seq // band_tile),
        in_specs=[band_spec(3, 0),
                  band_spec(4, 2), band_spec(4, 1), band_spec(4, 0),
                  band_spec(5, 2), band_spec(5, 1), band_spec(5, 0),
                  pl.BlockSpec((1, 2, band_tile, 3 * band_tile), lambda b, h, qi: (h, 0, 0, 0))],
        out_specs=pl.BlockSpec((1, band_tile, LANES), lambda b, h, qi: (b, qi, h)),
        compiler_params=_cparams(("parallel", "parallel", "parallel")),
        name="band_attention",
    )(qkv, qkv, qkv, qkv, qkv, qkv, qkv, band_bias)
    return o_a, o_b, o_c, o_d


N_STEP_HEADS = 8


def _step_scores(qs_ref, kt_ref):
    rows = [_dot(qs_ref[h], kt_ref[0, 0, h].astype(BF16)) for h in range(N_STEP_HEADS)]
    return jnp.concatenate(rows, axis=0)


def _step_pv(p, v_ref, frames, v_token_major):
    pb = p.astype(BF16)
    if v_token_major:
        rows = [_dot(pb[2 * g * frames:(2 * g + 2) * frames], v_ref[0, 0, :, g, :].astype(BF16))
                for g in range(N_STEP_HEADS // 2)]
    else:
        rows = [_dot_nt(pb[h * frames:(h + 1) * frames], v_ref[0, 0, h].astype(BF16))
                for h in range(N_STEP_HEADS)]
    return jnp.concatenate(rows, axis=0)


def _store_scaled_queries(q_ref, qs_ref):
    for h in range(N_STEP_HEADS):
        qs_ref[h] = (q_ref[0, h] * Q_SCALE).astype(BF16)


def _rows_per_head(x, frames):
    return jnp.concatenate([jnp.broadcast_to(x[h:h + 1, :], (frames, x.shape[1]))
                            for h in range(x.shape[0])], axis=0)


def _new_rows_mask(frames, n_rows, strict):
    frame = lax.broadcasted_iota(jnp.int32, (n_rows, frames), 0) % frames
    key = lax.broadcasted_iota(jnp.int32, (n_rows, frames), 1)
    return key < frame if strict else key <= frame


def _heads_to_lanes(acc, frames):
    return jnp.concatenate([acc[h * frames:(h + 1) * frames] for h in range(N_STEP_HEADS)], axis=1)


def _softmax_step_kernel(*refs, mode, frames, lam_init):
    if mode == "A":
        (q_ref, kc_ref, vc_ref, kn_ref, vn_ref, lam_ref, w_ref,
         o_ref, qs_ref, m_ref, l_ref, acc_ref) = refs
    elif mode == "B":
        (q_ref, kc_ref, vc_ref, kn_ref, vn_ref, bias_c_ref, bias_n_ref,
         o_ref, qs_ref, m_ref, l_ref, acc_ref) = refs
    else:
        (q_ref, kc_ref, vc_ref, kn_ref, vn_ref, cq_ref, ckc_ref, ckn_ref,
         o_ref, qs_ref, m_ref, l_ref, acc_ref) = refs
    n_rows = N_STEP_HEADS * frames
    token_major_v = mode == "A"
    kt = pl.program_id(1)

    @pl.when(kt == 0)
    def _():
        _store_scaled_queries(q_ref, qs_ref)
        m_ref[...] = jnp.full(m_ref.shape, NEG_INF, F32)
        l_ref[...] = jnp.zeros(l_ref.shape, F32)
        acc_ref[...] = jnp.zeros(acc_ref.shape, F32)

    def softmax_tile(s, v_ref):
        m_prev = m_ref[...]
        m_new = jnp.maximum(m_prev, jnp.max(s, axis=-1, keepdims=True))
        alpha = jnp.exp(m_prev - m_new)
        p = jnp.exp(s - m_new)
        l_ref[...] = alpha * l_ref[...] + jnp.sum(p, axis=-1, keepdims=True)
        acc_ref[...] = alpha * acc_ref[...] + _step_pv(p, v_ref, frames, token_major_v)
        m_ref[...] = m_new

    s = _step_scores(qs_ref, kc_ref)
    if mode == "B":
        s = s + bias_c_ref[...]
    elif mode == "C":
        s = s + cq_ref[0] - _rows_per_head(ckc_ref[0], frames)
    softmax_tile(s, vc_ref)

    @pl.when(kt == pl.num_programs(1) - 1)
    def _():
        s = _step_scores(qs_ref, kn_ref)
        if mode == "B":
            s = s + bias_n_ref[...]
        elif mode == "C":
            s = s + cq_ref[0] - _rows_per_head(ckn_ref[0], frames)
            s = jnp.where(_new_rows_mask(frames, n_rows, strict=False), s, NEG_INF)
        softmax_tile(s, vn_ref)
        o = acc_ref[...] / l_ref[...]
        if mode == "A":
            lam = _diff_lambda(lam_ref, lam_init)
            pieces = []
            for h in range(N_STEP_HEADS // 2):
                o1 = o[(2 * h) * frames:(2 * h + 1) * frames]
                o2 = o[(2 * h + 1) * frames:(2 * h + 2) * frames]
                pieces.append(_subln(o1 - lam * o2, w_ref[...], lam_init))
            o_ref[0] = jnp.concatenate(pieces, axis=1).astype(o_ref.dtype)
        else:
            o_ref[0] = _heads_to_lanes(o, frames).astype(o_ref.dtype)


def _stick_step_kernel(q_ref, kc_ref, vc_ref, kn_ref, vn_ref, ones_ref, o_ref,
                       qs_ref, run_ref, acc_ref, *, frames):
    n_rows = N_STEP_HEADS * frames
    kt = pl.program_id(1)

    @pl.when(kt == 0)
    def _():
        _store_scaled_queries(q_ref, qs_ref)
        z = _step_scores(qs_ref, kn_ref)
        mask = _new_rows_mask(frames, n_rows, strict=True)
        a, run = _stick_weights(z, mask, jnp.zeros((n_rows, 1), F32), _strict_upper_ones(frames))
        acc_ref[...] = _step_pv(a, vn_ref, frames, False)
        run_ref[...] = run

    z = _step_scores(qs_ref, kc_ref)
    a, run = _stick_weights(z, None, run_ref[...], ones_ref[...])
    acc_ref[...] += _step_pv(a, vc_ref, frames, False)
    run_ref[...] = run

    @pl.when(kt == pl.num_programs(1) - 1)
    def _():
        o_ref[0] = _heads_to_lanes(acc_ref[...], frames).astype(o_ref.dtype)


def step_attention(layer, new, caches, cum_q_col, cum_cache_row, cum_new_row, bias_cache, bias_new,
                   lam_vec, subln_w, lam_init):
    batch, _, frames, _ = new["a"][0].shape
    width = N_STEP_HEADS * HEAD_DIM
    n_rows = N_STEP_HEADS * frames
    out_shape = jax.ShapeDtypeStruct((batch, frames, width), BF16)
    sem = ("parallel", "arbitrary")
    out_spec = pl.BlockSpec((1, frames, width), lambda b, kt: (b, 0, 0))
    q_spec = pl.BlockSpec((1, N_STEP_HEADS, frames, HEAD_DIM), lambda b, kt: (b, 0, 0, 0))
    new_t_spec = pl.BlockSpec((1, 1, N_STEP_HEADS, HEAD_DIM, frames), lambda b, kt: (0, b, 0, 0, 0))
    forward = lambda kt: kt

    def cache_t_spec(tk, tile_of):
        return pl.BlockSpec((1, 1, N_STEP_HEADS, HEAD_DIM, tk), lambda b, kt: (layer, b, 0, 0, tile_of(kt)))

    def softmax_call(mode, branch, kc, vc, extra_args, extra_specs, name):
        past = kc.shape[-1]
        tk = min(past, 4096)
        assert past % tk == 0
        q, k_new, v_new = new[branch]
        if mode == "A":
            vc_spec = pl.BlockSpec((1, 1, tk, 4, LANES), lambda b, kt: (layer, b, kt, 0, 0))
            vn_spec = pl.BlockSpec((1, 1, frames, 4, LANES), lambda b, kt: (0, b, 0, 0, 0))
            dv = LANES
        else:
            vc_spec, vn_spec, dv = cache_t_spec(tk, forward), new_t_spec, HEAD_DIM
        return pl.pallas_call(
            functools.partial(_softmax_step_kernel, mode=mode, frames=frames, lam_init=lam_init),
            out_shape=out_shape,
            grid=(batch, past // tk),
            in_specs=[q_spec, cache_t_spec(tk, forward), vc_spec, new_t_spec, vn_spec] + extra_specs(tk),
            out_specs=out_spec,
            scratch_shapes=[pltpu.VMEM((N_STEP_HEADS, frames, HEAD_DIM), BF16), pltpu.VMEM((n_rows, 1), F32),
                            pltpu.VMEM((n_rows, 1), F32), pltpu.VMEM((n_rows, dv), F32)],
            compiler_params=_cparams(sem),
            name=name,
        )(q, kc, vc, k_new[None], v_new[None], *extra_args)

    o_a = softmax_call(
        "A", "a", caches["a_k"], caches["a_v"], (lam_vec, subln_w.reshape(1, LANES)),
        lambda tk: [pl.BlockSpec((4, HEAD_DIM), lambda b, kt: (0, 0)),
                    pl.BlockSpec((1, LANES), lambda b, kt: (0, 0))],
        "diff_attention_step")
    o_b = softmax_call(
        "B", "b", caches["b_k"], caches["b_v"], (bias_cache, bias_new),
        lambda tk: [pl.BlockSpec((n_rows, tk), lambda b, kt: (0, kt)),
                    pl.BlockSpec((n_rows, frames), lambda b, kt: (0, 0))],
        "band_attention_step")
    o_c = softmax_call(
        "C", "c", caches["c_k"], caches["c_v"], (cum_q_col, cum_cache_row, cum_new_row),
        lambda tk: [pl.BlockSpec((1, n_rows, 1), lambda b, kt: (b, 0, 0)),
                    pl.BlockSpec((1, N_STEP_HEADS, tk), lambda b, kt: (b, 0, kt)),
                    pl.BlockSpec((1, N_STEP_HEADS, frames), lambda b, kt: (b, 0, 0))],
        "forgetting_attention_step")

    past = caches["d_k"].shape[-1]
    tk = min(past, 1024)
    assert past % tk == 0 and tk % SUFFIX_BLOCK == 0
    n_kt = past // tk
    backward = lambda kt: n_kt - 1 - kt
    ones_after = jnp.tril(jnp.ones((SUFFIX_BLOCK, SUFFIX_BLOCK), F32), -1).astype(BF16)
    q, k_new, v_new = new["d"]
    o_d = pl.pallas_call(
        functools.partial(_stick_step_kernel, frames=frames),
        out_shape=out_shape,
        grid=(batch, n_kt),
        in_specs=[q_spec, cache_t_spec(tk, backward), cache_t_spec(tk, backward), new_t_spec, new_t_spec,
                  pl.BlockSpec((SUFFIX_BLOCK, SUFFIX_BLOCK), lambda b, kt: (0, 0))],
        out_specs=out_spec,
        scratch_shapes=[pltpu.VMEM((N_STEP_HEADS, frames, HEAD_DIM), BF16), pltpu.VMEM((n_rows, 1), F32),
                        pltpu.VMEM((n_rows, HEAD_DIM), F32)],
        compiler_params=_cparams(sem),
        name="stick_breaking_attention_step",
    )(q, caches["d_k"], caches["d_v"], k_new[None], v_new[None], ones_after)
    return o_a, o_b, o_c, o_d


def _merge_kernel(x_ref, g0_ref, g1_ref, g2_ref, g3_ref, b0_ref, b1_ref, b2_ref, b3_ref, wb_ref, o_ref,
                  gs_ref, ws_ref):
    @pl.when(pl.program_id(1) == 0)
    def _():
        for n, g_ref in enumerate((g0_ref, g1_ref, g2_ref, g3_ref)):
            gs_ref[n] = g_ref[0].astype(BF16)
        ws_ref[...] = wb_ref[0].astype(BF16)

    x = x_ref[...]
    acc = None
    for n, b_ref in enumerate((b0_ref, b1_ref, b2_ref, b3_ref)):
        term = _sigmoid(_dot_nt(x, gs_ref[n])) * _dot(b_ref[...], ws_ref[n])
        acc = term if acc is None else acc + term
    o_ref[...] = acc.astype(o_ref.dtype)


def gated_merge(xn, w_in_t, gate_row0, w_branch, layer, branches):
    t, d = xn.shape
    wcol = 256
    tm = _row_tile(t, 1024)
    bw = branches[0].shape[1]
    gate_specs = [pl.BlockSpec((pl.Element(1), pl.Element(wcol), pl.Element(d)),
                               functools.partial(lambda c, m, n: (layer, pl.multiple_of(gate_row0 + n * d + c * wcol, 8), 0), n=n))
                  for n in range(N_BRANCH)]
    branch_specs = [pl.BlockSpec((tm, bw), lambda c, m: (m, 0)) for _ in range(N_BRANCH)]
    return pl.pallas_call(
        _merge_kernel,
        out_shape=jax.ShapeDtypeStruct((t, d), BF16),
        grid=(d // wcol, t // tm),
        in_specs=[pl.BlockSpec((tm, d), lambda c, m: (m, 0))] + gate_specs + branch_specs
                 + [pl.BlockSpec((1, N_BRANCH, bw, wcol), lambda c, m: (layer, 0, 0, c))],
        out_specs=pl.BlockSpec((tm, wcol), lambda c, m: (m, c)),
        scratch_shapes=[pltpu.VMEM((N_BRANCH, wcol, d), BF16), pltpu.VMEM((N_BRANCH, bw, wcol), BF16)],
        compiler_params=_cparams(("parallel", "arbitrary")),
        name="gated_merge",
    )(xn, w_in_t, w_in_t, w_in_t, w_in_t, *branches, w_branch)


def _outproj_kernel(a_ref, w_ref, x_ref, o_ref):
    o_ref[...] = x_ref[...] + _dot(a_ref[...], w_ref[0])


def out_projection(merged, w_out, layer, x):
    t, d = x.shape
    tm = _row_tile(t, 512)
    return pl.pallas_call(
        _outproj_kernel,
        out_shape=jax.ShapeDtypeStruct((t, d), F32),
        grid=(t // tm,),
        in_specs=[pl.BlockSpec((tm, d), lambda m: (m, 0)),
                  pl.BlockSpec((1, d, d), lambda m: (layer, 0, 0)),
                  pl.BlockSpec((tm, d), lambda m: (m, 0))],
        out_specs=pl.BlockSpec((tm, d), lambda m: (m, 0)),
        compiler_params=_cparams(("parallel",)),
        name="out_projection",
    )(merged, w_out, x)


def _router_kernel(h_ref, g_ref, w_ref, b_ref, xn_ref, comb_ref):
    h = h_ref[...]
    ms = jnp.mean(h * h, axis=-1, keepdims=True)
    xn = h * lax.rsqrt(ms + RMS_EPS) * g_ref[...]
    xn_ref[...] = xn.astype(xn_ref.dtype)
    logits = jnp.dot(xn, w_ref[...], preferred_element_type=F32, precision=lax.Precision.HIGHEST) + b_ref[...]
    lane = lax.broadcasted_iota(jnp.int32, logits.shape, 1).astype(F32)
    far = float(LANES)

    def first_argmax(vals):
        top = jnp.max(vals, axis=-1, keepdims=True)
        idx = jnp.min(jnp.where(vals == top, lane, far), axis=-1, keepdims=True)
        return top, idx

    group_logits = jnp.where(lane < N_GROUPS, logits, -jnp.inf)
    g_top, g_idx = first_argmax(group_logits)
    g_weight = 1.0 / jnp.sum(jnp.exp(group_logits - g_top), axis=-1, keepdims=True)
    first = N_GROUPS + EXPERTS_PER_GROUP * g_idx
    in_group = (lane >= first) & (lane < first + EXPERTS_PER_GROUP)
    expert_logits = jnp.where(in_group, logits, -jnp.inf)
    top1, idx1 = first_argmax(expert_logits)
    top2, idx2 = first_argmax(jnp.where(lane == idx1, -jnp.inf, expert_logits))
    w1 = 1.0 / (1.0 + jnp.exp(top2 - top1))
    w2 = jnp.exp(top2 - top1) * w1
    comb_ref[...] = g_weight * (jnp.where(lane == idx1, w1, 0.0) + jnp.where(lane == idx2, w2, 0.0))


def router(h, g_ffn, w_router, b_router):
    t, d = h.shape
    tm = _row_tile(t, 256)
    return pl.pallas_call(
        _router_kernel,
        out_shape=(jax.ShapeDtypeStruct((t, d), BF16), jax.ShapeDtypeStruct((t, LANES), F32)),
        grid=(t // tm,),
        in_specs=[pl.BlockSpec((tm, d), lambda m: (m, 0)),
                  pl.BlockSpec((1, d), lambda m: (0, 0)),
                  pl.BlockSpec((d, LANES), lambda m: (0, 0)),
                  pl.BlockSpec((1, LANES), lambda m: (0, 0))],
        out_specs=(pl.BlockSpec((tm, d), lambda m: (m, 0)),
                   pl.BlockSpec((tm, LANES), lambda m: (m, 0))),
        compiler_params=_cparams(("parallel",)),
        name="router",
    )(h, g_ffn.reshape(1, d), w_router, b_router)


EXPERTS_PER_STEP = 2


def _moe_kernel(x_ref, comb_ref, wg_ref, wu_ref, wd_ref, h_ref, o_ref):
    step = pl.program_id(1)

    @pl.when(step == 0)
    def _():
        o_ref[...] = h_ref[...]

    x = x_ref[...]
    comb = comb_ref[...]
    lane = lax.broadcasted_iota(jnp.int32, comb.shape, 1)
    total = None
    for i in range(EXPERTS_PER_STEP):
        e = step * EXPERTS_PER_STEP + i
        weight = jnp.sum(jnp.where(lane == N_GROUPS + e, comb, 0.0), axis=-1, keepdims=True)
        gate = _dot(x, wg_ref[0, i])
        hid = gate * _sigmoid(gate) * _dot(x, wu_ref[0, i]) * weight
        y = _dot(hid.astype(BF16), wd_ref[0, i])
        total = y if total is None else total + y
    o_ref[...] += total


def moe(xn, comb, w_gate, w_up, w_down, layer, h):
    t, d = xn.shape
    _, n_exp, _, f = w_gate.shape
    tm = _row_tile(t, 512)
    return pl.pallas_call(
        _moe_kernel,
        out_shape=jax.ShapeDtypeStruct((t, d), F32),
        grid=(t // tm, n_exp // EXPERTS_PER_STEP),
        in_specs=[pl.BlockSpec((tm, d), lambda m, e: (m, 0)),
                  pl.BlockSpec((tm, LANES), lambda m, e: (m, 0)),
                  pl.BlockSpec((1, EXPERTS_PER_STEP, d, f), lambda m, e: (layer, e, 0, 0)),
                  pl.BlockSpec((1, EXPERTS_PER_STEP, d, f), lambda m, e: (layer, e, 0, 0)),
                  pl.BlockSpec((1, EXPERTS_PER_STEP, f, d), lambda m, e: (layer, e, 0, 0)),
                  pl.BlockSpec((tm, d), lambda m, e: (m, 0))],
        out_specs=pl.BlockSpec((tm, d), lambda m, e: (m, 0)),
        compiler_params=_cparams(("parallel", "arbitrary")),
        name="moe",
    )(xn, comb, w_gate, w_up, w_down, h)


def _rope_tables(pos, width):
    half = HEAD_DIM // 2
    inv_freq = jnp.exp(jnp.arange(half, dtype=F32) * (-2.0 * math.log(ROPE_THETA) / HEAD_DIM))
    ang = pos.astype(F32)[:, None] * inv_freq[None, :]
    reps = width // half
    cos = jnp.tile(jnp.cos(ang), (1, reps))
    sin = jnp.tile(jnp.sin(ang), (1, reps))
    lower = (jnp.arange(width) % HEAD_DIM) < half
    return cos, jnp.where(lower[None, :], -sin, sin)


def _rel_bias_table(rel_bias, qpos, kpos):
    rel = jnp.clip(qpos[:, None] - kpos[None, :], -REL_CLIP, REL_CLIP) + REL_CLIP
    return rel_bias[:, rel].astype(F32)


def _band_bias_table(rel_bias, tile):
    period = 4 * tile + 1
    m = jnp.arange(period)
    diff = jnp.where(m < 3 * tile, m, m - period)
    vec = rel_bias[:, jnp.clip(LEFT_CHUNKS * CHUNK - diff, -REL_CLIP, REL_CLIP) + REL_CLIP].astype(F32)
    flat = jnp.tile(vec, (1, tile))[:, :tile * (period - 1)]
    return flat.reshape(vec.shape[0], tile, period - 1)[:, :, :3 * tile]


def _layer_weights(l, p):
    n_qkv = N_SEG * 512
    w_in = p["w_in"][l]
    n_fg = w_in.shape[1] - n_qkv - N_BRANCH * w_in.shape[0]
    w_f = jnp.pad(p["w_in_t"][l, n_qkv:n_qkv + n_fg], ((0, LANES - n_fg), (0, 0)))
    b_f = jnp.pad(p["b_forget"][l], (0, LANES - n_fg)).reshape(1, LANES)
    w_router = jnp.pad(jnp.concatenate([p["router_group_w"][l], p["router_expert_w"][l]], axis=1),
                       ((0, 0), (0, LANES - N_GROUPS - N_EXPERTS)))
    b_router = jnp.pad(jnp.concatenate([p["router_group_b"][l], p["router_expert_b"][l]]),
                       (0, LANES - N_GROUPS - N_EXPERTS)).reshape(1, LANES)
    return dict(
        w_f=w_f, b_f=b_f, n_fg=n_fg,
        gate_row0=n_qkv + n_fg, w_router=w_router, b_router=b_router,
    )


def _ffn(h, l, p, lw):
    xn2, comb = router(h, p["norm_ffn"][l], lw["w_router"], lw["b_router"])
    return moe(xn2, comb, p["e_gate"], p["e_up"], p["e_down"], l, h)


def _prompt_layer(x, l, p, lw, lam_init, batch, seq, tables, state_bufs):
    t, d = x.shape
    xn = rmsnorm(x, p["norm_mix"][l], BF16)
    qkv_bf, state_bufs = in_projection_prompt(xn, p["w_in_t"], l, tables["cos"], tables["sin"], batch, seq, state_bufs)
    width = qkv_bf.shape[-1]
    n_fg = lw["n_fg"]
    logf = log_forget(xn, lw["w_f"], lw["b_f"])[:, :n_fg].reshape(batch, seq, n_fg)
    cum_row = cumsum_lanes(jnp.swapaxes(logf, 1, 2))
    cum_col = jnp.swapaxes(cum_row.reshape(batch, n_fg // 2, 2, seq), 2, 3)
    branches = prompt_attention(
        qkv_bf.reshape(N_SEG, batch, seq, width), cum_col, tables["band_bias"][l],
        p["diff_lambda"][l], p["diff_subln"][l], lam_init, batch, seq)
    merged = gated_merge(xn, p["w_in_t"], lw["gate_row0"], p["w_branch"], l, [b.reshape(t, width) for b in branches])
    h = out_projection(merged, p["w_out_bf"], l, x)
    h = _ffn(h, l, p, lw)
    return h, state_bufs, logf


def _step_layer(x, l, p, lw, lam_init, batch, frames, caches, tables):
    t, d = x.shape
    xn = rmsnorm(x, p["norm_mix"][l], BF16)
    qkv, _ = in_projection(xn, p["w_in_t"], l, tables["cos"], tables["sin"], frames)
    width = qkv.shape[-1]
    n_fg = lw["n_fg"]
    logf = log_forget(xn, lw["w_f"], lw["b_f"])[:, :n_fg].reshape(batch, frames, n_fg)
    past_f = caches["c_logf"][l]
    past = past_f.shape[1]
    total = past + frames
    padded = -(-total // LANES) * LANES
    series = jnp.concatenate([past_f, logf], axis=1)
    cum = cumsum_lanes(jnp.pad(jnp.swapaxes(series, 1, 2), ((0, 0), (0, 0), (0, padded - total))))
    cum_cache_row = cum[:, :, :past]
    cum_new_row = cum[:, :, past:total]
    cum_q_col = cum_new_row.reshape(batch, n_fg * frames, 1)
    heads = lambda i: qkv[i].reshape(batch, frames, N_STEP_HEADS, HEAD_DIM)
    q_heads = lambda i: jnp.transpose(heads(i), (0, 2, 1, 3))
    seq_minor = lambda i: jnp.transpose(heads(i), (0, 2, 3, 1))
    new = dict(a=(q_heads(0), seq_minor(1), qkv[2].reshape(batch, frames, 4, LANES)),
               b=(q_heads(3), seq_minor(4), seq_minor(5)),
               c=(q_heads(6), seq_minor(7), seq_minor(8)),
               d=(q_heads(9), seq_minor(10), seq_minor(11)))
    branches = step_attention(
        l, new, caches, cum_q_col, cum_cache_row, cum_new_row,
        tables["bias_cache"][l], tables["bias_new"][l], p["diff_lambda"][l], p["diff_subln"][l], lam_init)
    merged = gated_merge(xn, p["w_in_t"], lw["gate_row0"], p["w_branch"], l, [b.reshape(t, width) for b in branches])
    h = out_projection(merged, p["w_out_bf"], l, x)
    h = _ffn(h, l, p, lw)
    seg = lambda i: qkv[i].reshape(batch, frames, width)
    state = (seg(1), seg(2), seg(4), seg(5), seg(7), seg(8), logf, seg(10), seg(11))
    return h, state


def kernel(x_prompt, x_sample, cache_a_k, cache_a_v, cache_b_k, cache_b_v, cache_c_k, cache_c_v, cache_c_logf, cache_d_k, cache_d_v, norm_mix, w_in, b_forget, diff_lambda, diff_subln, rel_bias, w_branch, w_out, norm_ffn, router_group_w, router_group_b, router_expert_w, router_expert_b, expert_w_gate, expert_w_up, expert_w_down, norm_final):
    p = dict(norm_mix=norm_mix, w_in=w_in, w_in_t=jnp.swapaxes(w_in, 1, 2), b_forget=b_forget, diff_lambda=diff_lambda, diff_subln=diff_subln,
             w_branch=w_branch, w_out=w_out, norm_ffn=norm_ffn, router_group_w=router_group_w,
             router_group_b=router_group_b, router_expert_w=router_expert_w, router_expert_b=router_expert_b,
             w_out_bf=w_out.astype(BF16), e_gate=expert_w_gate.astype(BF16), e_up=expert_w_up.astype(BF16),
             e_down=expert_w_down.astype(BF16))
    depth = w_in.shape[0]
    bp, sp, d = x_prompt.shape
    bs, fs, _ = x_sample.shape
    past = cache_a_k.shape[2]
    width = 512
    n_heads = width // HEAD_DIM

    def seq_minor(c):
        c = jnp.moveaxis(c, 2, -1)
        return c.reshape(c.shape[0], c.shape[1], N_STEP_HEADS, HEAD_DIM, c.shape[-1])

    caches = dict(a_k=seq_minor(cache_a_k), a_v=cache_a_v, b_k=seq_minor(cache_b_k), b_v=seq_minor(cache_b_v),
                  c_k=seq_minor(cache_c_k), c_v=seq_minor(cache_c_v), c_logf=cache_c_logf,
                  d_k=seq_minor(cache_d_k), d_v=seq_minor(cache_d_v))

    pos_p = jnp.arange(sp, dtype=jnp.int32)
    pos_s = past + jnp.arange(fs, dtype=jnp.int32)
    cos_p, sin_p = _rope_tables(pos_p, width)
    cos_s, sin_s = _rope_tables(pos_s, width)
    tile = 256
    band_bias = jnp.stack([_band_bias_table(rel_bias[l], tile).reshape(n_heads // 2, 2, tile, 3 * tile)
                           for l in range(depth)])
    wb = cache_b_k.shape[2]
    kpos_b = jnp.arange(past - wb, past + fs, dtype=jnp.int32)
    step_bias = jnp.stack([_rel_bias_table(rel_bias[l], pos_s, kpos_b).reshape(n_heads * fs, wb + fs)
                           for l in range(depth)])
    tables_p = dict(cos=cos_p, sin=sin_p, band_bias=band_bias)
    tables_s = dict(cos=cos_s, sin=sin_s, bias_cache=step_bias[:, :, :wb], bias_new=step_bias[:, :, wb:])

    hp = x_prompt.reshape(bp * sp, d)
    hs = x_sample.reshape(bs * fs, d)
    bufs_p, logf_p, st_s = prompt_state_buffers(depth, bp, sp), [], []
    for l in range(depth):
        lam_init = 0.8 - 0.6 * math.exp(-0.3 * l)
        lw = _layer_weights(l, p)
        hp, bufs_p, logf = _prompt_layer(hp, l, p, lw, lam_init, bp, sp, tables_p, bufs_p)
        hs, state_s = _step_layer(hs, l, p, lw, lam_init, bs, fs, caches, tables_s)
        logf_p.append(logf)
        st_s.append(state_s)
    y_prompt = rmsnorm(hp, norm_final, F32).reshape(bp, sp, d)
    y_sample = rmsnorm(hs, norm_final, F32).reshape(bs, fs, d)

    h_a = width // (2 * HEAD_DIM)
    state_shapes = lambda b, s, sb: (
        (b, s, h_a, 2, HEAD_DIM), (b, s, h_a, 2 * HEAD_DIM), (b, sb, n_heads, HEAD_DIM), (b, sb, n_heads, HEAD_DIM),
        (b, s, n_heads, HEAD_DIM), (b, s, n_heads, HEAD_DIM), (b, s, n_heads), (b, s, n_heads, HEAD_DIM),
        (b, s, n_heads, HEAD_DIM))

    def stack(states, shapes):
        return tuple(jnp.stack([st[i].reshape(shapes[i]) for st in states], axis=0) for i in range(len(shapes)))

    token_major = lambda buf, dims: jnp.moveaxis(buf.reshape(buf.shape[:2] + dims + buf.shape[-1:]), -1, 2)
    a_k, a_v, b_k, b_v, c_k, c_v, d_k, d_v = bufs_p
    per_head = (n_heads, HEAD_DIM)
    out_p = (token_major(a_k, (h_a, 2, HEAD_DIM)), a_v, token_major(b_k, per_head), token_major(b_v, per_head),
             token_major(c_k, per_head), token_major(c_v, per_head), jnp.stack(logf_p, axis=0),
             token_major(d_k, per_head), token_major(d_v, per_head))
    out_s = stack(st_s, state_shapes(bs, fs, fs))
    return (y_prompt, y_sample) + out_p + out_s
```

```python
import functools
import math

import jax
import jax.numpy as jnp
from jax import lax
from jax.experimental import pallas as pl
from jax.experimental.pallas import tpu as pltpu

F32 = jnp.float32
BF16 = jnp.bfloat16

CHUNK = 64
HEAD_DIM = 64
LEFT_CHUNKS = 8
REL_CLIP = 128
ROPE_THETA = 10000.0
N_BRANCH = 4
N_GROUPS = 4
EXPERTS_PER_GROUP = 4
N_EXPERTS = N_GROUPS * EXPERTS_PER_GROUP
RMS_EPS = 1e-6
NEG_INF = -1e30
Q_SCALE = HEAD_DIM ** -0.5
N_SEG = 12

LANES = 128
VMEM_LIMIT = 56 * 1024 * 1024

NT_DIMS = (((1,), (1,)), ((), ()))


def _cparams(sem):
    return pltpu.CompilerParams(dimension_semantics=sem, vmem_limit_bytes=VMEM_LIMIT)


def _row_tile(t, pref):
    return pref if t % pref == 0 else t


def _sigmoid(z):
    return 1.0 / (1.0 + jnp.exp(-z))


def _dot(a, b):
    return jnp.dot(a, b, preferred_element_type=F32)


def _dot_nt(a, b):
    return lax.dot_general(a, b, NT_DIMS, preferred_element_type=F32)


def _rms_kernel(x_ref, g_ref, o_ref):
    x = x_ref[...]
    ms = jnp.mean(x * x, axis=-1, keepdims=True)
    o_ref[...] = (x * lax.rsqrt(ms + RMS_EPS) * g_ref[...]).astype(o_ref.dtype)


def rmsnorm(x, g, out_dtype):
    t, d = x.shape
    tm = _row_tile(t, 512)
    return pl.pallas_call(
        _rms_kernel,
        out_shape=jax.ShapeDtypeStruct((t, d), out_dtype),
        grid=(t // tm,),
        in_specs=[pl.BlockSpec((tm, d), lambda m: (m, 0)),
                  pl.BlockSpec((1, d), lambda m: (0, 0))],
        out_specs=pl.BlockSpec((tm, d), lambda m: (m, 0)),
        compiler_params=_cparams(("parallel",)),
        name="rmsnorm",
    )(x, g.reshape(1, d))


def _inproj_kernel(x_ref, w_ref, cos_ref, sin_ref, o_ref, ob_ref, wb_ref):
    n = pl.program_id(0)

    @pl.when(pl.program_id(1) == 0)
    def _():
        wb_ref[...] = w_ref[0].astype(BF16)

    acc = _dot_nt(x_ref[...], wb_ref[...])
    q_scale = jnp.where(n % 3 == 0, Q_SCALE, 1.0)

    @pl.when(n < 2)
    def _():
        width = acc.shape[1]
        half = HEAD_DIM // 2
        lane = lax.broadcasted_iota(jnp.int32, acc.shape, 1)
        lower = (lane % HEAD_DIM) < half
        partner = jnp.where(lower, pltpu.roll(acc, width - half, 1), pltpu.roll(acc, half, 1))
        roped = acc * cos_ref[...] + partner * sin_ref[...]
        o_ref[0] = roped
        ob_ref[0] = (roped * q_scale).astype(ob_ref.dtype)

    @pl.when(n >= 2)
    def _():
        o_ref[0] = acc
        ob_ref[0] = (acc * q_scale).astype(ob_ref.dtype)


def in_projection(xn, w_in_t, layer, cos_tab, sin_tab, seq):
    t, d = xn.shape
    w = 512
    tm = _row_tile(t, 512)
    if seq % tm == 0:
        n_tab = seq // tm
    else:
        assert tm % seq == 0
        cos_tab = jnp.tile(cos_tab, (tm // seq, 1))
        sin_tab = jnp.tile(sin_tab, (tm // seq, 1))
        n_tab = 1
    return pl.pallas_call(
        _inproj_kernel,
        out_shape=(jax.ShapeDtypeStruct((N_SEG, t, w), F32), jax.ShapeDtypeStruct((N_SEG, t, w), BF16)),
        grid=(N_SEG, t // tm),
        in_specs=[pl.BlockSpec((tm, d), lambda n, m: (m, 0)),
                  pl.BlockSpec((1, w, d), lambda n, m: (layer, n, 0)),
                  pl.BlockSpec((tm, w), lambda n, m: (m % n_tab, 0)),
                  pl.BlockSpec((tm, w), lambda n, m: (m % n_tab, 0))],
        out_specs=(pl.BlockSpec((1, tm, w), lambda n, m: (n, m, 0)),
                   pl.BlockSpec((1, tm, w), lambda n, m: (n, m, 0))),
        scratch_shapes=[pltpu.VMEM((w, d), BF16)],
        compiler_params=_cparams(("parallel", "arbitrary")),
        name="in_projection",
    )(xn, w_in_t, cos_tab, sin_tab)


STATE_SEGS = (1, 2, 4, 5, 7, 8, 10, 11)
VALUE_A_SEG = 2
BAND_SEGS = (4, 5)


def _inproj_prompt_kernel(*refs, n_alias, tiles_per_seq):
    x_ref, w_ref, cos_ref, sin_ref = refs[:4]
    ob_ref = refs[4 + n_alias]
    st_refs = refs[5 + n_alias:5 + n_alias + len(STATE_SEGS)]
    wb_ref = refs[-1]
    n, m = pl.program_id(0), pl.program_id(1)

    @pl.when(m == 0)
    def _():
        wb_ref[...] = w_ref[0].astype(BF16)

    acc = _dot_nt(x_ref[...], wb_ref[...])
    q_scale = jnp.where(n % 3 == 0, Q_SCALE, 1.0)

    def emit(val):
        ob_ref[0] = (val * q_scale).astype(ob_ref.dtype)
        for seg, st_ref in zip(STATE_SEGS, st_refs):
            if seg == VALUE_A_SEG:
                @pl.when(n == seg)
                def _(st_ref=st_ref):
                    for h in range(st_ref.shape[3]):
                        st_ref[0, 0, :, h, :] = val[:, h * LANES:(h + 1) * LANES]
            elif seg in BAND_SEGS:
                @pl.when(jnp.logical_and(n == seg, m % tiles_per_seq == tiles_per_seq - 1))
                def _(st_ref=st_ref):
                    st_ref[0, 0] = val.T
            else:
                @pl.when(n == seg)
                def _(st_ref=st_ref):
                    st_ref[0, 0] = val.T

    @pl.when(n < 2)
    def _():
        width = acc.shape[1]
        half = HEAD_DIM // 2
        lane = lax.broadcasted_iota(jnp.int32, acc.shape, 1)
        lower = (lane % HEAD_DIM) < half
        partner = jnp.where(lower, pltpu.roll(acc, width - half, 1), pltpu.roll(acc, half, 1))
        emit(acc * cos_ref[...] + partner * sin_ref[...])

    @pl.when(n >= 2)
    def _():
        emit(acc)


def prompt_state_buffers(depth, batch, seq, w=512):
    band = LEFT_CHUNKS * CHUNK
    shapes = []
    for seg in STATE_SEGS:
        if seg == VALUE_A_SEG:
            shapes.append((depth, batch, seq, w // LANES, LANES))
        elif seg in BAND_SEGS:
            shapes.append((depth, batch, w, band))
        else:
            shapes.append((depth, batch, w, seq))
    return tuple(jnp.zeros(s, F32) for s in shapes)


def in_projection_prompt(xn, w_in_t, layer, cos_tab, sin_tab, batch, seq, state_bufs):
    t, d = xn.shape
    w = 512
    tm = 512
    band = LEFT_CHUNKS * CHUNK
    assert seq % tm == 0 and band == tm and t == batch * seq
    per_seq = seq // tm
    n_tiles = t // tm

    def parked(seg, n, m):
        return jnp.where(n < seg, 0, jnp.where(n > seg, n_tiles - 1, m))

    specs = []
    for seg in STATE_SEGS:
        if seg == VALUE_A_SEG:
            specs.append(pl.BlockSpec((1, 1, tm, w // LANES, LANES), functools.partial(
                lambda n, m, seg: (layer, parked(seg, n, m) // per_seq, parked(seg, n, m) % per_seq, 0, 0), seg=seg)))
        elif seg in BAND_SEGS:
            specs.append(pl.BlockSpec((1, 1, w, band), functools.partial(
                lambda n, m, seg: (layer, parked(seg, n, m) // per_seq, 0, 0), seg=seg)))
        else:
            specs.append(pl.BlockSpec((1, 1, w, tm), functools.partial(
                lambda n, m, seg: (layer, parked(seg, n, m) // per_seq, 0, parked(seg, n, m) % per_seq), seg=seg)))
    alias_in = list(state_bufs)
    shapes = [jax.ShapeDtypeStruct(b.shape, b.dtype) for b in alias_in]
    n_fixed = 4
    out = pl.pallas_call(
        functools.partial(_inproj_prompt_kernel, n_alias=len(alias_in), tiles_per_seq=per_seq),
        out_shape=[jax.ShapeDtypeStruct((N_SEG, t, w), BF16)] + shapes,
        grid=(N_SEG, n_tiles),
        in_specs=[pl.BlockSpec((tm, d), lambda n, m: (m, 0)),
                  pl.BlockSpec((1, w, d), lambda n, m: (layer, n, 0)),
                  pl.BlockSpec((tm, w), lambda n, m: (m % per_seq, 0)),
                  pl.BlockSpec((tm, w), lambda n, m: (m % per_seq, 0))]
                 + [pl.BlockSpec(memory_space=pl.ANY) for _ in alias_in],
        out_specs=[pl.BlockSpec((1, tm, w), lambda n, m: (n, m, 0))] + specs,
        input_output_aliases={n_fixed + i: 1 + i for i in range(len(alias_in))},
        scratch_shapes=[pltpu.VMEM((w, d), BF16)],
        compiler_params=_cparams(("arbitrary", "arbitrary")),
        name="in_projection_prompt",
    )(xn, w_in_t, cos_tab, sin_tab, *alias_in)
    return out[0], tuple(out[1:])


def _logf_kernel(x_ref, w_ref, b_ref, o_ref):
    z = _dot_nt(x_ref[...], w_ref[...].astype(BF16)) + b_ref[...]
    o_ref[...] = jnp.minimum(z, 0.0) - jnp.log1p(jnp.exp(-jnp.abs(z)))


def log_forget(xn, w_f, b_f):
    t, d = xn.shape
    tm = _row_tile(t, 512)
    return pl.pallas_call(
        _logf_kernel,
        out_shape=jax.ShapeDtypeStruct((t, LANES), F32),
        grid=(t // tm,),
        in_specs=[pl.BlockSpec((tm, d), lambda m: (m, 0)),
                  pl.BlockSpec((LANES, d), lambda m: (0, 0)),
                  pl.BlockSpec((1, LANES), lambda m: (0, 0))],
        out_specs=pl.BlockSpec((tm, LANES), lambda m: (m, 0)),
        compiler_params=_cparams(("parallel",)),
        name="log_forget",
    )(xn, w_f, b_f)


def _cumsum_kernel(x_ref, o_ref):
    x = x_ref[0]
    length = x.shape[1]
    lane = lax.broadcasted_iota(jnp.int32, x.shape, 1)
    shift = 1
    while shift < length:
        x = x + jnp.where(lane >= shift, pltpu.roll(x, shift, 1), 0.0)
        shift *= 2
    o_ref[0] = x


def cumsum_lanes(x):
    b, h, length = x.shape
    return pl.pallas_call(
        _cumsum_kernel,
        out_shape=jax.ShapeDtypeStruct(x.shape, F32),
        grid=(b,),
        in_specs=[pl.BlockSpec((1, h, length), lambda i: (i, 0, 0))],
        out_specs=pl.BlockSpec((1, h, length), lambda i: (i, 0, 0)),
        compiler_params=_cparams(("parallel",)),
        name="cumsum",
    )(x)


def _diff_lambda(lam_ref, lam_init):
    lv = lam_ref[...]
    a = jnp.sum(lv[0:1] * lv[1:2], axis=1, keepdims=True)
    b = jnp.sum(lv[2:3] * lv[3:4], axis=1, keepdims=True)
    return jnp.exp(a) - jnp.exp(b) + lam_init


def _subln(o, w, lam_init):
    ms = jnp.mean(o * o, axis=-1, keepdims=True)
    return o * lax.rsqrt(ms + RMS_EPS) * w * (1.0 - lam_init)


def _head_lanes(shape, j):
    lane = lax.broadcasted_iota(jnp.int32, shape, 1)
    return (lane >= j * HEAD_DIM) & (lane < (j + 1) * HEAD_DIM)


def _split3(x):
    hi = x.astype(BF16).astype(F32)
    mid = (x - hi).astype(BF16).astype(F32)
    lo = (x - hi - mid).astype(BF16).astype(F32)
    return hi, mid, lo


def _with_bias_columns(x, j, cols):
    lane = lax.broadcasted_iota(jnp.int32, x.shape, 1)
    out = jnp.where(_head_lanes(x.shape, j), x.astype(F32), 0.0)
    base = (1 - j) * HEAD_DIM
    for i, c in enumerate(cols):
        out = jnp.where(lane == base + i, c, out)
    return out.astype(BF16)


def _softmax_tile_update(state, s, v):
    m, l, acc = state
    m_new = jnp.maximum(m, jnp.max(s, axis=-1, keepdims=True))
    alpha = jnp.exp(m - m_new)
    p = jnp.exp(s - m_new)
    return (m_new, alpha * l + jnp.sum(p, axis=-1, keepdims=True), alpha * acc + _dot(p.astype(BF16), v))


def _causal_softmax_sweep(qs, k_rows, v_rows, diag_mask, qi, tile):
    init = tuple((jnp.full((tile, 1), NEG_INF, F32), jnp.zeros((tile, 1), F32), jnp.zeros((tile, LANES), F32))
                 for _ in qs)

    def update(states, start, size, mask):
        start = pl.multiple_of(start, tile)
        v = v_rows(start, size)
        out = []
        for j, q in enumerate(qs):
            s = _dot_nt(q, k_rows(j, start, size))
            if mask is not None:
                s = jnp.where(mask, s, NEG_INF)
            out.append(_softmax_tile_update(states[j], s, v))
        return tuple(out)

    pairs = qi // 2
    states = lax.fori_loop(0, pairs, lambda i, st: update(st, i * 2 * tile, 2 * tile, None), init)
    states = lax.fori_loop(0, qi - 2 * pairs, lambda i, st: update(st, pairs * 2 * tile, tile, None), states)
    return update(states, qi * tile, tile, diag_mask)


def _a_prompt_kernel(q_ref, k_ref, v_ref, lam_ref, w_ref, o_ref, *, tile, lam_init):
    qi = pl.program_id(2)
    q = q_ref[0, 0]
    qs = [jnp.where(_head_lanes(q.shape, j), q, jnp.zeros_like(q)) for j in range(2)]
    row = lax.broadcasted_iota(jnp.int32, (tile, tile), 0)
    col = lax.broadcasted_iota(jnp.int32, (tile, tile), 1)
    diag_mask = (col // CHUNK) <= (row // CHUNK)
    k_rows = lambda j, start, size: k_ref[0, 0, pl.ds(start, size), :]
    v_rows = lambda start, size: v_ref[0, 0, pl.ds(start, size), :]
    (_, l1, acc1), (_, l2, acc2) = _causal_softmax_sweep(qs, k_rows, v_rows, diag_mask, qi, tile)
    lam = _diff_lambda(lam_ref, lam_init)
    o = acc1 / l1 - lam * (acc2 / l2)
    o_ref[0] = _subln(o, w_ref[...], lam_init).astype(o_ref.dtype)


def _c_prompt_kernel(q_ref, k_ref, v_ref, cq_ref, ck_ref, o_ref, kb_ref, *, tile):
    qi = pl.program_id(2)
    one = jnp.float32(1.0)

    @pl.when(qi == 0)
    def _():
        k = k_ref[0, 0]
        ck = ck_ref[0, 0]
        for j in range(2):
            hi, mid, lo = _split3(-ck[:, j:j + 1])
            kb_ref[j] = _with_bias_columns(k, j, [one, one, one, hi, mid, lo])

    q = q_ref[0, 0]
    cq = cq_ref[0, 0]
    qs = []
    for j in range(2):
        hi, mid, lo = _split3(cq[:, j:j + 1])
        qs.append(_with_bias_columns(q, j, [hi, mid, lo, one, one, one]))
    row = lax.broadcasted_iota(jnp.int32, (tile, tile), 0)
    col = lax.broadcasted_iota(jnp.int32, (tile, tile), 1)
    k_rows = lambda j, start, size: kb_ref[j, pl.ds(start, size), :]
    v_rows = lambda start, size: v_ref[0, 0, pl.ds(start, size), :]
    (_, l1, acc1), (_, l2, acc2) = _causal_softmax_sweep(qs, k_rows, v_rows, col <= row, qi, tile)
    o_ref[0] = jnp.where(_head_lanes(acc1.shape, 0), acc1 / l1, acc2 / l2).astype(o_ref.dtype)


SUFFIX_BLOCK = 256


def _strict_upper_ones(n):
    r = lax.broadcasted_iota(jnp.int32, (n, n), 0)
    c = lax.broadcasted_iota(jnp.int32, (n, n), 1)
    return jnp.where(r > c, 1.0, 0.0).astype(BF16)


def _suffix_sums(lm, ones_after, two_pass):
    hi = lm.astype(BF16)
    out = _dot(hi, ones_after)
    if two_pass:
        out = out + _dot((lm - hi.astype(F32)).astype(BF16), ones_after)
    return out


def _stick_weights(z, mask, run, ones_after, two_pass=True):
    sp = jnp.maximum(z, 0.0) + jnp.log(1.0 + jnp.exp(-jnp.abs(z)))
    lm = -sp if mask is None else jnp.where(mask, -sp, 0.0)
    block = ones_after.shape[0]
    n_block = z.shape[1] // block
    later = [None] * n_block
    for c in reversed(range(n_block)):
        part = lm[:, c * block:(c + 1) * block]
        later[c] = _suffix_sums(part, ones_after, two_pass) + run
        run = run + jnp.sum(part, axis=-1, keepdims=True)
    later = later[0] if n_block == 1 else jnp.concatenate(later, axis=1)
    a = jnp.exp((z - sp) + later)
    if mask is not None:
        a = jnp.where(mask, a, 0.0)
    return a, run


def _d_prompt_kernel(q_ref, k_ref, v_ref, o_ref, *, tile):
    qi = pl.program_id(2)
    q = q_ref[0, 0]
    qs = [jnp.where(_head_lanes(q.shape, j), q, jnp.zeros_like(q)) for j in range(2)]
    ones_after = _strict_upper_ones(SUFFIX_BLOCK)
    row = lax.broadcasted_iota(jnp.int32, (tile, tile), 0)
    col = lax.broadcasted_iota(jnp.int32, (tile, tile), 1)

    def update(states, start, size, mask):
        start = pl.multiple_of(start, tile)
        k = k_ref[0, 0, pl.ds(start, size), :]
        v = v_ref[0, 0, pl.ds(start, size), :]
        out = []
        for j, q_j in enumerate(qs):
            run, acc = states[j]
            a, run = _stick_weights(_dot_nt(q_j, k), mask, run, ones_after, two_pass=False)
            out.append((run, acc + _dot(a.astype(BF16), v)))
        return tuple(out)

    init = tuple((jnp.zeros((tile, 1), F32), jnp.zeros((tile, LANES), F32)) for _ in qs)
    states = update(init, qi * tile, tile, col < row)
    pairs = qi // 2
    states = lax.fori_loop(0, pairs, lambda i, st: update(st, (qi - 2 - 2 * i) * tile, 2 * tile, None), states)
    states = lax.fori_loop(0, qi - 2 * pairs, lambda i, st: update(st, 0, tile, None), states)
    o_ref[0] = jnp.where(_head_lanes((tile, LANES), 0), states[0][1], states[1][1]).astype(o_ref.dtype)


def _b_prompt_kernel(q_ref, k0_ref, k1_ref, k2_ref, v0_ref, v1_ref, v2_ref, bias_ref, o_ref, *, tile):
    qi = pl.program_id(2)
    q = q_ref[0, 0]
    row_chunk = lax.broadcasted_iota(jnp.int32, (tile, tile), 0) // CHUNK
    col_chunk = lax.broadcasted_iota(jnp.int32, (tile, tile), 1) // CHUNK
    back = 2
    ks = [r[0, 0] for r in (k0_ref, k1_ref, k2_ref)]
    vs = [r[0, 0] for r in (v0_ref, v1_ref, v2_ref)]
    valids = []
    for c in range(3):
        dist = row_chunk - col_chunk + (back - c) * (tile // CHUNK)
        valids.append((dist >= 0) & (dist <= LEFT_CHUNKS) & (qi - back + c >= 0))
    outs = []
    for j in range(2):
        q_j = jnp.where(_head_lanes(q.shape, j), q, jnp.zeros_like(q))
        scores = []
        for c in range(3):
            s = _dot_nt(q_j, ks[c]) + bias_ref[0, j, :, c * tile:(c + 1) * tile]
            scores.append(jnp.where(valids[c], s, NEG_INF))
        m = jnp.maximum(jnp.maximum(jnp.max(scores[0], axis=-1, keepdims=True),
                                    jnp.max(scores[1], axis=-1, keepdims=True)),
                        jnp.max(scores[2], axis=-1, keepdims=True))
        ps = [jnp.exp(s - m) for s in scores]
        denom = sum(jnp.sum(p, axis=-1, keepdims=True) for p in ps)
        o = sum(_dot(p.astype(BF16), v) for p, v in zip(ps, vs))
        outs.append(o / denom)
    o_ref[0] = jnp.where(_head_lanes(outs[0].shape, 0), outs[0], outs[1]).astype(o_ref.dtype)


def prompt_attention(qkv, cum_col, band_bias, lam_vec, subln_w, lam_init, batch, seq):
    width = qkv.shape[-1]
    n_pair = width // LANES
    out_shape = jax.ShapeDtypeStruct((batch, seq, width), BF16)
    tile = 512
    assert seq % tile == 0 and tile % CHUNK == 0
    sem = ("parallel", "parallel", "arbitrary")

    q_spec = lambda seg: pl.BlockSpec((1, 1, tile, LANES), lambda b, h, qi: (seg, b, qi, h))
    full_spec = lambda seg: pl.BlockSpec((1, 1, seq, LANES), lambda b, h, qi: (seg, b, 0, h))
    out_spec = pl.BlockSpec((1, tile, LANES), lambda b, h, qi: (b, qi, h))
    grid = (batch, n_pair, seq // tile)

    o_a = pl.pallas_call(
        functools.partial(_a_prompt_kernel, tile=tile, lam_init=lam_init),
        out_shape=out_shape, grid=grid,
        in_specs=[q_spec(0), full_spec(1), full_spec(2),
                  pl.BlockSpec((4, HEAD_DIM), lambda b, h, qi: (0, 0)),
                  pl.BlockSpec((1, LANES), lambda b, h, qi: (0, 0))],
        out_specs=out_spec,
        compiler_params=_cparams(sem),
        name="diff_attention",
    )(qkv, qkv, qkv, lam_vec, subln_w.reshape(1, LANES))

    o_c = pl.pallas_call(
        functools.partial(_c_prompt_kernel, tile=tile),
        out_shape=out_shape, grid=grid,
        in_specs=[q_spec(6), full_spec(7), full_spec(8),
                  pl.BlockSpec((1, 1, tile, 2), lambda b, h, qi: (b, h, qi, 0)),
                  pl.BlockSpec((1, 1, seq, 2), lambda b, h, qi: (b, h, 0, 0))],
        out_specs=out_spec,
        scratch_shapes=[pltpu.VMEM((2, seq, LANES), BF16)],
        compiler_params=_cparams(sem),
        name="forgetting_attention",
    )(qkv, qkv, qkv, cum_col, cum_col)

    o_d = pl.pallas_call(
        functools.partial(_d_prompt_kernel, tile=tile),
        out_shape=out_shape, grid=grid,
        in_specs=[q_spec(9), full_spec(10), full_spec(11)],
        out_specs=out_spec,
        compiler_params=_cparams(sem),
        name="stick_breaking_attention",
    )(qkv, qkv, qkv)

    band_tile = 256
    assert seq % band_tile == 0 and band_tile * 2 == LEFT_CHUNKS * CHUNK

    def band_spec(seg, back):
        return pl.BlockSpec((1, 1, band_tile, LANES), lambda b, h, qi: (seg, b, jnp.maximum(qi - back, 0), h))

    o_b = pl.pallas_call(
        functools.partial(_b_prompt_kernel, tile=band_tile),
        out_shape=out_shape,
        grid=(batch, n_pair, seq // band_tile),
        in_specs=[band_spec(3, 0),
                  band_spec(4, 2), band_spec(4, 1), band_spec(4, 0),
                  band_spec(5, 2), band_spec(5, 1), band_spec(5, 0),
                  pl.BlockSpec((1, 2, band_tile, 3 * band_tile), lambda b, h, qi: (h, 0, 0, 0))],
        out_specs=pl.BlockSpec((1, band_tile, LANES), lambda b, h, qi: (b, qi, h)),
        compiler_params=_cparams(("parallel", "parallel", "parallel")),
        name="band_attention",
    )(qkv, qkv, qkv, qkv, qkv, qkv, qkv, band_bias)
    return o_a, o_b, o_c, o_d


N_STEP_HEADS = 8


def _step_scores(qs_ref, kt_ref):
    rows = [_dot(qs_ref[h], kt_ref[0, 0, h].astype(BF16)) for h in range(N_STEP_HEADS)]
    return jnp.concatenate(rows, axis=0)


def _step_pv(p, v_ref, frames, v_token_major):
    pb = p.astype(BF16)
    if v_token_major:
        rows = [_dot(pb[2 * g * frames:(2 * g + 2) * frames], v_ref[0, 0, :, g, :].astype(BF16))
                for g in range(N_STEP_HEADS // 2)]
    else:
        rows = [_dot_nt(pb[h * frames:(h + 1) * frames], v_ref[0, 0, h].astype(BF16))
                for h in range(N_STEP_HEADS)]
    return jnp.concatenate(rows, axis=0)


def _store_scaled_queries(q_ref, qs_ref):
    for h in range(N_STEP_HEADS):
        qs_ref[h] = (q_ref[0, h] * Q_SCALE).astype(BF16)


def _rows_per_head(x, frames):
    return jnp.concatenate([jnp.broadcast_to(x[h:h + 1, :], (frames, x.shape[1]))
                            for h in range(x.shape[0])], axis=0)


def _new_rows_mask(frames, n_rows, strict):
    frame = lax.broadcasted_iota(jnp.int32, (n_rows, frames), 0) % frames
    key = lax.broadcasted_iota(jnp.int32, (n_rows, frames), 1)
    return key < frame if strict else key <= frame


def _heads_to_lanes(acc, frames):
    return jnp.concatenate([acc[h * frames:(h + 1) * frames] for h in range(N_STEP_HEADS)], axis=1)


def _softmax_step_kernel(*refs, mode, frames, lam_init):
    if mode == "A":
        (q_ref, kc_ref, vc_ref, kn_ref, vn_ref, lam_ref, w_ref,
         o_ref, qs_ref, m_ref, l_ref, acc_ref) = refs
    elif mode == "B":
        (q_ref, kc_ref, vc_ref, kn_ref, vn_ref, bias_c_ref, bias_n_ref,
         o_ref, qs_ref, m_ref, l_ref, acc_ref) = refs
    else:
        (q_ref, kc_ref, vc_ref, kn_ref, vn_ref, cq_ref, ckc_ref, ckn_ref,
         o_ref, qs_ref, m_ref, l_ref, acc_ref) = refs
    n_rows = N_STEP_HEADS * frames
    token_major_v = mode == "A"
    kt = pl.program_id(1)

    @pl.when(kt == 0)
    def _():
        _store_scaled_queries(q_ref, qs_ref)
        m_ref[...] = jnp.full(m_ref.shape, NEG_INF, F32)
        l_ref[...] = jnp.zeros(l_ref.shape, F32)
        acc_ref[...] = jnp.zeros(acc_ref.shape, F32)

    def softmax_tile(s, v_ref):
        m_prev = m_ref[...]
        m_new = jnp.maximum(m_prev, jnp.max(s, axis=-1, keepdims=True))
        alpha = jnp.exp(m_prev - m_new)
        p = jnp.exp(s - m_new)
        l_ref[...] = alpha * l_ref[...] + jnp.sum(p, axis=-1, keepdims=True)
        acc_ref[...] = alpha * acc_ref[...] + _step_pv(p, v_ref, frames, token_major_v)
        m_ref[...] = m_new

    s = _step_scores(qs_ref, kc_ref)
    if mode == "B":
        s = s + bias_c_ref[...]
    elif mode == "C":
        s = s + cq_ref[0] - _rows_per_head(ckc_ref[0], frames)
    softmax_tile(s, vc_ref)

    @pl.when(kt == pl.num_programs(1) - 1)
    def _():
        s = _step_scores(qs_ref, kn_ref)
        if mode == "B":
            s = s + bias_n_ref[...]
        elif mode == "C":
            s = s + cq_ref[0] - _rows_per_head(ckn_ref[0], frames)
            s = jnp.where(_new_rows_mask(frames, n_rows, strict=False), s, NEG_INF)
        softmax_tile(s, vn_ref)
        o = acc_ref[...] / l_ref[...]
        if mode == "A":
            lam = _diff_lambda(lam_ref, lam_init)
            pieces = []
            for h in range(N_STEP_HEADS // 2):
                o1 = o[(2 * h) * frames:(2 * h + 1) * frames]
                o2 = o[(2 * h + 1) * frames:(2 * h + 2) * frames]
                pieces.append(_subln(o1 - lam * o2, w_ref[...], lam_init))
            o_ref[0] = jnp.concatenate(pieces, axis=1).astype(o_ref.dtype)
        else:
            o_ref[0] = _heads_to_lanes(o, frames).astype(o_ref.dtype)


def _stick_step_kernel(q_ref, kc_ref, vc_ref, kn_ref, vn_ref, ones_ref, o_ref,
                       qs_ref, run_ref, acc_ref, *, frames):
    n_rows = N_STEP_HEADS * frames
    kt = pl.program_id(1)

    @pl.when(kt == 0)
    def _():
        _store_scaled_queries(q_ref, qs_ref)
        z = _step_scores(qs_ref, kn_ref)
        mask = _new_rows_mask(frames, n_rows, strict=True)
        a, run = _stick_weights(z, mask, jnp.zeros((n_rows, 1), F32), _strict_upper_ones(frames))
        acc_ref[...] = _step_pv(a, vn_ref, frames, False)
        run_ref[...] = run

    z = _step_scores(qs_ref, kc_ref)
    a, run = _stick_weights(z, None, run_ref[...], ones_ref[...])
    acc_ref[...] += _step_pv(a, vc_ref, frames, False)
    run_ref[...] = run

    @pl.when(kt == pl.num_programs(1) - 1)
    def _():
        o_ref[0] = _heads_to_lanes(acc_ref[...], frames).astype(o_ref.dtype)


def step_attention(layer, new, caches, cum_q_col, cum_cache_row, cum_new_row, bias_cache, bias_new,
                   lam_vec, subln_w, lam_init):
    batch, _, frames, _ = new["a"][0].shape
    width = N_STEP_HEADS * HEAD_DIM
    n_rows = N_STEP_HEADS * frames
    out_shape = jax.ShapeDtypeStruct((batch, frames, width), BF16)
    sem = ("parallel", "arbitrary")
    out_spec = pl.BlockSpec((1, frames, width), lambda b, kt: (b, 0, 0))
    q_spec = pl.BlockSpec((1, N_STEP_HEADS, frames, HEAD_DIM), lambda b, kt: (b, 0, 0, 0))
    new_t_spec = pl.BlockSpec((1, 1, N_STEP_HEADS, HEAD_DIM, frames), lambda b, kt: (0, b, 0, 0, 0))
    forward = lambda kt: kt

    def cache_t_spec(tk, tile_of):
        return pl.BlockSpec((1, 1, N_STEP_HEADS, HEAD_DIM, tk), lambda b, kt: (layer, b, 0, 0, tile_of(kt)))

    def softmax_call(mode, branch, kc, vc, extra_args, extra_specs, name):
        past = kc.shape[-1]
        tk = min(past, 4096)
        assert past % tk == 0
        q, k_new, v_new = new[branch]
        if mode == "A":
            vc_spec = pl.BlockSpec((1, 1, tk, 4, LANES), lambda b, kt: (layer, b, kt, 0, 0))
            vn_spec = pl.BlockSpec((1, 1, frames, 4, LANES), lambda b, kt: (0, b, 0, 0, 0))
            dv = LANES
        else:
            vc_spec, vn_spec, dv = cache_t_spec(tk, forward), new_t_spec, HEAD_DIM
        return pl.pallas_call(
            functools.partial(_softmax_step_kernel, mode=mode, frames=frames, lam_init=lam_init),
            out_shape=out_shape,
            grid=(batch, past // tk),
            in_specs=[q_spec, cache_t_spec(tk, forward), vc_spec, new_t_spec, vn_spec] + extra_specs(tk),
            out_specs=out_spec,
            scratch_shapes=[pltpu.VMEM((N_STEP_HEADS, frames, HEAD_DIM), BF16), pltpu.VMEM((n_rows, 1), F32),
                            pltpu.VMEM((n_rows, 1), F32), pltpu.VMEM((n_rows, dv), F32)],
            compiler_params=_cparams(sem),
            name=name,
        )(q, kc, vc, k_new[None], v_new[None], *extra_args)

    o_a = softmax_call(
        "A", "a", caches["a_k"], caches["a_v"], (lam_vec, subln_w.reshape(1, LANES)),
        lambda tk: [pl.BlockSpec((4, HEAD_DIM), lambda b, kt: (0, 0)),
                    pl.BlockSpec((1, LANES), lambda b, kt: (0, 0))],
        "diff_attention_step")
    o_b = softmax_call(
        "B", "b", caches["b_k"], caches["b_v"], (bias_cache, bias_new),
        lambda tk: [pl.BlockSpec((n_rows, tk), lambda b, kt: (0, kt)),
                    pl.BlockSpec((n_rows, frames), lambda b, kt: (0, 0))],
        "band_attention_step")
    o_c = softmax_call(
        "C", "c", caches["c_k"], caches["c_v"], (cum_q_col, cum_cache_row, cum_new_row),
        lambda tk: [pl.BlockSpec((1, n_rows, 1), lambda b, kt: (b, 0, 0)),
                    pl.BlockSpec((1, N_STEP_HEADS, tk), lambda b, kt: (b, 0, kt)),
                    pl.BlockSpec((1, N_STEP_HEADS, frames), lambda b, kt: (b, 0, 0))],
        "forgetting_attention_step")

    past = caches["d_k"].shape[-1]
    tk = min(past, 2048)
    assert past % tk == 0 and tk % SUFFIX_BLOCK == 0
    n_kt = past // tk
    backward = lambda kt: n_kt - 1 - kt
    ones_after = jnp.tril(jnp.ones((SUFFIX_BLOCK, SUFFIX_BLOCK), F32), -1).astype(BF16)
    q, k_new, v_new = new["d"]
    o_d = pl.pallas_call(
        functools.partial(_stick_step_kernel, frames=frames),
        out_shape=out_shape,
        grid=(batch, n_kt),
        in_specs=[q_spec, cache_t_spec(tk, backward), cache_t_spec(tk, backward), new_t_spec, new_t_spec,
                  pl.BlockSpec((SUFFIX_BLOCK, SUFFIX_BLOCK), lambda b, kt: (0, 0))],
        out_specs=out_spec,
        scratch_shapes=[pltpu.VMEM((N_STEP_HEADS, frames, HEAD_DIM), BF16), pltpu.VMEM((n_rows, 1), F32),
                        pltpu.VMEM((n_rows, HEAD_DIM), F32)],
        compiler_params=_cparams(sem),
        name="stick_breaking_attention_step",
    )(q, caches["d_k"], caches["d_v"], k_new[None], v_new[None], ones_after)
    return o_a, o_b, o_c, o_d


def _merge_kernel(x_ref, g0_ref, g1_ref, g2_ref, g3_ref, b0_ref, b1_ref, b2_ref, b3_ref, wb_ref, o_ref,
                  gs_ref, ws_ref):
    @pl.when(pl.program_id(1) == 0)
    def _():
        for n, g_ref in enumerate((g0_ref, g1_ref, g2_ref, g3_ref)):
            gs_ref[n] = g_ref[0].astype(BF16)
        ws_ref[...] = wb_ref[0].astype(BF16)

    x = x_ref[...]
    acc = None
    for n, b_ref in enumerate((b0_ref, b1_ref, b2_ref, b3_ref)):
        term = _sigmoid(_dot_nt(x, gs_ref[n])) * _dot(b_ref[...], ws_ref[n])
        acc = term if acc is None else acc + term
    o_ref[...] = acc.astype(o_ref.dtype)


def gated_merge(xn, w_in_t, gate_row0, w_branch, layer, branches):
    t, d = xn.shape
    wcol = 256
    tm = _row_tile(t, 1024)
    bw = branches[0].shape[1]
    gate_specs = [pl.BlockSpec((pl.Element(1), pl.Element(wcol), pl.Element(d)),
                               functools.partial(lambda c, m, n: (layer, pl.multiple_of(gate_row0 + n * d + c * wcol, 8), 0), n=n))
                  for n in range(N_BRANCH)]
    branch_specs = [pl.BlockSpec((tm, bw), lambda c, m: (m, 0)) for _ in range(N_BRANCH)]
    return pl.pallas_call(
        _merge_kernel,
        out_shape=jax.ShapeDtypeStruct((t, d), BF16),
        grid=(d // wcol, t // tm),
        in_specs=[pl.BlockSpec((tm, d), lambda c, m: (m, 0))] + gate_specs + branch_specs
                 + [pl.BlockSpec((1, N_BRANCH, bw, wcol), lambda c, m: (layer, 0, 0, c))],
        out_specs=pl.BlockSpec((tm, wcol), lambda c, m: (m, c)),
        scratch_shapes=[pltpu.VMEM((N_BRANCH, wcol, d), BF16), pltpu.VMEM((N_BRANCH, bw, wcol), BF16)],
        compiler_params=_cparams(("parallel", "arbitrary")),
        name="gated_merge",
    )(xn, w_in_t, w_in_t, w_in_t, w_in_t, *branches, w_branch)


def _outproj_router_kernel(a_ref, wo_ref, x_ref, g_ref, w_ref, b_ref, h_ref, xn_ref, comb_ref):
    h = x_ref[...] + _dot(a_ref[...], wo_ref[0])
    h_ref[...] = h
    ms = jnp.mean(h * h, axis=-1, keepdims=True)
    xn = h * lax.rsqrt(ms + RMS_EPS) * g_ref[...]
    xn_ref[...] = xn.astype(xn_ref.dtype)
    logits = jnp.dot(xn, w_ref[...], preferred_element_type=F32, precision=lax.Precision.HIGHEST) + b_ref[...]
    lane = lax.broadcasted_iota(jnp.int32, logits.shape, 1).astype(F32)
    far = float(LANES)

    def first_argmax(vals):
        top = jnp.max(vals, axis=-1, keepdims=True)
        idx = jnp.min(jnp.where(vals == top, lane, far), axis=-1, keepdims=True)
        return top, idx

    group_logits = jnp.where(lane < N_GROUPS, logits, -jnp.inf)
    g_top, g_idx = first_argmax(group_logits)
    g_weight = 1.0 / jnp.sum(jnp.exp(group_logits - g_top), axis=-1, keepdims=True)
    first = N_GROUPS + EXPERTS_PER_GROUP * g_idx
    in_group = (lane >= first) & (lane < first + EXPERTS_PER_GROUP)
    expert_logits = jnp.where(in_group, logits, -jnp.inf)
    top1, idx1 = first_argmax(expert_logits)
    top2, idx2 = first_argmax(jnp.where(lane == idx1, -jnp.inf, expert_logits))
    w1 = 1.0 / (1.0 + jnp.exp(top2 - top1))
    w2 = jnp.exp(top2 - top1) * w1
    comb_ref[...] = g_weight * (jnp.where(lane == idx1, w1, 0.0) + jnp.where(lane == idx2, w2, 0.0))


def out_projection_router(merged, w_out, layer, x, g_ffn, w_router, b_router):
    t, d = x.shape
    tm = _row_tile(t, 256)
    return pl.pallas_call(
        _outproj_router_kernel,
        out_shape=(jax.ShapeDtypeStruct((t, d), F32), jax.ShapeDtypeStruct((t, d), BF16),
                   jax.ShapeDtypeStruct((t, LANES), F32)),
        grid=(t // tm,),
        in_specs=[pl.BlockSpec((tm, d), lambda m: (m, 0)),
                  pl.BlockSpec((1, d, d), lambda m: (layer, 0, 0)),
                  pl.BlockSpec((tm, d), lambda m: (m, 0)),
                  pl.BlockSpec((1, d), lambda m: (0, 0)),
                  pl.BlockSpec((d, LANES), lambda m: (0, 0)),
                  pl.BlockSpec((1, LANES), lambda m: (0, 0))],
        out_specs=(pl.BlockSpec((tm, d), lambda m: (m, 0)),
                   pl.BlockSpec((tm, d), lambda m: (m, 0)),
                   pl.BlockSpec((tm, LANES), lambda m: (m, 0))),
        compiler_params=_cparams(("parallel",)),
        name="out_projection_router",
    )(merged, w_out, x, g_ffn.reshape(1, d), w_router, b_router)


EXPERTS_PER_STEP = 2


def _moe_kernel(x_ref, comb_ref, wg_ref, wu_ref, wd_ref, h_ref, o_ref):
    step = pl.program_id(1)

    @pl.when(step == 0)
    def _():
        o_ref[...] = h_ref[...]

    x = x_ref[...]
    comb = comb_ref[...]
    lane = lax.broadcasted_iota(jnp.int32, comb.shape, 1)
    total = None
    for i in range(EXPERTS_PER_STEP):
        e = step * EXPERTS_PER_STEP + i
        weight = jnp.sum(jnp.where(lane == N_GROUPS + e, comb, 0.0), axis=-1, keepdims=True)
        gate = _dot(x, wg_ref[0, i])
        hid = gate * _sigmoid(gate) * _dot(x, wu_ref[0, i]) * weight
        y = _dot(hid.astype(BF16), wd_ref[0, i])
        total = y if total is None else total + y
    o_ref[...] += total


def moe(xn, comb, w_gate, w_up, w_down, layer, h):
    t, d = xn.shape
    _, n_exp, _, f = w_gate.shape
    tm = _row_tile(t, 512)
    return pl.pallas_call(
        _moe_kernel,
        out_shape=jax.ShapeDtypeStruct((t, d), F32),
        grid=(t // tm, n_exp // EXPERTS_PER_STEP),
        in_specs=[pl.BlockSpec((tm, d), lambda m, e: (m, 0)),
                  pl.BlockSpec((tm, LANES), lambda m, e: (m, 0)),
                  pl.BlockSpec((1, EXPERTS_PER_STEP, d, f), lambda m, e: (layer, e, 0, 0)),
                  pl.BlockSpec((1, EXPERTS_PER_STEP, d, f), lambda m, e: (layer, e, 0, 0)),
                  pl.BlockSpec((1, EXPERTS_PER_STEP, f, d), lambda m, e: (layer, e, 0, 0)),
                  pl.BlockSpec((tm, d), lambda m, e: (m, 0))],
        out_specs=pl.BlockSpec((tm, d), lambda m, e: (m, 0)),
        compiler_params=_cparams(("parallel", "arbitrary")),
        name="moe",
    )(xn, comb, w_gate, w_up, w_down, h)


def _rope_tables(pos, width):
    half = HEAD_DIM // 2
    inv_freq = jnp.exp(jnp.arange(half, dtype=F32) * (-2.0 * math.log(ROPE_THETA) / HEAD_DIM))
    ang = pos.astype(F32)[:, None] * inv_freq[None, :]
    reps = width // half
    cos = jnp.tile(jnp.cos(ang), (1, reps))
    sin = jnp.tile(jnp.sin(ang), (1, reps))
    lower = (jnp.arange(width) % HEAD_DIM) < half
    return cos, jnp.where(lower[None, :], -sin, sin)


def _rel_bias_table(rel_bias, qpos, kpos):
    rel = jnp.clip(qpos[:, None] - kpos[None, :], -REL_CLIP, REL_CLIP) + REL_CLIP
    return rel_bias[:, rel].astype(F32)


def _band_bias_table(rel_bias, tile):
    period = 4 * tile + 1
    m = jnp.arange(period)
    diff = jnp.where(m < 3 * tile, m, m - period)
    vec = rel_bias[:, jnp.clip(LEFT_CHUNKS * CHUNK - diff, -REL_CLIP, REL_CLIP) + REL_CLIP].astype(F32)
    flat = jnp.tile(vec, (1, tile))[:, :tile * (period - 1)]
    return flat.reshape(vec.shape[0], tile, period - 1)[:, :, :3 * tile]


def _layer_weights(l, p):
    n_qkv = N_SEG * 512
    w_in = p["w_in"][l]
    n_fg = w_in.shape[1] - n_qkv - N_BRANCH * w_in.shape[0]
    w_f = jnp.pad(p["w_in_t"][l, n_qkv:n_qkv + n_fg], ((0, LANES - n_fg), (0, 0)))
    b_f = jnp.pad(p["b_forget"][l], (0, LANES - n_fg)).reshape(1, LANES)
    w_router = jnp.pad(jnp.concatenate([p["router_group_w"][l], p["router_expert_w"][l]], axis=1),
                       ((0, 0), (0, LANES - N_GROUPS - N_EXPERTS)))
    b_router = jnp.pad(jnp.concatenate([p["router_group_b"][l], p["router_expert_b"][l]]),
                       (0, LANES - N_GROUPS - N_EXPERTS)).reshape(1, LANES)
    return dict(
        w_f=w_f, b_f=b_f, n_fg=n_fg,
        gate_row0=n_qkv + n_fg, w_router=w_router, b_router=b_router,
    )


def _project_and_ffn(merged, x, l, p, lw):
    h, xn2, comb = out_projection_router(merged, p["w_out_bf"], l, x, p["norm_ffn"][l], lw["w_router"], lw["b_router"])
    return moe(xn2, comb, p["e_gate"], p["e_up"], p["e_down"], l, h)


def _prompt_layer(x, l, p, lw, lam_init, batch, seq, tables, state_bufs):
    t, d = x.shape
    xn = rmsnorm(x, p["norm_mix"][l], BF16)
    qkv_bf, state_bufs = in_projection_prompt(xn, p["w_in_t"], l, tables["cos"], tables["sin"], batch, seq, state_bufs)
    width = qkv_bf.shape[-1]
    n_fg = lw["n_fg"]
    logf = log_forget(xn, lw["w_f"], lw["b_f"])[:, :n_fg].reshape(batch, seq, n_fg)
    cum_row = cumsum_lanes(jnp.swapaxes(logf, 1, 2))
    cum_col = jnp.swapaxes(cum_row.reshape(batch, n_fg // 2, 2, seq), 2, 3)
    branches = prompt_attention(
        qkv_bf.reshape(N_SEG, batch, seq, width), cum_col, tables["band_bias"][l],
        p["diff_lambda"][l], p["diff_subln"][l], lam_init, batch, seq)
    merged = gated_merge(xn, p["w_in_t"], lw["gate_row0"], p["w_branch"], l, [b.reshape(t, width) for b in branches])
    h = _project_and_ffn(merged, x, l, p, lw)
    return h, state_bufs, logf


def _step_layer(x, l, p, lw, lam_init, batch, frames, caches, tables):
    t, d = x.shape
    xn = rmsnorm(x, p["norm_mix"][l], BF16)
    qkv, _ = in_projection(xn, p["w_in_t"], l, tables["cos"], tables["sin"], frames)
    width = qkv.shape[-1]
    n_fg = lw["n_fg"]
    logf = log_forget(xn, lw["w_f"], lw["b_f"])[:, :n_fg].reshape(batch, frames, n_fg)
    past_f = caches["c_logf"][l]
    past = past_f.shape[1]
    total = past + frames
    padded = -(-total // LANES) * LANES
    series = jnp.concatenate([past_f, logf], axis=1)
    cum = cumsum_lanes(jnp.pad(jnp.swapaxes(series, 1, 2), ((0, 0), (0, 0), (0, padded - total))))
    cum_cache_row = cum[:, :, :past]
    cum_new_row = cum[:, :, past:total]
    cum_q_col = cum_new_row.reshape(batch, n_fg * frames, 1)
    heads = lambda i: qkv[i].reshape(batch, frames, N_STEP_HEADS, HEAD_DIM)
    q_heads = lambda i: jnp.transpose(heads(i), (0, 2, 1, 3))
    seq_minor = lambda i: jnp.transpose(heads(i), (0, 2, 3, 1))
    new = dict(a=(q_heads(0), seq_minor(1), qkv[2].reshape(batch, frames, 4, LANES)),
               b=(q_heads(3), seq_minor(4), seq_minor(5)),
               c=(q_heads(6), seq_minor(7), seq_minor(8)),
               d=(q_heads(9), seq_minor(10), seq_minor(11)))
    branches = step_attention(
        l, new, caches, cum_q_col, cum_cache_row, cum_new_row,
        tables["bias_cache"][l], tables["bias_new"][l], p["diff_lambda"][l], p["diff_subln"][l], lam_init)
    merged = gated_merge(xn, p["w_in_t"], lw["gate_row0"], p["w_branch"], l, [b.reshape(t, width) for b in branches])
    h = _project_and_ffn(merged, x, l, p, lw)
    seg = lambda i: qkv[i].reshape(batch, frames, width)
    state = (seg(1), seg(2), seg(4), seg(5), seg(7), seg(8), logf, seg(10), seg(11))
    return h, state


def kernel(x_prompt, x_sample, cache_a_k, cache_a_v, cache_b_k, cache_b_v, cache_c_k, cache_c_v, cache_c_logf, cache_d_k, cache_d_v, norm_mix, w_in, b_forget, diff_lambda, diff_subln, rel_bias, w_branch, w_out, norm_ffn, router_group_w, router_group_b, router_expert_w, router_expert_b, expert_w_gate, expert_w_up, expert_w_down, norm_final):
    p = dict(norm_mix=norm_mix, w_in=w_in, w_in_t=jnp.swapaxes(w_in, 1, 2), b_forget=b_forget, diff_lambda=diff_lambda, diff_subln=diff_subln,
             w_branch=w_branch, w_out=w_out, norm_ffn=norm_ffn, router_group_w=router_group_w,
             router_group_b=router_group_b, router_expert_w=router_expert_w, router_expert_b=router_expert_b,
             w_out_bf=w_out.astype(BF16), e_gate=expert_w_gate.astype(BF16), e_up=expert_w_up.astype(BF16),
             e_down=expert_w_down.astype(BF16))
    depth = w_in.shape[0]
    bp, sp, d = x_prompt.shape
    bs, fs, _ = x_sample.shape
    past = cache_a_k.shape[2]
    width = 512
    n_heads = width // HEAD_DIM

    def seq_minor(c):
        c = jnp.moveaxis(c, 2, -1)
        return c.reshape(c.shape[0], c.shape[1], N_STEP_HEADS, HEAD_DIM, c.shape[-1])

    caches = dict(a_k=seq_minor(cache_a_k), a_v=cache_a_v, b_k=seq_minor(cache_b_k), b_v=seq_minor(cache_b_v),
                  c_k=seq_minor(cache_c_k), c_v=seq_minor(cache_c_v), c_logf=cache_c_logf,
                  d_k=seq_minor(cache_d_k), d_v=seq_minor(cache_d_v))

    pos_p = jnp.arange(sp, dtype=jnp.int32)
    pos_s = past + jnp.arange(fs, dtype=jnp.int32)
    cos_p, sin_p = _rope_tables(pos_p, width)
    cos_s, sin_s = _rope_tables(pos_s, width)
    tile = 256
    band_bias = jnp.stack([_band_bias_table(rel_bias[l], tile).reshape(n_heads // 2, 2, tile, 3 * tile)
                           for l in range(depth)])
    wb = cache_b_k.shape[2]
    kpos_b = jnp.arange(past - wb, past + fs, dtype=jnp.int32)
    step_bias = jnp.stack([_rel_bias_table(rel_bias[l], pos_s, kpos_b).reshape(n_heads * fs, wb + fs)
                           for l in range(depth)])
    tables_p = dict(cos=cos_p, sin=sin_p, band_bias=band_bias)
    tables_s = dict(cos=cos_s, sin=sin_s, bias_cache=step_bias[:, :, :wb], bias_new=step_bias[:, :, wb:])

    hp = x_prompt.reshape(bp * sp, d)
    hs = x_sample.reshape(bs * fs, d)
    bufs_p, logf_p, st_s = prompt_state_buffers(depth, bp, sp), [], []
    for l in range(depth):
        lam_init = 0.8 - 0.6 * math.exp(-0.3 * l)
        lw = _layer_weights(l, p)
        hp, bufs_p, logf = _prompt_layer(hp, l, p, lw, lam_init, bp, sp, tables_p, bufs_p)
        hs, state_s = _step_layer(hs, l, p, lw, lam_init, bs, fs, caches, tables_s)
        logf_p.append(logf)
        st_s.append(state_s)
    y_prompt = rmsnorm(hp, norm_final, F32).reshape(bp, sp, d)
    y_sample = rmsnorm(hs, norm_final, F32).reshape(bs, fs, d)

    h_a = width // (2 * HEAD_DIM)
    state_shapes = lambda b, s, sb: (
        (b, s, h_a, 2, HEAD_DIM), (b, s, h_a, 2 * HEAD_DIM), (b, sb, n_heads, HEAD_DIM), (b, sb, n_heads, HEAD_DIM),
        (b, s, n_heads, HEAD_DIM), (b, s, n_heads, HEAD_DIM), (b, s, n_heads), (b, s, n_heads, HEAD_DIM),
        (b, s, n_heads, HEAD_DIM))

    def stack(states, shapes):
        return tuple(jnp.stack([st[i].reshape(shapes[i]) for st in states], axis=0) for i in range(len(shapes)))

    token_major = lambda buf, dims: jnp.moveaxis(buf.reshape(buf.shape[:2] + dims + buf.shape[-1:]), -1, 2)
    a_k, a_v, b_k, b_v, c_k, c_v, d_k, d_v = bufs_p
    per_head = (n_heads, HEAD_DIM)
    out_p = (token_major(a_k, (h_a, 2, HEAD_DIM)), a_v, token_major(b_k, per_head), token_major(b_v, per_head),
             token_major(c_k, per_head), token_major(c_v, per_head), jnp.stack(logf_p, axis=0),
             token_major(d_k, per_head), token_major(d_v, per_head))
    out_s = stack(st_s, state_shapes(bs, fs, fs))
    return (y_prompt, y_sample) + out_p + out_s
```

```python
import functools
import math

import jax
import jax.numpy as jnp
from jax import lax
from jax.experimental import pallas as pl
from jax.experimental.pallas import tpu as pltpu

F32 = jnp.float32
BF16 = jnp.bfloat16

CHUNK = 64
HEAD_DIM = 64
LEFT_CHUNKS = 8
REL_CLIP = 128
ROPE_THETA = 10000.0
N_BRANCH = 4
N_GROUPS = 4
EXPERTS_PER_GROUP = 4
N_EXPERTS = N_GROUPS * EXPERTS_PER_GROUP
RMS_EPS = 1e-6
NEG_INF = -1e30
Q_SCALE = HEAD_DIM ** -0.5
N_SEG = 12

LANES = 128
VMEM_LIMIT = 56 * 1024 * 1024

NT_DIMS = (((1,), (1,)), ((), ()))


def _cparams(sem):
    return pltpu.CompilerParams(dimension_semantics=sem, vmem_limit_bytes=VMEM_LIMIT)


def _row_tile(t, pref):
    return pref if t % pref == 0 else t


def _sigmoid(z):
    return 1.0 / (1.0 + jnp.exp(-z))


def _dot(a, b):
    return jnp.dot(a, b, preferred_element_type=F32)


def _dot_nt(a, b):
    return lax.dot_general(a, b, NT_DIMS, preferred_element_type=F32)


def _rms_kernel(x_ref, g_ref, o_ref):
    x = x_ref[...]
    ms = jnp.mean(x * x, axis=-1, keepdims=True)
    o_ref[...] = (x * lax.rsqrt(ms + RMS_EPS) * g_ref[...]).astype(o_ref.dtype)


def rmsnorm(x, g, out_dtype):
    t, d = x.shape
    tm = _row_tile(t, 512)
    return pl.pallas_call(
        _rms_kernel,
        out_shape=jax.ShapeDtypeStruct((t, d), out_dtype),
        grid=(t // tm,),
        in_specs=[pl.BlockSpec((tm, d), lambda m: (m, 0)),
                  pl.BlockSpec((1, d), lambda m: (0, 0))],
        out_specs=pl.BlockSpec((tm, d), lambda m: (m, 0)),
        compiler_params=_cparams(("parallel",)),
        name="rmsnorm",
    )(x, g.reshape(1, d))


def _inproj_kernel(x_ref, w_ref, cos_ref, sin_ref, o_ref, ob_ref, wb_ref):
    n = pl.program_id(0)

    @pl.when(pl.program_id(1) == 0)
    def _():
        wb_ref[...] = w_ref[0].astype(BF16)

    acc = _dot_nt(x_ref[...], wb_ref[...])
    q_scale = jnp.where(n % 3 == 0, Q_SCALE, 1.0)

    @pl.when(n < 2)
    def _():
        width = acc.shape[1]
        half = HEAD_DIM // 2
        lane = lax.broadcasted_iota(jnp.int32, acc.shape, 1)
        lower = (lane % HEAD_DIM) < half
        partner = jnp.where(lower, pltpu.roll(acc, width - half, 1), pltpu.roll(acc, half, 1))
        roped = acc * cos_ref[...] + partner * sin_ref[...]
        o_ref[0] = roped
        ob_ref[0] = (roped * q_scale).astype(ob_ref.dtype)

    @pl.when(n >= 2)
    def _():
        o_ref[0] = acc
        ob_ref[0] = (acc * q_scale).astype(ob_ref.dtype)


def in_projection(xn, w_in_t, layer, cos_tab, sin_tab, seq):
    t, d = xn.shape
    w = 512
    tm = _row_tile(t, 512)
    if seq % tm == 0:
        n_tab = seq // tm
    else:
        assert tm % seq == 0
        cos_tab = jnp.tile(cos_tab, (tm // seq, 1))
        sin_tab = jnp.tile(sin_tab, (tm // seq, 1))
        n_tab = 1
    return pl.pallas_call(
        _inproj_kernel,
        out_shape=(jax.ShapeDtypeStruct((N_SEG, t, w), F32), jax.ShapeDtypeStruct((N_SEG, t, w), BF16)),
        grid=(N_SEG, t // tm),
        in_specs=[pl.BlockSpec((tm, d), lambda n, m: (m, 0)),
                  pl.BlockSpec((1, w, d), lambda n, m: (layer, n, 0)),
                  pl.BlockSpec((tm, w), lambda n, m: (m % n_tab, 0)),
                  pl.BlockSpec((tm, w), lambda n, m: (m % n_tab, 0))],
        out_specs=(pl.BlockSpec((1, tm, w), lambda n, m: (n, m, 0)),
                   pl.BlockSpec((1, tm, w), lambda n, m: (n, m, 0))),
        scratch_shapes=[pltpu.VMEM((w, d), BF16)],
        compiler_params=_cparams(("parallel", "arbitrary")),
        name="in_projection",
    )(xn, w_in_t, cos_tab, sin_tab)


STATE_SEGS = (1, 2, 4, 5, 7, 8, 10, 11)
VALUE_A_SEG = 2
BAND_SEGS = (4, 5)


def _inproj_prompt_kernel(*refs, n_alias, tiles_per_seq):
    x_ref, w_ref, cos_ref, sin_ref = refs[:4]
    ob_ref = refs[4 + n_alias]
    st_refs = refs[5 + n_alias:5 + n_alias + len(STATE_SEGS)]
    wb_ref = refs[-1]
    n, m = pl.program_id(0), pl.program_id(1)

    @pl.when(m == 0)
    def _():
        wb_ref[...] = w_ref[0].astype(BF16)

    acc = _dot_nt(x_ref[...], wb_ref[...])
    q_scale = jnp.where(n % 3 == 0, Q_SCALE, 1.0)

    def emit(val):
        ob_ref[0] = (val * q_scale).astype(ob_ref.dtype)
        for seg, st_ref in zip(STATE_SEGS, st_refs):
            if seg == VALUE_A_SEG:
                @pl.when(n == seg)
                def _(st_ref=st_ref):
                    for h in range(st_ref.shape[3]):
                        st_ref[0, 0, :, h, :] = val[:, h * LANES:(h + 1) * LANES]
            elif seg in BAND_SEGS:
                @pl.when(jnp.logical_and(n == seg, m % tiles_per_seq == tiles_per_seq - 1))
                def _(st_ref=st_ref):
                    st_ref[0, 0] = val.T
            else:
                @pl.when(n == seg)
                def _(st_ref=st_ref):
                    st_ref[0, 0] = val.T

    @pl.when(n < 2)
    def _():
        width = acc.shape[1]
        half = HEAD_DIM // 2
        lane = lax.broadcasted_iota(jnp.int32, acc.shape, 1)
        lower = (lane % HEAD_DIM) < half
        partner = jnp.where(lower, pltpu.roll(acc, width - half, 1), pltpu.roll(acc, half, 1))
        emit(acc * cos_ref[...] + partner * sin_ref[...])

    @pl.when(n >= 2)
    def _():
        emit(acc)


def prompt_state_buffers(depth, batch, seq, w=512):
    band = LEFT_CHUNKS * CHUNK
    shapes = []
    for seg in STATE_SEGS:
        if seg == VALUE_A_SEG:
            shapes.append((depth, batch, seq, w // LANES, LANES))
        elif seg in BAND_SEGS:
            shapes.append((depth, batch, w, band))
        else:
            shapes.append((depth, batch, w, seq))
    return tuple(jnp.zeros(s, F32) for s in shapes)


def in_projection_prompt(xn, w_in_t, layer, cos_tab, sin_tab, batch, seq, state_bufs):
    t, d = xn.shape
    w = 512
    tm = 512
    band = LEFT_CHUNKS * CHUNK
    assert seq % tm == 0 and band == tm and t == batch * seq
    per_seq = seq // tm
    n_tiles = t // tm

    def parked(seg, n, m):
        return jnp.where(n < seg, 0, jnp.where(n > seg, n_tiles - 1, m))

    specs = []
    for seg in STATE_SEGS:
        if seg == VALUE_A_SEG:
            specs.append(pl.BlockSpec((1, 1, tm, w // LANES, LANES), functools.partial(
                lambda n, m, seg: (layer, parked(seg, n, m) // per_seq, parked(seg, n, m) % per_seq, 0, 0), seg=seg)))
        elif seg in BAND_SEGS:
            specs.append(pl.BlockSpec((1, 1, w, band), functools.partial(
                lambda n, m, seg: (layer, parked(seg, n, m) // per_seq, 0, 0), seg=seg)))
        else:
            specs.append(pl.BlockSpec((1, 1, w, tm), functools.partial(
                lambda n, m, seg: (layer, parked(seg, n, m) // per_seq, 0, parked(seg, n, m) % per_seq), seg=seg)))
    alias_in = list(state_bufs)
    shapes = [jax.ShapeDtypeStruct(b.shape, b.dtype) for b in alias_in]
    n_fixed = 4
    out = pl.pallas_call(
        functools.partial(_inproj_prompt_kernel, n_alias=len(alias_in), tiles_per_seq=per_seq),
        out_shape=[jax.ShapeDtypeStruct((N_SEG, t, w), BF16)] + shapes,
        grid=(N_SEG, n_tiles),
        in_specs=[pl.BlockSpec((tm, d), lambda n, m: (m, 0)),
                  pl.BlockSpec((1, w, d), lambda n, m: (layer, n, 0)),
                  pl.BlockSpec((tm, w), lambda n, m: (m % per_seq, 0)),
                  pl.BlockSpec((tm, w), lambda n, m: (m % per_seq, 0))]
                 + [pl.BlockSpec(memory_space=pl.ANY) for _ in alias_in],
        out_specs=[pl.BlockSpec((1, tm, w), lambda n, m: (n, m, 0))] + specs,
        input_output_aliases={n_fixed + i: 1 + i for i in range(len(alias_in))},
        scratch_shapes=[pltpu.VMEM((w, d), BF16)],
        compiler_params=_cparams(("arbitrary", "arbitrary")),
        name="in_projection_prompt",
    )(xn, w_in_t, cos_tab, sin_tab, *alias_in)
    return out[0], tuple(out[1:])


def _logf_kernel(x_ref, w_ref, b_ref, o_ref):
    z = _dot_nt(x_ref[...], w_ref[...].astype(BF16)) + b_ref[...]
    o_ref[...] = jnp.minimum(z, 0.0) - jnp.log1p(jnp.exp(-jnp.abs(z)))


def log_forget(xn, w_f, b_f):
    t, d = xn.shape
    tm = _row_tile(t, 512)
    return pl.pallas_call(
        _logf_kernel,
        out_shape=jax.ShapeDtypeStruct((t, LANES), F32),
        grid=(t // tm,),
        in_specs=[pl.BlockSpec((tm, d), lambda m: (m, 0)),
                  pl.BlockSpec((LANES, d), lambda m: (0, 0)),
                  pl.BlockSpec((1, LANES), lambda m: (0, 0))],
        out_specs=pl.BlockSpec((tm, LANES), lambda m: (m, 0)),
        compiler_params=_cparams(("parallel",)),
        name="log_forget",
    )(xn, w_f, b_f)


def _cumsum_kernel(x_ref, o_ref):
    x = x_ref[0]
    length = x.shape[1]
    lane = lax.broadcasted_iota(jnp.int32, x.shape, 1)
    shift = 1
    while shift < length:
        x = x + jnp.where(lane >= shift, pltpu.roll(x, shift, 1), 0.0)
        shift *= 2
    o_ref[0] = x


def cumsum_lanes(x):
    b, h, length = x.shape
    return pl.pallas_call(
        _cumsum_kernel,
        out_shape=jax.ShapeDtypeStruct(x.shape, F32),
        grid=(b,),
        in_specs=[pl.BlockSpec((1, h, length), lambda i: (i, 0, 0))],
        out_specs=pl.BlockSpec((1, h, length), lambda i: (i, 0, 0)),
        compiler_params=_cparams(("parallel",)),
        name="cumsum",
    )(x)


def _diff_lambda(lam_ref, lam_init):
    lv = lam_ref[...]
    a = jnp.sum(lv[0:1] * lv[1:2], axis=1, keepdims=True)
    b = jnp.sum(lv[2:3] * lv[3:4], axis=1, keepdims=True)
    return jnp.exp(a) - jnp.exp(b) + lam_init


def _subln(o, w, lam_init):
    ms = jnp.mean(o * o, axis=-1, keepdims=True)
    return o * lax.rsqrt(ms + RMS_EPS) * w * (1.0 - lam_init)


def _head_lanes(shape, j):
    lane = lax.broadcasted_iota(jnp.int32, shape, 1)
    return (lane >= j * HEAD_DIM) & (lane < (j + 1) * HEAD_DIM)


def _split3(x):
    hi = x.astype(BF16).astype(F32)
    mid = (x - hi).astype(BF16).astype(F32)
    lo = (x - hi - mid).astype(BF16).astype(F32)
    return hi, mid, lo


def _with_bias_columns(x, j, cols):
    lane = lax.broadcasted_iota(jnp.int32, x.shape, 1)
    out = jnp.where(_head_lanes(x.shape, j), x.astype(F32), 0.0)
    base = (1 - j) * HEAD_DIM
    for i, c in enumerate(cols):
        out = jnp.where(lane == base + i, c, out)
    return out.astype(BF16)


def _softmax_tile_update(state, s, v):
    m, l, acc = state
    m_new = jnp.maximum(m, jnp.max(s, axis=-1, keepdims=True))
    alpha = jnp.exp(m - m_new)
    p = jnp.exp(s - m_new)
    return (m_new, alpha * l + jnp.sum(p, axis=-1, keepdims=True), alpha * acc + _dot(p.astype(BF16), v))


def _causal_softmax_sweep(qs, k_rows, v_rows, diag_mask, qi, tile):
    init = tuple((jnp.full((tile, 1), NEG_INF, F32), jnp.zeros((tile, 1), F32), jnp.zeros((tile, LANES), F32))
                 for _ in qs)

    def update(states, start, size, mask):
        start = pl.multiple_of(start, tile)
        v = v_rows(start, size)
        out = []
        for j, q in enumerate(qs):
            s = _dot_nt(q, k_rows(j, start, size))
            if mask is not None:
                s = jnp.where(mask, s, NEG_INF)
            out.append(_softmax_tile_update(states[j], s, v))
        return tuple(out)

    pairs = qi // 2
    states = lax.fori_loop(0, pairs, lambda i, st: update(st, i * 2 * tile, 2 * tile, None), init)
    states = lax.fori_loop(0, qi - 2 * pairs, lambda i, st: update(st, pairs * 2 * tile, tile, None), states)
    return update(states, qi * tile, tile, diag_mask)


def _a_prompt_kernel(q_ref, k_ref, v_ref, lam_ref, w_ref, o_ref, *, tile, lam_init):
    qi = pl.program_id(2)
    q = q_ref[0, 0]
    qs = [jnp.where(_head_lanes(q.shape, j), q, jnp.zeros_like(q)) for j in range(2)]
    row = lax.broadcasted_iota(jnp.int32, (tile, tile), 0)
    col = lax.broadcasted_iota(jnp.int32, (tile, tile), 1)
    diag_mask = (col // CHUNK) <= (row // CHUNK)
    k_rows = lambda j, start, size: k_ref[0, 0, pl.ds(start, size), :]
    v_rows = lambda start, size: v_ref[0, 0, pl.ds(start, size), :]
    (_, l1, acc1), (_, l2, acc2) = _causal_softmax_sweep(qs, k_rows, v_rows, diag_mask, qi, tile)
    lam = _diff_lambda(lam_ref, lam_init)
    o = acc1 / l1 - lam * (acc2 / l2)
    o_ref[0] = _subln(o, w_ref[...], lam_init).astype(o_ref.dtype)


def _c_prompt_kernel(q_ref, k_ref, v_ref, cq_ref, ck_ref, o_ref, kb_ref, *, tile):
    qi = pl.program_id(2)
    one = jnp.float32(1.0)

    @pl.when(qi == 0)
    def _():
        k = k_ref[0, 0]
        ck = ck_ref[0, 0]
        for j in range(2):
            hi, mid, lo = _split3(-ck[:, j:j + 1])
            kb_ref[j] = _with_bias_columns(k, j, [one, one, one, hi, mid, lo])

    q = q_ref[0, 0]
    cq = cq_ref[0, 0]
    qs = []
    for j in range(2):
        hi, mid, lo = _split3(cq[:, j:j + 1])
        qs.append(_with_bias_columns(q, j, [hi, mid, lo, one, one, one]))
    row = lax.broadcasted_iota(jnp.int32, (tile, tile), 0)
    col = lax.broadcasted_iota(jnp.int32, (tile, tile), 1)
    k_rows = lambda j, start, size: kb_ref[j, pl.ds(start, size), :]
    v_rows = lambda start, size: v_ref[0, 0, pl.ds(start, size), :]
    (_, l1, acc1), (_, l2, acc2) = _causal_softmax_sweep(qs, k_rows, v_rows, col <= row, qi, tile)
    o_ref[0] = jnp.where(_head_lanes(acc1.shape, 0), acc1 / l1, acc2 / l2).astype(o_ref.dtype)


SUFFIX_BLOCK = 256


def _strict_upper_ones(n):
    r = lax.broadcasted_iota(jnp.int32, (n, n), 0)
    c = lax.broadcasted_iota(jnp.int32, (n, n), 1)
    return jnp.where(r > c, 1.0, 0.0).astype(BF16)


def _suffix_sums(lm, ones_after, two_pass):
    hi = lm.astype(BF16)
    out = _dot(hi, ones_after)
    if two_pass:
        out = out + _dot((lm - hi.astype(F32)).astype(BF16), ones_after)
    return out


def _stick_weights(z, mask, run, ones_after, two_pass=True):
    sp = jnp.maximum(z, 0.0) + jnp.log(1.0 + jnp.exp(-jnp.abs(z)))
    lm = -sp if mask is None else jnp.where(mask, -sp, 0.0)
    block = ones_after.shape[0]
    n_block = z.shape[1] // block
    later = [None] * n_block
    for c in reversed(range(n_block)):
        part = lm[:, c * block:(c + 1) * block]
        later[c] = _suffix_sums(part, ones_after, two_pass) + run
        run = run + jnp.sum(part, axis=-1, keepdims=True)
    later = later[0] if n_block == 1 else jnp.concatenate(later, axis=1)
    a = jnp.exp((z - sp) + later)
    if mask is not None:
        a = jnp.where(mask, a, 0.0)
    return a, run


def _d_prompt_kernel(q_ref, k_ref, v_ref, o_ref, *, tile):
    qi = pl.program_id(2)
    q = q_ref[0, 0]
    qs = [jnp.where(_head_lanes(q.shape, j), q, jnp.zeros_like(q)) for j in range(2)]
    ones_after = _strict_upper_ones(SUFFIX_BLOCK)
    row = lax.broadcasted_iota(jnp.int32, (tile, tile), 0)
    col = lax.broadcasted_iota(jnp.int32, (tile, tile), 1)

    def update(states, start, size, mask):
        start = pl.multiple_of(start, tile)
        k = k_ref[0, 0, pl.ds(start, size), :]
        v = v_ref[0, 0, pl.ds(start, size), :]
        out = []
        for j, q_j in enumerate(qs):
            run, acc = states[j]
            a, run = _stick_weights(_dot_nt(q_j, k), mask, run, ones_after, two_pass=False)
            out.append((run, acc + _dot(a.astype(BF16), v)))
        return tuple(out)

    init = tuple((jnp.zeros((tile, 1), F32), jnp.zeros((tile, LANES), F32)) for _ in qs)
    states = update(init, qi * tile, tile, col < row)
    pairs = qi // 2
    states = lax.fori_loop(0, pairs, lambda i, st: update(st, (qi - 2 - 2 * i) * tile, 2 * tile, None), states)
    states = lax.fori_loop(0, qi - 2 * pairs, lambda i, st: update(st, 0, tile, None), states)
    o_ref[0] = jnp.where(_head_lanes((tile, LANES), 0), states[0][1], states[1][1]).astype(o_ref.dtype)


def _b_prompt_kernel(q_ref, k0_ref, k1_ref, k2_ref, v0_ref, v1_ref, v2_ref, bias_ref, o_ref, *, tile):
    qi = pl.program_id(2)
    q = q_ref[0, 0]
    row_chunk = lax.broadcasted_iota(jnp.int32, (tile, tile), 0) // CHUNK
    col_chunk = lax.broadcasted_iota(jnp.int32, (tile, tile), 1) // CHUNK
    back = 2
    ks = [r[0, 0] for r in (k0_ref, k1_ref, k2_ref)]
    vs = [r[0, 0] for r in (v0_ref, v1_ref, v2_ref)]
    valids = []
    for c in range(3):
        dist = row_chunk - col_chunk + (back - c) * (tile // CHUNK)
        valids.append((dist >= 0) & (dist <= LEFT_CHUNKS) & (qi - back + c >= 0))
    outs = []
    for j in range(2):
        q_j = jnp.where(_head_lanes(q.shape, j), q, jnp.zeros_like(q))
        scores = []
        for c in range(3):
            s = _dot_nt(q_j, ks[c]) + bias_ref[0, j, :, c * tile:(c + 1) * tile]
            scores.append(jnp.where(valids[c], s, NEG_INF))
        m = jnp.maximum(jnp.maximum(jnp.max(scores[0], axis=-1, keepdims=True),
                                    jnp.max(scores[1], axis=-1, keepdims=True)),
                        jnp.max(scores[2], axis=-1, keepdims=True))
        ps = [jnp.exp(s - m) for s in scores]
        denom = sum(jnp.sum(p, axis=-1, keepdims=True) for p in ps)
        o = sum(_dot(p.astype(BF16), v) for p, v in zip(ps, vs))
        outs.append(o / denom)
    o_ref[0] = jnp.where(_head_lanes(outs[0].shape, 0), outs[0], outs[1]).astype(o_ref.dtype)


def prompt_attention(qkv, cum_col, band_bias, lam_vec, subln_w, lam_init, batch, seq):
    width = qkv.shape[-1]
    n_pair = width // LANES
    out_shape = jax.ShapeDtypeStruct((batch, seq, width), BF16)
    tile = 512
    assert seq % tile == 0 and tile % CHUNK == 0
    sem = ("parallel", "parallel", "arbitrary")

    q_spec = lambda seg: pl.BlockSpec((1, 1, tile, LANES), lambda b, h, qi: (seg, b, qi, h))
    full_spec = lambda seg: pl.BlockSpec((1, 1, seq, LANES), lambda b, h, qi: (seg, b, 0, h))
    out_spec = pl.BlockSpec((1, tile, LANES), lambda b, h, qi: (b, qi, h))
    grid = (batch, n_pair, seq // tile)

    o_a = pl.pallas_call(
        functools.partial(_a_prompt_kernel, tile=tile, lam_init=lam_init),
        out_shape=out_shape, grid=grid,
        in_specs=[q_spec(0), full_spec(1), full_spec(2),
                  pl.BlockSpec((4, HEAD_DIM), lambda b, h, qi: (0, 0)),
                  pl.BlockSpec((1, LANES), lambda b, h, qi: (0, 0))],
        out_specs=out_spec,
        compiler_params=_cparams(sem),
        name="diff_attention",
    )(qkv, qkv, qkv, lam_vec, subln_w.reshape(1, LANES))

    o_c = pl.pallas_call(
        functools.partial(_c_prompt_kernel, tile=tile),
        out_shape=out_shape, grid=grid,
        in_specs=[q_spec(6), full_spec(7), full_spec(8),
                  pl.BlockSpec((1, 1, tile, 2), lambda b, h, qi: (b, h, qi, 0)),
                  pl.BlockSpec((1, 1, seq, 2), lambda b, h, qi: (b, h, 0, 0))],
        out_specs=out_spec,
        scratch_shapes=[pltpu.VMEM((2, seq, LANES), BF16)],
        compiler_params=_cparams(sem),
        name="forgetting_attention",
    )(qkv, qkv, qkv, cum_col, cum_col)

    o_d = pl.pallas_call(
        functools.partial(_d_prompt_kernel, tile=tile),
        out_shape=out_shape, grid=grid,
        in_specs=[q_spec(9), full_spec(10), full_spec(11)],
        out_specs=out_spec,
        compiler_params=_cparams(sem),
        name="stick_breaking_attention",
    )(qkv, qkv, qkv)

    band_tile = 256
    assert seq % band_tile == 0 and band_tile * 2 == LEFT_CHUNKS * CHUNK

    def band_spec(seg, back):
        return pl.BlockSpec((1, 1, band_tile, LANES), lambda b, h, qi: (seg, b, jnp.maximum(qi - back, 0), h))

    o_b = pl.pallas_call(
        functools.partial(_b_prompt_kernel, tile=band_tile),
        out_shape=out_shape,
        grid=(batch, n_pair, seq // band_tile),
        in_specs=[band_spec(3, 0),
                  band_spec(4, 2), band_spec(4, 1), band_spec(4, 0),
                  band_spec(5, 2), band_spec(5, 1), band_spec(5, 0),
                  pl.BlockSpec((1, 2, band_tile, 3 * band_tile), lambda b, h, qi: (h, 0, 0, 0))],
        out_specs=pl.BlockSpec((1, band_tile, LANES), lambda b, h, qi: (b, qi, h)),
        compiler_params=_cparams(("parallel", "parallel", "parallel")),
        name="band_attention",
    )(qkv, qkv, qkv, qkv, qkv, qkv, qkv, band_bias)
    return o_a, o_b, o_c, o_d


N_STEP_HEADS = 8


def _step_scores(qs_ref, kt_ref):
    rows = [_dot(qs_ref[h], kt_ref[0, 0, h].astype(BF16)) for h in range(N_STEP_HEADS)]
    return jnp.concatenate(rows, axis=0)


def _step_pv(p, v_ref, frames, v_token_major):
    pb = p.astype(BF16)
    if v_token_major:
        rows = [_dot(pb[2 * g * frames:(2 * g + 2) * frames], v_ref[0, 0, :, g, :].astype(BF16))
                for g in range(N_STEP_HEADS // 2)]
    else:
        rows = [_dot_nt(pb[h * frames:(h + 1) * frames], v_ref[0, 0, h].astype(BF16))
                for h in range(N_STEP_HEADS)]
    return jnp.concatenate(rows, axis=0)


def _store_scaled_queries(q_ref, qs_ref):
    for h in range(N_STEP_HEADS):
        qs_ref[h] = (q_ref[0, h] * Q_SCALE).astype(BF16)


def _rows_per_head(x, frames):
    return jnp.concatenate([jnp.broadcast_to(x[h:h + 1, :], (frames, x.shape[1]))
                            for h in range(x.shape[0])], axis=0)


def _new_rows_mask(frames, n_rows, strict):
    frame = lax.broadcasted_iota(jnp.int32, (n_rows, frames), 0) % frames
    key = lax.broadcasted_iota(jnp.int32, (n_rows, frames), 1)
    return key < frame if strict else key <= frame


def _heads_to_lanes(acc, frames):
    return jnp.concatenate([acc[h * frames:(h + 1) * frames] for h in range(N_STEP_HEADS)], axis=1)


def _softmax_step_kernel(*refs, mode, frames, lam_init):
    if mode == "A":
        (q_ref, kc_ref, vc_ref, kn_ref, vn_ref, lam_ref, w_ref,
         o_ref, qs_ref, m_ref, l_ref, acc_ref) = refs
    elif mode == "B":
        (q_ref, kc_ref, vc_ref, kn_ref, vn_ref, bias_c_ref, bias_n_ref,
         o_ref, qs_ref, m_ref, l_ref, acc_ref) = refs
    else:
        (q_ref, kc_ref, vc_ref, kn_ref, vn_ref, cq_ref, ckc_ref, ckn_ref,
         o_ref, qs_ref, m_ref, l_ref, acc_ref) = refs
    n_rows = N_STEP_HEADS * frames
    token_major_v = mode == "A"
    kt = pl.program_id(1)

    @pl.when(kt == 0)
    def _():
        _store_scaled_queries(q_ref, qs_ref)
        m_ref[...] = jnp.full(m_ref.shape, NEG_INF, F32)
        l_ref[...] = jnp.zeros(l_ref.shape, F32)
        acc_ref[...] = jnp.zeros(acc_ref.shape, F32)

    def softmax_tile(s, v_ref):
        m_prev = m_ref[...]
        m_new = jnp.maximum(m_prev, jnp.max(s, axis=-1, keepdims=True))
        alpha = jnp.exp(m_prev - m_new)
        p = jnp.exp(s - m_new)
        l_ref[...] = alpha * l_ref[...] + jnp.sum(p, axis=-1, keepdims=True)
        acc_ref[...] = alpha * acc_ref[...] + _step_pv(p, v_ref, frames, token_major_v)
        m_ref[...] = m_new

    s = _step_scores(qs_ref, kc_ref)
    if mode == "B":
        s = s + bias_c_ref[...]
    elif mode == "C":
        s = s + cq_ref[0] - _rows_per_head(ckc_ref[0], frames)
    softmax_tile(s, vc_ref)

    @pl.when(kt == pl.num_programs(1) - 1)
    def _():
        s = _step_scores(qs_ref, kn_ref)
        if mode == "B":
            s = s + bias_n_ref[...]
        elif mode == "C":
            s = s + cq_ref[0] - _rows_per_head(ckn_ref[0], frames)
            s = jnp.where(_new_rows_mask(frames, n_rows, strict=False), s, NEG_INF)
        softmax_tile(s, vn_ref)
        o = acc_ref[...] / l_ref[...]
        if mode == "A":
            lam = _diff_lambda(lam_ref, lam_init)
            pieces = []
            for h in range(N_STEP_HEADS // 2):
                o1 = o[(2 * h) * frames:(2 * h + 1) * frames]
                o2 = o[(2 * h + 1) * frames:(2 * h + 2) * frames]
                pieces.append(_subln(o1 - lam * o2, w_ref[...], lam_init))
            o_ref[0] = jnp.concatenate(pieces, axis=1).astype(o_ref.dtype)
        else:
            o_ref[0] = _heads_to_lanes(o, frames).astype(o_ref.dtype)


def _stick_step_kernel(q_ref, kc_ref, vc_ref, kn_ref, vn_ref, ones_ref, o_ref,
                       qs_ref, run_ref, acc_ref, *, frames):
    n_rows = N_STEP_HEADS * frames
    kt = pl.program_id(1)

    @pl.when(kt == 0)
    def _():
        _store_scaled_queries(q_ref, qs_ref)
        z = _step_scores(qs_ref, kn_ref)
        mask = _new_rows_mask(frames, n_rows, strict=True)
        a, run = _stick_weights(z, mask, jnp.zeros((n_rows, 1), F32), _strict_upper_ones(frames))
        acc_ref[...] = _step_pv(a, vn_ref, frames, False)
        run_ref[...] = run

    z = _step_scores(qs_ref, kc_ref)
    a, run = _stick_weights(z, None, run_ref[...], ones_ref[...])
    acc_ref[...] += _step_pv(a, vc_ref, frames, False)
    run_ref[...] = run

    @pl.when(kt == pl.num_programs(1) - 1)
    def _():
        o_ref[0] = _heads_to_lanes(acc_ref[...], frames).astype(o_ref.dtype)


def step_attention(layer, new, caches, cum_q_col, cum_cache_row, cum_new_row, bias_cache, bias_new,
                   lam_vec, subln_w, lam_init):
    batch, _, frames, _ = new["a"][0].shape
    width = N_STEP_HEADS * HEAD_DIM
    n_rows = N_STEP_HEADS * frames
    out_shape = jax.ShapeDtypeStruct((batch, frames, width), BF16)
    sem = ("parallel", "arbitrary")
    out_spec = pl.BlockSpec((1, frames, width), lambda b, kt: (b, 0, 0))
    q_spec = pl.BlockSpec((1, N_STEP_HEADS, frames, HEAD_DIM), lambda b, kt: (b, 0, 0, 0))
    new_t_spec = pl.BlockSpec((1, 1, N_STEP_HEADS, HEAD_DIM, frames), lambda b, kt: (0, b, 0, 0, 0))
    forward = lambda kt: kt

    def cache_t_spec(tk, tile_of):
        return pl.BlockSpec((1, 1, N_STEP_HEADS, HEAD_DIM, tk), lambda b, kt: (layer, b, 0, 0, tile_of(kt)))

    def softmax_call(mode, branch, kc, vc, extra_args, extra_specs, name):
        past = kc.shape[-1]
        tk = min(past, 4096)
        assert past % tk == 0
        q, k_new, v_new = new[branch]
        if mode == "A":
            vc_spec = pl.BlockSpec((1, 1, tk, 4, LANES), lambda b, kt: (layer, b, kt, 0, 0))
            vn_spec = pl.BlockSpec((1, 1, frames, 4, LANES), lambda b, kt: (0, b, 0, 0, 0))
            dv = LANES
        else:
            vc_spec, vn_spec, dv = cache_t_spec(tk, forward), new_t_spec, HEAD_DIM
        return pl.pallas_call(
            functools.partial(_softmax_step_kernel, mode=mode, frames=frames, lam_init=lam_init),
            out_shape=out_shape,
            grid=(batch, past // tk),
            in_specs=[q_spec, cache_t_spec(tk, forward), vc_spec, new_t_spec, vn_spec] + extra_specs(tk),
            out_specs=out_spec,
            scratch_shapes=[pltpu.VMEM((N_STEP_HEADS, frames, HEAD_DIM), BF16), pltpu.VMEM((n_rows, 1), F32),
                            pltpu.VMEM((n_rows, 1), F32), pltpu.VMEM((n_rows, dv), F32)],
            compiler_params=_cparams(sem),
            name=name,
        )(q, kc, vc, k_new[None], v_new[None], *extra_args)

    o_a = softmax_call(
        "A", "a", caches["a_k"], caches["a_v"], (lam_vec, subln_w.reshape(1, LANES)),
        lambda tk: [pl.BlockSpec((4, HEAD_DIM), lambda b, kt: (0, 0)),
                    pl.BlockSpec((1, LANES), lambda b, kt: (0, 0))],
        "diff_attention_step")
    o_b = softmax_call(
        "B", "b", caches["b_k"], caches["b_v"], (bias_cache, bias_new),
        lambda tk: [pl.BlockSpec((n_rows, tk), lambda b, kt: (0, kt)),
                    pl.BlockSpec((n_rows, frames), lambda b, kt: (0, 0))],
        "band_attention_step")
    o_c = softmax_call(
        "C", "c", caches["c_k"], caches["c_v"], (cum_q_col, cum_cache_row, cum_new_row),
        lambda tk: [pl.BlockSpec((1, n_rows, 1), lambda b, kt: (b, 0, 0)),
                    pl.BlockSpec((1, N_STEP_HEADS, tk), lambda b, kt: (b, 0, kt)),
                    pl.BlockSpec((1, N_STEP_HEADS, frames), lambda b, kt: (b, 0, 0))],
        "forgetting_attention_step")

    past = caches["d_k"].shape[-1]
    tk = min(past, 2048)
    assert past % tk == 0 and tk % SUFFIX_BLOCK == 0
    n_kt = past // tk
    backward = lambda kt: n_kt - 1 - kt
    ones_after = jnp.tril(jnp.ones((SUFFIX_BLOCK, SUFFIX_BLOCK), F32), -1).astype(BF16)
    q, k_new, v_new = new["d"]
    o_d = pl.pallas_call(
        functools.partial(_stick_step_kernel, frames=frames),
        out_shape=out_shape,
        grid=(batch, n_kt),
        in_specs=[q_spec, cache_t_spec(tk, backward), cache_t_spec(tk, backward), new_t_spec, new_t_spec,
                  pl.BlockSpec((SUFFIX_BLOCK, SUFFIX_BLOCK), lambda b, kt: (0, 0))],
        out_specs=out_spec,
        scratch_shapes=[pltpu.VMEM((N_STEP_HEADS, frames, HEAD_DIM), BF16), pltpu.VMEM((n_rows, 1), F32),
                        pltpu.VMEM((n_rows, HEAD_DIM), F32)],
        compiler_params=_cparams(sem),
        name="stick_breaking_attention_step",
    )(q, caches["d_k"], caches["d_v"], k_new[None], v_new[None], ones_after)
    return o_a, o_b, o_c, o_d


def _merge_kernel(x_ref, g0_ref, g1_ref, g2_ref, g3_ref, b0_ref, b1_ref, b2_ref, b3_ref, wb_ref, o_ref,
                  gs_ref, ws_ref):
    @pl.when(pl.program_id(1) == 0)
    def _():
        for n, g_ref in enumerate((g0_ref, g1_ref, g2_ref, g3_ref)):
            gs_ref[n] = g_ref[0].astype(BF16)
        ws_ref[...] = wb_ref[0].astype(BF16)

    x = x_ref[...]
    acc = None
    for n, b_ref in enumerate((b0_ref, b1_ref, b2_ref, b3_ref)):
        term = _sigmoid(_dot_nt(x, gs_ref[n])) * _dot(b_ref[...], ws_ref[n])
        acc = term if acc is None else acc + term
    o_ref[...] = acc.astype(o_ref.dtype)


def gated_merge(xn, w_in_t, gate_row0, w_branch, layer, branches):
    t, d = xn.shape
    wcol = 256
    tm = _row_tile(t, 1024)
    bw = branches[0].shape[1]
    gate_specs = [pl.BlockSpec((pl.Element(1), pl.Element(wcol), pl.Element(d)),
                               functools.partial(lambda c, m, n: (layer, pl.multiple_of(gate_row0 + n * d + c * wcol, 8), 0), n=n))
                  for n in range(N_BRANCH)]
    branch_specs = [pl.BlockSpec((tm, bw), lambda c, m: (m, 0)) for _ in range(N_BRANCH)]
    return pl.pallas_call(
        _merge_kernel,
        out_shape=jax.ShapeDtypeStruct((t, d), BF16),
        grid=(d // wcol, t // tm),
        in_specs=[pl.BlockSpec((tm, d), lambda c, m: (m, 0))] + gate_specs + branch_specs
                 + [pl.BlockSpec((1, N_BRANCH, bw, wcol), lambda c, m: (layer, 0, 0, c))],
        out_specs=pl.BlockSpec((tm, wcol), lambda c, m: (m, c)),
        scratch_shapes=[pltpu.VMEM((N_BRANCH, wcol, d), BF16), pltpu.VMEM((N_BRANCH, bw, wcol), BF16)],
        compiler_params=_cparams(("parallel", "arbitrary")),
        name="gated_merge",
    )(xn, w_in_t, w_in_t, w_in_t, w_in_t, *branches, w_branch)


def _outproj_kernel(a_ref, w_ref, x_ref, o_ref):
    o_ref[...] = x_ref[...] + _dot(a_ref[...], w_ref[0])


def out_projection(merged, w_out, layer, x):
    t, d = x.shape
    tm = _row_tile(t, 512)
    return pl.pallas_call(
        _outproj_kernel,
        out_shape=jax.ShapeDtypeStruct((t, d), F32),
        grid=(t // tm,),
        in_specs=[pl.BlockSpec((tm, d), lambda m: (m, 0)),
                  pl.BlockSpec((1, d, d), lambda m: (layer, 0, 0)),
                  pl.BlockSpec((tm, d), lambda m: (m, 0))],
        out_specs=pl.BlockSpec((tm, d), lambda m: (m, 0)),
        compiler_params=_cparams(("parallel",)),
        name="out_projection",
    )(merged, w_out, x)


def _router_kernel(h_ref, g_ref, w_ref, b_ref, xn_ref, comb_ref):
    h = h_ref[...]
    ms = jnp.mean(h * h, axis=-1, keepdims=True)
    xn = h * lax.rsqrt(ms + RMS_EPS) * g_ref[...]
    xn_ref[...] = xn.astype(xn_ref.dtype)
    logits = jnp.dot(xn, w_ref[...], preferred_element_type=F32, precision=lax.Precision.HIGHEST) + b_ref[...]
    lane = lax.broadcasted_iota(jnp.int32, logits.shape, 1).astype(F32)
    far = float(LANES)

    def first_argmax(vals):
        top = jnp.max(vals, axis=-1, keepdims=True)
        idx = jnp.min(jnp.where(vals == top, lane, far), axis=-1, keepdims=True)
        return top, idx

    group_logits = jnp.where(lane < N_GROUPS, logits, -jnp.inf)
    g_top, g_idx = first_argmax(group_logits)
    g_weight = 1.0 / jnp.sum(jnp.exp(group_logits - g_top), axis=-1, keepdims=True)
    first = N_GROUPS + EXPERTS_PER_GROUP * g_idx
    in_group = (lane >= first) & (lane < first + EXPERTS_PER_GROUP)
    expert_logits = jnp.where(in_group, logits, -jnp.inf)
    top1, idx1 = first_argmax(expert_logits)
    top2, idx2 = first_argmax(jnp.where(lane == idx1, -jnp.inf, expert_logits))
    w1 = 1.0 / (1.0 + jnp.exp(top2 - top1))
    w2 = jnp.exp(top2 - top1) * w1
    comb_ref[...] = g_weight * (jnp.where(lane == idx1, w1, 0.0) + jnp.where(lane == idx2, w2, 0.0))


def router(h, g_ffn, w_router, b_router):
    t, d = h.shape
    tm = _row_tile(t, 256)
    return pl.pallas_call(
        _router_kernel,
        out_shape=(jax.ShapeDtypeStruct((t, d), BF16), jax.ShapeDtypeStruct((t, LANES), F32)),
        grid=(t // tm,),
        in_specs=[pl.BlockSpec((tm, d), lambda m: (m, 0)),
                  pl.BlockSpec((1, d), lambda m: (0, 0)),
                  pl.BlockSpec((d, LANES), lambda m: (0, 0)),
                  pl.BlockSpec((1, LANES), lambda m: (0, 0))],
        out_specs=(pl.BlockSpec((tm, d), lambda m: (m, 0)),
                   pl.BlockSpec((tm, LANES), lambda m: (m, 0))),
        compiler_params=_cparams(("parallel",)),
        name="router",
    )(h, g_ffn.reshape(1, d), w_router, b_router)


EXPERTS_PER_STEP = 2


def _moe_kernel(x_ref, comb_ref, wg_ref, wu_ref, wd_ref, h_ref, o_ref):
    step = pl.program_id(1)

    @pl.when(step == 0)
    def _():
        o_ref[...] = h_ref[...]

    x = x_ref[...]
    comb = comb_ref[...]
    lane = lax.broadcasted_iota(jnp.int32, comb.shape, 1)
    total = None
    for i in range(EXPERTS_PER_STEP):
        e = step * EXPERTS_PER_STEP + i
        weight = jnp.sum(jnp.where(lane == N_GROUPS + e, comb, 0.0), axis=-1, keepdims=True)
        gate = _dot(x, wg_ref[0, i])
        hid = gate * _sigmoid(gate) * _dot(x, wu_ref[0, i]) * weight
        y = _dot(hid.astype(BF16), wd_ref[0, i])
        total = y if total is None else total + y
    o_ref[...] += total


def moe(xn, comb, w_gate, w_up, w_down, layer, h):
    t, d = xn.shape
    _, n_exp, _, f = w_gate.shape
    tm = _row_tile(t, 512)
    return pl.pallas_call(
        _moe_kernel,
        out_shape=jax.ShapeDtypeStruct((t, d), F32),
        grid=(t // tm, n_exp // EXPERTS_PER_STEP),
        in_specs=[pl.BlockSpec((tm, d), lambda m, e: (m, 0)),
                  pl.BlockSpec((tm, LANES), lambda m, e: (m, 0)),
                  pl.BlockSpec((1, EXPERTS_PER_STEP, d, f), lambda m, e: (layer, e, 0, 0)),
                  pl.BlockSpec((1, EXPERTS_PER_STEP, d, f), lambda m, e: (layer, e, 0, 0)),
                  pl.BlockSpec((1, EXPERTS_PER_STEP, f, d), lambda m, e: (layer, e, 0, 0)),
                  pl.BlockSpec((tm, d), lambda m, e: (m, 0))],
        out_specs=pl.BlockSpec((tm, d), lambda m, e: (m, 0)),
        compiler_params=_cparams(("parallel", "arbitrary")),
        name="moe",
    )(xn, comb, w_gate, w_up, w_down, h)


def _rope_tables(pos, width):
    half = HEAD_DIM // 2
    inv_freq = jnp.exp(jnp.arange(half, dtype=F32) * (-2.0 * math.log(ROPE_THETA) / HEAD_DIM))
    ang = pos.astype(F32)[:, None] * inv_freq[None, :]
    reps = width // half
    cos = jnp.tile(jnp.cos(ang), (1, reps))
    sin = jnp.tile(jnp.sin(ang), (1, reps))
    lower = (jnp.arange(width) % HEAD_DIM) < half
    return cos, jnp.where(lower[None, :], -sin, sin)


def _rel_bias_table(rel_bias, qpos, kpos):
    rel = jnp.clip(qpos[:, None] - kpos[None, :], -REL_CLIP, REL_CLIP) + REL_CLIP
    return rel_bias[:, rel].astype(F32)


def _band_bias_table(rel_bias, tile):
    period = 4 * tile + 1
    m = jnp.arange(period)
    diff = jnp.where(m < 3 * tile, m, m - period)
    vec = rel_bias[:, jnp.clip(LEFT_CHUNKS * CHUNK - diff, -REL_CLIP, REL_CLIP) + REL_CLIP].astype(F32)
    flat = jnp.tile(vec, (1, tile))[:, :tile * (period - 1)]
    return flat.reshape(vec.shape[0], tile, period - 1)[:, :, :3 * tile]


def _layer_weights(l, p):
    n_qkv = N_SEG * 512
    w_in = p["w_in"][l]
    n_fg = w_in.shape[1] - n_qkv - N_BRANCH * w_in.shape[0]
    w_f = jnp.pad(p["w_in_t"][l, n_qkv:n_qkv + n_fg], ((0, LANES - n_fg), (0, 0)))
    b_f = jnp.pad(p["b_forget"][l], (0, LANES - n_fg)).reshape(1, LANES)
    w_router = jnp.pad(jnp.concatenate([p["router_group_w"][l], p["router_expert_w"][l]], axis=1),
                       ((0, 0), (0, LANES - N_GROUPS - N_EXPERTS)))
    b_router = jnp.pad(jnp.concatenate([p["router_group_b"][l], p["router_expert_b"][l]]),
                       (0, LANES - N_GROUPS - N_EXPERTS)).reshape(1, LANES)
    return dict(
        w_f=w_f, b_f=b_f, n_fg=n_fg,
        gate_row0=n_qkv + n_fg, w_router=w_router, b_router=b_router,
    )


def _ffn(h, l, p, lw):
    xn2, comb = router(h, p["norm_ffn"][l], lw["w_router"], lw["b_router"])
    return moe(xn2, comb, p["e_gate"], p["e_up"], p["e_down"], l, h)


def _prompt_layer(x, l, p, lw, lam_init, batch, seq, tables, state_bufs):
    t, d = x.shape
    xn = rmsnorm(x, p["norm_mix"][l], BF16)
    qkv_bf, state_bufs = in_projection_prompt(xn, p["w_in_t"], l, tables["cos"], tables["sin"], batch, seq, state_bufs)
    width = qkv_bf.shape[-1]
    n_fg = lw["n_fg"]
    logf = log_forget(xn, lw["w_f"], lw["b_f"])[:, :n_fg].reshape(batch, seq, n_fg)
    cum_row = cumsum_lanes(jnp.swapaxes(logf, 1, 2))
    cum_col = jnp.swapaxes(cum_row.reshape(batch, n_fg // 2, 2, seq), 2, 3)
    branches = prompt_attention(
        qkv_bf.reshape(N_SEG, batch, seq, width), cum_col, tables["band_bias"][l],
        p["diff_lambda"][l], p["diff_subln"][l], lam_init, batch, seq)
    merged = gated_merge(xn, p["w_in_t"], lw["gate_row0"], p["w_branch"], l, [b.reshape(t, width) for b in branches])
    h = out_projection(merged, p["w_out_bf"], l, x)
    h = _ffn(h, l, p, lw)
    return h, state_bufs, logf


def _step_layer(x, l, p, lw, lam_init, batch, frames, caches, tables):
    t, d = x.shape
    xn = rmsnorm(x, p["norm_mix"][l], BF16)
    qkv, _ = in_projection(xn, p["w_in_t"], l, tables["cos"], tables["sin"], frames)
    width = qkv.shape[-1]
    n_fg = lw["n_fg"]
    logf = log_forget(xn, lw["w_f"], lw["b_f"])[:, :n_fg].reshape(batch, frames, n_fg)
    past_f = caches["c_logf"][l]
    past = past_f.shape[1]
    total = past + frames
    padded = -(-total // LANES) * LANES
    series = jnp.concatenate([past_f, logf], axis=1)
    cum = cumsum_lanes(jnp.pad(jnp.swapaxes(series, 1, 2), ((0, 0), (0, 0), (0, padded - total))))
    cum_cache_row = cum[:, :, :past]
    cum_new_row = cum[:, :, past:total]
    cum_q_col = cum_new_row.reshape(batch, n_fg * frames, 1)
    heads = lambda i: qkv[i].reshape(batch, frames, N_STEP_HEADS, HEAD_DIM)
    q_heads = lambda i: jnp.transpose(heads(i), (0, 2, 1, 3))
    seq_minor = lambda i: jnp.transpose(heads(i), (0, 2, 3, 1))
    new = dict(a=(q_heads(0), seq_minor(1), qkv[2].reshape(batch, frames, 4, LANES)),
               b=(q_heads(3), seq_minor(4), seq_minor(5)),
               c=(q_heads(6), seq_minor(7), seq_minor(8)),
               d=(q_heads(9), seq_minor(10), seq_minor(11)))
    branches = step_attention(
        l, new, caches, cum_q_col, cum_cache_row, cum_new_row,
        tables["bias_cache"][l], tables["bias_new"][l], p["diff_lambda"][l], p["diff_subln"][l], lam_init)
    merged = gated_merge(xn, p["w_in_t"], lw["gate_row0"], p["w_branch"], l, [b.reshape(t, width) for b in branches])
    h = out_projection(merged, p["w_out_bf"], l, x)
    h = _ffn(h, l, p, lw)
    seg = lambda i: qkv[i].reshape(batch, frames, width)
    state = (seg(1), seg(2), seg(4), seg(5), seg(7), seg(8), logf, seg(10), seg(11))
    return h, state


def kernel(x_prompt, x_sample, cache_a_k, cache_a_v, cache_b_k, cache_b_v, cache_c_k, cache_c_v, cache_c_logf, cache_d_k, cache_d_v, norm_mix, w_in, b_forget, diff_lambda, diff_subln, rel_bias, w_branch, w_out, norm_ffn, router_group_w, router_group_b, router_expert_w, router_expert_b, expert_w_gate, expert_w_up, expert_w_down, norm_final):
    p = dict(norm_mix=norm_mix, w_in=w_in, w_in_t=jnp.swapaxes(w_in, 1, 2), b_forget=b_forget, diff_lambda=diff_lambda, diff_subln=diff_subln,
             w_branch=w_branch, w_out=w_out, norm_ffn=norm_ffn, router_group_w=router_group_w,
             router_group_b=router_group_b, router_expert_w=router_expert_w, router_expert_b=router_expert_b,
             w_out_bf=w_out.astype(BF16), e_gate=expert_w_gate.astype(BF16), e_up=expert_w_up.astype(BF16),
             e_down=expert_w_down.astype(BF16))
    depth = w_in.shape[0]
    bp, sp, d = x_prompt.shape
    bs, fs, _ = x_sample.shape
    past = cache_a_k.shape[2]
    width = 512
    n_heads = width // HEAD_DIM

    def seq_minor(c):
        c = jnp.moveaxis(c, 2, -1)
        return c.reshape(c.shape[0], c.shape[1], N_STEP_HEADS, HEAD_DIM, c.shape[-1])

    caches = dict(a_k=seq_minor(cache_a_k), a_v=cache_a_v, b_k=seq_minor(cache_b_k), b_v=seq_minor(cache_b_v),
                  c_k=seq_minor(cache_c_k), c_v=seq_minor(cache_c_v), c_logf=cache_c_logf,
                  d_k=seq_minor(cache_d_k), d_v=seq_minor(cache_d_v))

    pos_p = jnp.arange(sp, dtype=jnp.int32)
    pos_s = past + jnp.arange(fs, dtype=jnp.int32)
    cos_p, sin_p = _rope_tables(pos_p, width)
    cos_s, sin_s = _rope_tables(pos_s, width)
    tile = 256
    band_bias = jnp.stack([_band_bias_table(rel_bias[l], tile).reshape(n_heads // 2, 2, tile, 3 * tile)
                           for l in range(depth)])
    wb = cache_b_k.shape[2]
    kpos_b = jnp.arange(past - wb, past + fs, dtype=jnp.int32)
    step_bias = jnp.stack([_rel_bias_table(rel_bias[l], pos_s, kpos_b).reshape(n_heads * fs, wb + fs)
                           for l in range(depth)])
    tables_p = dict(cos=cos_p, sin=sin_p, band_bias=band_bias)
    tables_s = dict(cos=cos_s, sin=sin_s, bias_cache=step_bias[:, :, :wb], bias_new=step_bias[:, :, wb:])

    hp = x_prompt.reshape(bp * sp, d)
    hs = x_sample.reshape(bs * fs, d)
    bufs_p, logf_p, st_s = prompt_state_buffers(depth, bp, sp), [], []
    for l in range(depth):
        lam_init = 0.8 - 0.6 * math.exp(-0.3 * l)
        lw = _layer_weights(l, p)
        hp, bufs_p, logf = _prompt_layer(hp, l, p, lw, lam_init, bp, sp, tables_p, bufs_p)
        hs, state_s = _step_layer(hs, l, p, lw, lam_init, bs, fs, caches, tables_s)
        logf_p.append(logf)
        st_s.append(state_s)
    y_prompt = rmsnorm(hp, norm_final, F32).reshape(bp, sp, d)
    y_sample = rmsnorm(hs, norm_final, F32).reshape(bs, fs, d)

    h_a = width // (2 * HEAD_DIM)
    state_shapes = lambda b, s, sb: (
        (b, s, h_a, 2, HEAD_DIM), (b, s, h_a, 2 * HEAD_DIM), (b, sb, n_heads, HEAD_DIM), (b, sb, n_heads, HEAD_DIM),
        (b, s, n_heads, HEAD_DIM), (b, s, n_heads, HEAD_DIM), (b, s, n_heads), (b, s, n_heads, HEAD_DIM),
        (b, s, n_heads, HEAD_DIM))

    def stack(states, shapes):
        return tuple(jnp.stack([st[i].reshape(shapes[i]) for st in states], axis=0) for i in range(len(shapes)))

    token_major = lambda buf, dims: jnp.moveaxis(buf.reshape(buf.shape[:2] + dims + buf.shape[-1:]), -1, 2)
    a_k, a_v, b_k, b_v, c_k, c_v, d_k, d_v = bufs_p
    per_head = (n_heads, HEAD_DIM)
    out_p = (token_major(a_k, (h_a, 2, HEAD_DIM)), a_v, token_major(b_k, per_head), token_major(b_v, per_head),
             token_major(c_k, per_head), token_major(c_v, per_head), jnp.stack(logf_p, axis=0),
             token_major(d_k, per_head), token_major(d_v, per_head))
    out_s = stack(st_s, state_shapes(bs, fs, fs))
    return (y_prompt, y_sample) + out_p + out_s
```
